```python
import jax, jax.numpy as jnp
from jax import lax
import numpy as np

D_MODEL = 2048
BATCH = 2
SEQ = 4096
DEPTH = 1

GRID_W = 64
CTX_LEN = 256
SSD_WIDTH = 1024
SSD_HEADS = 16
SSD_HEAD_DIM = SSD_WIDTH // SSD_HEADS
SSD_GROUPS = 2
D_STATE = 128
SSD_CHUNK = 128
D_CONV = 4
CONV_DIM = SSD_WIDTH + 2 * SSD_GROUPS * D_STATE
CMLP_WIDTH = 1024
CMLP_HEADS = 8
CMLP_HEAD_DIM = CMLP_WIDTH // CMLP_HEADS
CMLP_CHUNK = 128
ROWS_PER_CHUNK = CMLP_CHUNK // GRID_W
MIX_WIDTH = SSD_WIDTH + CMLP_WIDTH
OFF_Z = SSD_WIDTH
OFF_XBC = OFF_Z + CONV_DIM
OFF_DTF = OFF_XBC + SSD_HEADS
OFF_DTB = OFF_DTF + SSD_HEADS
OFF_U = OFF_DTB + CMLP_WIDTH
IN_PROJ_DIM = OFF_U + CMLP_WIDTH
IN_SPLITS = [OFF_Z, OFF_XBC, OFF_DTF, OFF_DTB, OFF_U]
N_EXPERTS = 256
TOP_K = 8
N_ROUTE_GROUPS = 8
TOPK_ROUTE_GROUPS = 4
D_EXPERT = 512
ROUTED_SCALE = 2.5
MOE_BLOCK = 64
N_MOD = 6
EPS = 1e-6

kernel_name = "hymba_ssd_chunkmlp_moe_prefix_dit_block"


def rmsnorm(x, w):
    xf = x.astype(jnp.float32)
    y = xf * lax.rsqrt(jnp.mean(xf * xf, axis=-1, keepdims=True) + EPS)
    return (y * w.astype(jnp.float32)).astype(x.dtype)


def modulate(h, shift, scale):
    return h * (1 + scale) + shift


def dwconv_silu(u, w, b):
    pad = ((D_CONV - 1) // 2, D_CONV // 2)
    y = lax.conv_general_dilated(u, w[:, None, :].astype(u.dtype), window_strides=(1,), padding=[pad],
                                 dimension_numbers=('NWC', 'WIO', 'NWC'), feature_group_count=u.shape[-1])
    return jax.nn.silu(y + b.astype(u.dtype))


def ssd_scan(xh, dt, a, bh, ch, h0, with_output):
    bsz, seq, heads, hdim = xh.shape
    nc = seq // SSD_CHUNK
    q = SSD_CHUNK
    xdt = (xh.astype(jnp.float32) * dt[..., None]).reshape(bsz, nc, q, heads, hdim)
    bc = bh.astype(jnp.float32).reshape(bsz, nc, q, heads, D_STATE)
    acs = jnp.cumsum((dt * a).reshape(bsz, nc, q, heads), axis=2)
    decay_to_end = jnp.exp(acs[:, :, -1:, :] - acs)
    chunk_states = jnp.einsum('bcshn,bcsh,bcshp->bchpn', bc, decay_to_end, xdt)
    chunk_decay = jnp.exp(acs[:, :, -1, :])

    def step(h, inp):
        s_c, d_c = inp
        return h * d_c[:, :, None, None] + s_c, h

    h_final, h_in = lax.scan(step, h0, (jnp.moveaxis(chunk_states, 1, 0), jnp.moveaxis(chunk_decay, 1, 0)))
    if not with_output:
        return None, h_final
    cc = ch.astype(jnp.float32).reshape(bsz, nc, q, heads, D_STATE)
    lower = jnp.tril(jnp.ones((q, q), dtype=bool))[None, None, :, :, None]
    seg = acs[:, :, :, None, :] - acs[:, :, None, :, :]
    decay = jnp.exp(jnp.where(lower, seg, -jnp.inf))
    scores = jnp.einsum('bclhn,bcshn->bclsh', cc, bc) * decay
    y = (jnp.einsum('bclsh,bcshp->bclhp', scores, xdt)
         + jnp.einsum('bclhn,bchpn,bclh->bclhp', cc, jnp.moveaxis(h_in, 0, 1), jnp.exp(acs)))
    return y.reshape(bsz, seq, heads, hdim), h_final


def ssd_branch(p, h0_f, h0_b, conv_w, conv_b, dt_bias_f, dt_bias_b, a_log_f, a_log_b, d_skip, norm_w, with_output):
    z, xbc, dtr_f, dtr_b, _ = jnp.split(p, IN_SPLITS, axis=-1)[:5]
    bsz, seq, _ = xbc.shape
    xbc = dwconv_silu(xbc, conv_w, conv_b)
    xs, bs, cs = jnp.split(xbc, [SSD_WIDTH, SSD_WIDTH + SSD_GROUPS * D_STATE], axis=-1)
    rep = SSD_HEADS // SSD_GROUPS
    xh = xs.reshape(bsz, seq, SSD_HEADS, SSD_HEAD_DIM)
    bh = jnp.repeat(bs.reshape(bsz, seq, SSD_GROUPS, D_STATE), rep, axis=2)
    ch = jnp.repeat(cs.reshape(bsz, seq, SSD_GROUPS, D_STATE), rep, axis=2)
    dt_f = jax.nn.softplus(dtr_f.astype(jnp.float32) + dt_bias_f.astype(jnp.float32))
    dt_b = jax.nn.softplus(dtr_b.astype(jnp.float32) + dt_bias_b.astype(jnp.float32))
    a_f = -jnp.exp(a_log_f.astype(jnp.float32))
    a_b = -jnp.exp(a_log_b.astype(jnp.float32))
    flip = lambda t: jnp.flip(t, axis=1)
    y_f, hf = ssd_scan(xh, dt_f, a_f, bh, ch, h0_f, with_output)
    y_b, hb = ssd_scan(flip(xh), flip(dt_b), a_b, flip(bh), flip(ch), h0_b, with_output)
    if not with_output:
        return None, hf, hb
    y = y_f + flip(y_b) + d_skip.astype(jnp.float32)[:, None] * xh.astype(jnp.float32)
    y = y.reshape(bsz, seq, SSD_WIDTH) * jax.nn.silu(z.astype(jnp.float32))
    return rmsnorm(y, norm_w).astype(p.dtype), hf, hb


def cmlp_branch(p, n_chunks, norm_w, w_s, b_s):
    u, v = jnp.split(p, IN_SPLITS, axis=-1)[4:]
    u = jax.nn.gelu(u)
    v = jax.nn.gelu(v)
    bsz, seq, _ = v.shape
    v = rmsnorm(v, norm_w).reshape(bsz, n_chunks, CMLP_CHUNK, CMLP_HEADS, CMLP_HEAD_DIM)
    mixed = jnp.einsum('hts,bcshd->bcthd', w_s, v) + jnp.swapaxes(b_s, 0, 1)[:, :, None]
    return u * mixed.reshape(bsz, seq, CMLP_WIDTH)


def moe_ffn(h, w_router, router_bias, w_gate_up, w_down, w_sh_gate_up, w_sh_down):
    shp = h.shape
    xt = h.reshape(-1, shp[-1])
    n_tok = xt.shape[0]
    scores = jax.nn.sigmoid((xt @ w_router).astype(jnp.float32))
    biased = scores + router_bias.astype(jnp.float32)
    grp = biased.reshape(n_tok, N_ROUTE_GROUPS, N_EXPERTS // N_ROUTE_GROUPS)
    grp_score = lax.top_k(grp, 2)[0].sum(-1)
    top_grp = lax.top_k(grp_score, TOPK_ROUTE_GROUPS)[1]
    grp_mask = jnp.any(top_grp[:, :, None] == jnp.arange(N_ROUTE_GROUPS)[None, None, :], axis=1)
    exp_mask = jnp.repeat(grp_mask, N_EXPERTS // N_ROUTE_GROUPS, axis=1)
    idx = lax.top_k(jnp.where(exp_mask, biased, -jnp.inf), TOP_K)[1]
    gates = jnp.take_along_axis(scores, idx, axis=1)
    gates = gates / jnp.sum(gates, axis=-1, keepdims=True) * ROUTED_SCALE

    n_assign = n_tok * TOP_K
    e_flat = idx.reshape(-1)
    tok_flat = jnp.repeat(jnp.arange(n_tok, dtype=jnp.int32), TOP_K)
    g_flat = gates.reshape(-1)
    order = jnp.argsort(e_flat)
    e_sorted = e_flat[order]
    counts = jnp.zeros(N_EXPERTS, jnp.int32).at[e_flat].add(1)
    padded = (counts + MOE_BLOCK - 1) // MOE_BLOCK * MOE_BLOCK
    starts = jnp.cumsum(counts) - counts
    pends = jnp.cumsum(padded)
    pstarts = pends - padded
    dest = pstarts[e_sorted] + jnp.arange(n_assign, dtype=jnp.int32) - starts[e_sorted]
    n_blocks = -(-n_assign // MOE_BLOCK) + N_EXPERTS
    n_slots = n_blocks * MOE_BLOCK
    slot_tok = jnp.zeros(n_slots, jnp.int32).at[dest].set(tok_flat[order])
    slot_gate = jnp.zeros(n_slots, jnp.float32).at[dest].set(g_flat[order])
    block_start = jnp.arange(n_blocks, dtype=jnp.int32) * MOE_BLOCK
    block_exp = jnp.minimum(jnp.searchsorted(pends, block_start, side='right'), N_EXPERTS - 1)

    def expert_block(args):
        e, toks, g = args
        xb = xt[toks]
        gt, up = jnp.split(xb @ w_gate_up[e], 2, axis=-1)
        return ((jax.nn.silu(gt) * up) @ w_down[e]) * g[:, None].astype(xb.dtype)

    y_blocks = lax.map(expert_block, (block_exp, slot_tok.reshape(n_blocks, MOE_BLOCK),
                                      slot_gate.reshape(n_blocks, MOE_BLOCK)))
    routed = jax.ops.segment_sum(y_blocks.reshape(n_slots, -1), slot_tok, num_segments=n_tok)
    sg, su = jnp.split(xt @ w_sh_gate_up, 2, axis=-1)
    shared = (jax.nn.silu(sg) * su) @ w_sh_down
    return (routed + shared).reshape(shp)


def setup_inputs(seed: int = 0) -> dict:
    key = jax.random.key(seed)
    ks = jax.random.split(key, 32)
    nrm = lambda k, shape, s: jax.random.normal(k, shape, jnp.float32) * s
    L, D = DEPTH, D_MODEL
    dt0 = jnp.exp(jax.random.uniform(ks[10], (L, SSD_HEADS), jnp.float32, np.log(1e-3), np.log(1e-1)))
    dt1 = jnp.exp(jax.random.uniform(ks[11], (L, SSD_HEADS), jnp.float32, np.log(1e-3), np.log(1e-1)))
    inv_softplus = lambda d: d + jnp.log(-jnp.expm1(-d))
    return {
        "x": nrm(ks[0], (BATCH, SEQ, D), 1.0),
        "c": nrm(ks[1], (BATCH, D), 1.0),
        "ctx": nrm(ks[2], (BATCH, CTX_LEN, D), 1.0),
        "c_ctx": nrm(ks[3], (D,), 1.0),
        "w_mod": nrm(ks[4], (L, D, N_MOD * D), 0.5 * D ** -0.5),
        "b_mod": nrm(ks[5], (L, N_MOD * D), 0.02),
        "norm1_w": 1.0 + nrm(ks[6], (L, D), 0.02),
        "w_in": nrm(ks[7], (L, D, IN_PROJ_DIM), D ** -0.5),
        "conv_w": nrm(ks[8], (L, D_CONV, CONV_DIM), D_CONV ** -0.5),
        "conv_b": nrm(ks[9], (L, CONV_DIM), 0.02),
        "dt_bias_f": inv_softplus(dt0),
        "dt_bias_b": inv_softplus(dt1),
        "a_log_f": jnp.log(jax.random.uniform(ks[12], (L, SSD_HEADS), jnp.float32, 1.0, 16.0)),
        "a_log_b": jnp.log(jax.random.uniform(ks[13], (L, SSD_HEADS), jnp.float32, 1.0, 16.0)),
        "d_skip": 1.0 + nrm(ks[14], (L, SSD_HEADS), 0.1),
        "ssd_norm_w": 1.0 + nrm(ks[15], (L, SSD_WIDTH), 0.02),
        "cmlp_norm_w": 1.0 + nrm(ks[16], (L, CMLP_WIDTH), 0.02),
        "w_spatial": nrm(ks[17], (L, CMLP_HEADS, CMLP_CHUNK, CMLP_CHUNK), CMLP_CHUNK ** -0.5),
        "b_spatial": 1.0 + nrm(ks[18], (L, CMLP_HEADS, CMLP_CHUNK), 0.1),
        "w_out": nrm(ks[19], (L, MIX_WIDTH, D), MIX_WIDTH ** -0.5),
        "norm2_w": 1.0 + nrm(ks[20], (L, D), 0.02),
        "w_router": nrm(ks[21], (L, D, N_EXPERTS), D ** -0.5),
        "router_bias": nrm(ks[22], (L, N_EXPERTS), 0.01),
        "w_gate_up": nrm(ks[23], (L, N_EXPERTS, D, 2 * D_EXPERT), D ** -0.5),
        "w_down": nrm(ks[24], (L, N_EXPERTS, D_EXPERT, D), D_EXPERT ** -0.5),
        "w_shared_gate_up": nrm(ks[25], (L, D, 2 * D_EXPERT), D ** -0.5),
        "w_shared_down": nrm(ks[26], (L, D_EXPERT, D), D_EXPERT ** -0.5),
        "final_norm_w": 1.0 + nrm(ks[27], (D,), 0.02),
    }


def reference(x, c, ctx, c_ctx, w_mod, b_mod, norm1_w, w_in, conv_w, conv_b, dt_bias_f, dt_bias_b,
              a_log_f, a_log_b, d_skip, ssd_norm_w, cmlp_norm_w, w_spatial, b_spatial, w_out, norm2_w,
              w_router, router_bias, w_gate_up, w_down, w_shared_gate_up, w_shared_down, final_norm_w):
    bsz, seq, _ = x.shape
    rows = seq // GRID_W
    n_chunks_lat = rows // ROWS_PER_CHUNK
    n_chunks_ctx = ctx.shape[1] // CMLP_CHUNK
    h_zero = jnp.zeros((bsz, SSD_HEADS, SSD_HEAD_DIM, D_STATE), jnp.float32)
    for i in range(DEPTH):
        last = i == DEPTH - 1
        mod_l = jnp.split((jax.nn.silu(c) @ w_mod[i] + b_mod[i])[:, None, :], N_MOD, axis=-1)
        mod_c = jnp.split(jax.nn.silu(c_ctx) @ w_mod[i] + b_mod[i], N_MOD, axis=-1)
        ssd_params = (conv_w[i], conv_b[i], dt_bias_f[i], dt_bias_b[i], a_log_f[i], a_log_b[i],
                      d_skip[i], ssd_norm_w[i])
        p_c = modulate(rmsnorm(ctx, norm1_w[i]), mod_c[0], mod_c[1]) @ w_in[i]
        p_l = modulate(rmsnorm(x, norm1_w[i]), mod_l[0], mod_l[1]) @ w_in[i]
        y_ssd_c, hf_c, hb_c = ssd_branch(p_c, h_zero, h_zero, *ssd_params, with_output=not last)
        y_ssd_l, _, _ = ssd_branch(p_l, hf_c, hb_c, *ssd_params, with_output=True)
        y_cm_l = cmlp_branch(p_l, n_chunks_lat, cmlp_norm_w[i], w_spatial[i], b_spatial[i])
        x = x + mod_l[2] * (jnp.concatenate([y_ssd_l, y_cm_l], axis=-1) @ w_out[i])
        h2 = modulate(rmsnorm(x, norm2_w[i]), mod_l[3], mod_l[4])
        x = x + mod_l[5] * moe_ffn(h2, w_router[i], router_bias[i], w_gate_up[i], w_down[i],
                                   w_shared_gate_up[i], w_shared_down[i])
        if not last:
            y_cm_c = cmlp_branch(p_c, n_chunks_ctx, cmlp_norm_w[i], w_spatial[i], b_spatial[i])
            ctx = ctx + mod_c[2] * (jnp.concatenate([y_ssd_c, y_cm_c], axis=-1) @ w_out[i])
            h2c = modulate(rmsnorm(ctx, norm2_w[i]), mod_c[3], mod_c[4])
            ctx = ctx + mod_c[5] * moe_ffn(h2c, w_router[i], router_bias[i], w_gate_up[i], w_down[i],
                                           w_shared_gate_up[i], w_shared_down[i])
    return rmsnorm(x, final_norm_w)
```

```python
import functools

import jax
import jax.numpy as jnp
from jax import lax
from jax.experimental import pallas as pl
from jax.experimental.pallas import tpu as pltpu

f32 = jnp.float32
bf16 = jnp.bfloat16
i32 = jnp.int32

D = 2048
NB_BATCH = 2
SEQ = 4096
CTX = 256
N_MOD = 6
SSD_W = 1024
SSD_H = 16
SSD_P = 64
SSD_G = 2
D_STATE = 128
CHUNK = 128
D_CONV = 4
BC_W = 2 * SSD_G * D_STATE
CM_W = 1024
CM_H = 8
CM_HD = 128
OFF_Z = SSD_W
OFF_XBC = OFF_Z + SSD_W + BC_W
OFF_DTF = OFF_XBC + SSD_H
OFF_DTB = OFF_DTF + SSD_H
OFF_U = OFF_DTB + CM_W
IN_PROJ_DIM = OFF_U + CM_W
N_EXP = 256
TOP_K = 8
N_GRP = 8
TOP_GRP = 4
GRP_SZ = N_EXP // N_GRP
D_EXP = 512
ROUTED_SCALE = 2.5
EPS = 1e-6

P_MAIN = 4 * 1024 + BC_W
DT_W = 128

LANES = 128
SUBLANES = 8
VMEM_LIMIT = 56 * 1024 * 1024

TM_INPROJ = 1024
TN_INPROJ = 512
TM_OUT = 256
TM_SHARED = 512
TT_ROUTE = 512
ROWS_PER_BLOCK = 128
N_TOK = NB_BATCH * SEQ
N_BLOCKS = N_TOK * TOP_K // ROWS_PER_BLOCK + N_EXP
TT_DISP = 64
TT_COMB = 64


def _cparams(sem):
    return pltpu.CompilerParams(dimension_semantics=sem, vmem_limit_bytes=VMEM_LIMIT)


def _sigmoid(x):
    return 1.0 / (1.0 + jnp.exp(-x))


def _silu(x):
    return x * _sigmoid(x)


def _dot(a, b):
    return jnp.dot(a, b, preferred_element_type=f32)


def _mod_kernel(c_ref, w_ref, b_ref, o_ref):
    a = _silu(c_ref[...]).astype(bf16)
    o_ref[...] = _dot(a, w_ref[...].astype(bf16)) + b_ref[...]


def _mod_call(cs, w_mod, b_mod):
    tn = 1024
    n = w_mod.shape[1]
    return pl.pallas_call(
        _mod_kernel,
        grid=(n // tn,),
        in_specs=[
            pl.BlockSpec((SUBLANES, D), lambda j: (0, 0)),
            pl.BlockSpec((D, tn), lambda j: (0, j)),
            pl.BlockSpec((1, tn), lambda j: (0, j)),
        ],
        out_specs=pl.BlockSpec((SUBLANES, tn), lambda j: (0, j)),
        out_shape=jax.ShapeDtypeStruct((SUBLANES, n), f32),
        compiler_params=_cparams(("arbitrary",)),
        name="mod",
    )(cs, w_mod, b_mod)


def _inproj_kernel(x_ref, sh_ref, sc_ref, nw_ref, w_ref, wdt_ref, p_ref, dt_ref, h_scr):
    @pl.when(pl.program_id(1) == 0)
    def _():
        x = x_ref[...]
        y = x * lax.rsqrt(jnp.mean(x * x, axis=-1, keepdims=True) + EPS) * nw_ref[...]
        h = (y * (1.0 + sc_ref[0, 0]) + sh_ref[0, 0]).astype(bf16)
        h_scr[...] = h
        dt_ref[...] = _dot(h, wdt_ref[...])

    p_ref[...] = _dot(h_scr[...], w_ref[...])


def _inproj_call(x2d, mod6, norm_w, w_main, w_dt, mod_row_fn):
    m = x2d.shape[0]
    tm, tn = min(TM_INPROJ, m), TN_INPROJ
    return pl.pallas_call(
        _inproj_kernel,
        grid=(m // tm, P_MAIN // tn),
        in_specs=[
            pl.BlockSpec((tm, D), lambda i, j: (i, 0)),
            pl.BlockSpec((1, 1, 1, D), lambda i, j: (mod_row_fn(i), 0, 0, 0)),
            pl.BlockSpec((1, 1, 1, D), lambda i, j: (mod_row_fn(i), 1, 0, 0)),
            pl.BlockSpec((1, D), lambda i, j: (0, 0)),
            pl.BlockSpec((D, tn), lambda i, j: (0, j)),
            pl.BlockSpec((D, DT_W), lambda i, j: (0, 0)),
        ],
        out_specs=[
            pl.BlockSpec((tm, tn), lambda i, j: (i, j)),
            pl.BlockSpec((tm, DT_W), lambda i, j: (i, 0)),
        ],
        out_shape=[
            jax.ShapeDtypeStruct((m, P_MAIN), f32),
            jax.ShapeDtypeStruct((m, DT_W), f32),
        ],
        scratch_shapes=[pltpu.VMEM((tm, D), bf16)],
        compiler_params=_cparams(("arbitrary", "arbitrary")),
        name="inproj",
    )(x2d, mod6, mod6, norm_w, w_main, w_dt)


def _split3(q):
    hi = q.astype(bf16)
    r1 = q - hi.astype(f32)
    mid = r1.astype(bf16)
    lo = (r1 - mid.astype(f32)).astype(bf16)
    return hi, mid, lo


def _split2_cat(q):
    hi = q.astype(bf16)
    lo = (q - hi.astype(f32)).astype(bf16)
    return jnp.concatenate([hi, lo], axis=1)


def _conv_silu(buf, w_ref, b_ref):
    w = w_ref[...]
    y = (w[0:1] * buf[7:135, :] + w[1:2] * buf[8:136, :] + w[2:3] * buf[9:137, :]
         + w[3:4] * buf[10:138, :] + b_ref[...])
    return _silu(y)


def _ssd_chunk(ph, first, last, xm_ref, xp_ref, xn_ref, bm_ref, bp_ref, bn_ref, dt_ref,
               cwx_ref, cbx_ref, cwb_ref, cbb_ref, dtb_ref, alog_ref, e2_ref,
               s_scr, xbuf, bcbuf, with_output):
    zero8x = jnp.zeros((SUBLANES, SSD_W), f32)
    zero8b = jnp.zeros((SUBLANES, BC_W), f32)
    xbuf[0:8, :] = jnp.where(first, zero8x, xp_ref[...])
    xbuf[8:136, :] = xm_ref[...]
    xbuf[136:144, :] = jnp.where(last, zero8x, xn_ref[...])
    bcbuf[0:8, :] = jnp.where(first, zero8b, bp_ref[...])
    bcbuf[8:136, :] = bm_ref[...]
    bcbuf[136:144, :] = jnp.where(last, zero8b, bn_ref[...])
    xs = _conv_silu(xbuf, cwx_ref, cbx_ref)
    bcv = _conv_silu(bcbuf, cwb_ref, cbb_ref)

    dtr = dt_ref[...]
    dtr = jnp.where(ph == 0, dtr, pltpu.roll(dtr, LANES - SSD_H, axis=1))
    pre = dtr + dtb_ref[0]
    dt = jnp.maximum(pre, 0.0) + jnp.log(1.0 + jnp.exp(-jnp.abs(pre)))
    d_a = dt * (-jnp.exp(alog_ref[0]))

    row = lax.broadcasted_iota(i32, (CHUNK, CHUNK), 0)
    col = lax.broadcasted_iota(i32, (CHUNK, CHUNK), 1)
    mask = jnp.where(ph == 0, row - col, col - row) >= 0
    tri = mask.astype(f32).astype(bf16)
    hi, mid, lo = _split3(d_a)
    acs = _dot(tri, hi) + _dot(tri, mid) + _dot(tri, lo)
    tot = jnp.sum(d_a, axis=0, keepdims=True)
    e_a = jnp.exp(acs)
    dte = jnp.exp(tot - acs)
    cd = jnp.broadcast_to(jnp.exp(tot), (SUBLANES, LANES))
    ex = _dot(jnp.concatenate([_split2_cat(dt), _split2_cat(e_a), _split2_cat(dte), _split2_cat(cd)], axis=0),
              e2_ref[...])
    dt_e = ex[0:128]
    ea_e = ex[128:256]
    dte_e = ex[256:384]
    cd_e = ex[384:385]
    xdt = xs * dt_e

    y_parts = []
    if with_output:
        acs_t = acs.T
        lane_lo = col < SSD_P
        g_mats = []
        for g in range(SSD_G):
            b_g = bcv[:, g * D_STATE:(g + 1) * D_STATE].astype(bf16)
            c_g = bcv[:, (SSD_G + g) * D_STATE:(SSD_G + g + 1) * D_STATE].astype(bf16)
            g_mats.append(lax.dot_general(c_g, b_g, (((1,), (1,)), ((), ())), preferred_element_type=f32))
        for p in range(SSD_H // 2):
            g = (2 * p) // (SSD_H // SSD_G)
            xp = xdt[:, p * LANES:(p + 1) * LANES]
            acc = None
            for k in range(2):
                h = 2 * p + k
                seg = jnp.broadcast_to(acs[:, h:h + 1], (CHUNK, CHUNK)) - jnp.broadcast_to(acs_t[h:h + 1, :], (CHUNK, CHUNK))
                dec = jnp.where(mask, jnp.exp(jnp.minimum(seg, 0.0)), 0.0)
                m_h = (g_mats[g] * dec).astype(bf16)
                x_h = jnp.where(lane_lo if k == 0 else jnp.logical_not(lane_lo), xp, 0.0).astype(bf16)
                t = _dot(m_h, x_h)
                acc = t if acc is None else acc + t
            y_parts.append(acc)

    gw = SSD_W // SSD_G
    y_off = []
    for g in range(SSD_G):
        sl = slice(g * gw, (g + 1) * gw)
        b_g = bcv[:, g * D_STATE:(g + 1) * D_STATE].astype(bf16)
        s_g = s_scr[g]
        if with_output:
            c_g = bcv[:, (SSD_G + g) * D_STATE:(SSD_G + g + 1) * D_STATE].astype(bf16)
            y_off.append(_dot(c_g, s_g.astype(bf16)) * ea_e[:, sl])
        upd = lax.dot_general(b_g, (xdt[:, sl] * dte_e[:, sl]).astype(bf16), (((0,), (0,)), ((), ())),
                              preferred_element_type=f32)
        s_scr[g] = s_g * cd_e[:, sl] + upd
    if not with_output:
        return xs, None
    y = jnp.concatenate(y_parts, axis=1) + jnp.concatenate(y_off, axis=1)
    return xs, y


def _ctx_state_kernel(xm_ref, xp_ref, xn_ref, bm_ref, bp_ref, bn_ref, dt_ref,
                      cwx_ref, cbx_ref, cwb_ref, cbb_ref, dtb_ref, alog_ref, e2_ref,
                      h_ref, s_scr, xbuf, bcbuf, *, nc):
    ph = pl.program_id(1)
    c = pl.program_id(2)
    ci = jnp.where(ph == 0, c, nc - 1 - c)

    @pl.when(c == 0)
    def _():
        s_scr[...] = jnp.zeros_like(s_scr)

    _ssd_chunk(ph, ci == 0, ci == nc - 1, xm_ref, xp_ref, xn_ref, bm_ref, bp_ref, bn_ref, dt_ref,
               cwx_ref, cbx_ref, cwb_ref, cbb_ref, dtb_ref, alog_ref, e2_ref, s_scr, xbuf, bcbuf, False)

    @pl.when(c == nc - 1)
    def _():
        h_ref[0, 0] = s_scr[...]


def _gelu_tanh(x):
    return 0.5 * x * (1.0 + jnp.tanh(0.7978845608028654 * (x + 0.044715 * x * x * x)))


def _ssd_main_kernel(xm_ref, xp_ref, xn_ref, bm_ref, bp_ref, bn_ref, dt_ref,
                     cwx_ref, cbx_ref, cwb_ref, cbb_ref, dtb_ref, alog_ref, e2_ref,
                     h0_ref, z_ref, u_ref, v_ref, dsk_ref, snw_ref, cnw_ref, ws_ref, bs_ref,
                     o_ref, s_scr, xbuf, bcbuf, yf_scr, *, nc):
    ph = pl.program_id(1)
    c = pl.program_id(2)
    ci = jnp.where(ph == 0, c, nc - 1 - c)

    @pl.when(c == 0)
    def _():
        s_scr[...] = h0_ref[0, 0]

    xs, y = _ssd_chunk(ph, ci == 0, ci == nc - 1, xm_ref, xp_ref, xn_ref, bm_ref, bp_ref, bn_ref, dt_ref,
                       cwx_ref, cbx_ref, cwb_ref, cbb_ref, dtb_ref, alog_ref, e2_ref, s_scr, xbuf, bcbuf, True)
    r0 = pl.multiple_of(ci * CHUNK, CHUNK)

    @pl.when(ph == 0)
    def _():
        yf_scr[pl.ds(r0, CHUNK), :] = y

    @pl.when(ph == 1)
    def _():
        yt = (yf_scr[pl.ds(r0, CHUNK), :] + y + dsk_ref[...] * xs) * _silu(z_ref[...])
        yt = yt * lax.rsqrt(jnp.mean(yt * yt, axis=-1, keepdims=True) + EPS) * snw_ref[...]
        o_ref[:, 0:SSD_W] = yt.astype(bf16)
        u = _gelu_tanh(u_ref[...])
        v = _gelu_tanh(v_ref[...])
        vn = (v * lax.rsqrt(jnp.mean(v * v, axis=-1, keepdims=True) + EPS) * cnw_ref[...]).astype(bf16)
        mixed = [_dot(ws_ref[h], vn[:, h * CM_HD:(h + 1) * CM_HD]) for h in range(CM_H)]
        o_ref[:, SSD_W:SSD_W + CM_W] = (u * (jnp.concatenate(mixed, axis=1) + bs_ref[...])).astype(bf16)


def _ssd_common_specs(nc, row_blocks_total):
    cpb = nc
    sub = CHUNK // SUBLANES

    def ci_of(ph, c):
        return jnp.where(ph == 0, c, nc - 1 - c)

    def main_i(b, ph, c):
        return b * cpb + ci_of(ph, c)

    def prev_i(b, ph, c):
        return jnp.maximum(main_i(b, ph, c) * sub - 1, 0)

    def next_i(b, ph, c):
        return jnp.minimum(main_i(b, ph, c) * sub + sub, row_blocks_total * sub - 1)

    bc_col = (4 * 1024) // BC_W
    specs = [
        pl.BlockSpec((CHUNK, SSD_W), lambda b, ph, c: (main_i(b, ph, c), 0)),
        pl.BlockSpec((SUBLANES, SSD_W), lambda b, ph, c: (prev_i(b, ph, c), 0)),
        pl.BlockSpec((SUBLANES, SSD_W), lambda b, ph, c: (next_i(b, ph, c), 0)),
        pl.BlockSpec((CHUNK, BC_W), lambda b, ph, c: (main_i(b, ph, c), bc_col)),
        pl.BlockSpec((SUBLANES, BC_W), lambda b, ph, c: (prev_i(b, ph, c), bc_col)),
        pl.BlockSpec((SUBLANES, BC_W), lambda b, ph, c: (next_i(b, ph, c), bc_col)),
        pl.BlockSpec((CHUNK, DT_W), lambda b, ph, c: (main_i(b, ph, c), 0)),
        pl.BlockSpec((D_CONV, SSD_W), lambda b, ph, c: (0, 0)),
        pl.BlockSpec((1, SSD_W), lambda b, ph, c: (0, 0)),
        pl.BlockSpec((D_CONV, BC_W), lambda b, ph, c: (0, 0)),
        pl.BlockSpec((1, BC_W), lambda b, ph, c: (0, 0)),
        pl.BlockSpec((1, 1, DT_W), lambda b, ph, c: (ph, 0, 0)),
        pl.BlockSpec((1, 1, DT_W), lambda b, ph, c: (ph, 0, 0)),
        pl.BlockSpec((2 * LANES, SSD_W), lambda b, ph, c: (0, 0)),
    ]
    return specs


def _ssd_scratch():
    return [
        pltpu.VMEM((SSD_G, D_STATE, SSD_W // SSD_G), f32),
        pltpu.VMEM((CHUNK + 2 * SUBLANES, SSD_W), f32),
        pltpu.VMEM((CHUNK + 2 * SUBLANES, BC_W), f32),
    ]


def _ctx_state_call(pm, dtm, ssd_consts):
    nc = CTX // CHUNK
    specs = _ssd_common_specs(nc, NB_BATCH * nc)
    return pl.pallas_call(
        functools.partial(_ctx_state_kernel, nc=nc),
        grid=(NB_BATCH, 2, nc),
        in_specs=specs,
        out_specs=pl.BlockSpec((1, 1, SSD_G, D_STATE, SSD_W // SSD_G), lambda b, ph, c: (b, ph, 0, 0, 0)),
        out_shape=jax.ShapeDtypeStruct((NB_BATCH, 2, SSD_G, D_STATE, SSD_W // SSD_G), f32),
        scratch_shapes=_ssd_scratch(),
        compiler_params=_cparams(("arbitrary", "arbitrary", "arbitrary")),
        name="ssd_ctx",
    )(pm, pm, pm, pm, pm, pm, dtm, *ssd_consts)


def _ssd_main_call(pm, dtm, ssd_consts, h0, dsk, snw, cnw, ws, bs_e):
    nc = SEQ // CHUNK
    specs = _ssd_common_specs(nc, NB_BATCH * nc)

    def zrow(b, ph, c):
        return b * nc + jnp.where(ph == 0, nc - 1, nc - 1 - c)

    specs += [
        pl.BlockSpec((1, 1, SSD_G, D_STATE, SSD_W // SSD_G), lambda b, ph, c: (b, ph, 0, 0, 0)),
        pl.BlockSpec((CHUNK, SSD_W), lambda b, ph, c: (zrow(b, ph, c), 1)),
        pl.BlockSpec((CHUNK, CM_W), lambda b, ph, c: (zrow(b, ph, c), 2)),
        pl.BlockSpec((CHUNK, CM_W), lambda b, ph, c: (zrow(b, ph, c), 3)),
        pl.BlockSpec((1, SSD_W), lambda b, ph, c: (0, 0)),
        pl.BlockSpec((1, SSD_W), lambda b, ph, c: (0, 0)),
        pl.BlockSpec((1, CM_W), lambda b, ph, c: (0, 0)),
        pl.BlockSpec((CM_H, CHUNK, CHUNK), lambda b, ph, c: (0, 0, 0)),
        pl.BlockSpec((CHUNK, CM_W), lambda b, ph, c: (0, 0)),
    ]
    return pl.pallas_call(
        functools.partial(_ssd_main_kernel, nc=nc),
        grid=(NB_BATCH, 2, nc),
        in_specs=specs,
        out_specs=pl.BlockSpec((CHUNK, SSD_W + CM_W), lambda b, ph, c: (zrow(b, ph, c), 0)),
        out_shape=jax.ShapeDtypeStruct((N_TOK, SSD_W + CM_W), bf16),
        scratch_shapes=_ssd_scratch() + [pltpu.VMEM((SEQ, SSD_W), f32)],
        compiler_params=_cparams(("arbitrary", "arbitrary", "arbitrary")),
        name="ssd_main",
    )(pm, pm, pm, pm, pm, pm, dtm, *ssd_consts, h0, pm, pm, pm, dsk, snw, cnw, ws, bs_e)


def _outproj_kernel(y_ref, x_ref, g_ref, sh_ref, sc_ref, nw_ref, wo_ref, wr_ref, x1_ref, h_ref, lg_ref):
    x1 = x_ref[...] + g_ref[0, 0] * _dot(y_ref[...], wo_ref[...])
    x1_ref[...] = x1
    y = x1 * lax.rsqrt(jnp.mean(x1 * x1, axis=-1, keepdims=True) + EPS) * nw_ref[...]
    h = y * (1.0 + sc_ref[0, 0]) + sh_ref[0, 0]
    h_ref[...] = h
    lg_ref[...] = lax.dot_general(wr_ref[...], h.astype(bf16), (((1,), (1,)), ((), ())),
                                  preferred_element_type=f32)


def _outproj_call(ycat, x2d, mod6, norm_w, w_out, w_router_t):
    tm = TM_OUT
    rb = SEQ // tm
    return pl.pallas_call(
        _outproj_kernel,
        grid=(N_TOK // tm,),
        in_specs=[
            pl.BlockSpec((tm, D), lambda i: (i, 0)),
            pl.BlockSpec((tm, D), lambda i: (i, 0)),
            pl.BlockSpec((1, 1, 1, D), lambda i: (i // rb, 2, 0, 0)),
            pl.BlockSpec((1, 1, 1, D), lambda i: (i // rb, 3, 0, 0)),
            pl.BlockSpec((1, 1, 1, D), lambda i: (i // rb, 4, 0, 0)),
            pl.BlockSpec((1, D), lambda i: (0, 0)),
            pl.BlockSpec((D, D), lambda i: (0, 0)),
            pl.BlockSpec((N_EXP, D), lambda i: (0, 0)),
        ],
        out_specs=[
            pl.BlockSpec((tm, D), lambda i: (i, 0)),
            pl.BlockSpec((tm, D), lambda i: (i, 0)),
            pl.BlockSpec((N_EXP, tm), lambda i: (0, i)),
        ],
        out_shape=[
            jax.ShapeDtypeStruct((N_TOK, D), f32),
            jax.ShapeDtypeStruct((N_TOK, D), f32),
            jax.ShapeDtypeStruct((N_EXP, N_TOK), f32),
        ],
        compiler_params=_cparams(("arbitrary",)),
        name="outproj",
    )(ycat, x2d, mod6, mod6, mod6, norm_w, w_out, w_router_t)


def _shared_kernel(h_ref, x1_ref, g_ref, w1_ref, w2_ref, o_ref):
    gu = _dot(h_ref[...].astype(bf16), w1_ref[...])
    act = (_silu(gu[:, :D_EXP]) * gu[:, D_EXP:]).astype(bf16)
    o_ref[...] = x1_ref[...] + g_ref[0, 0] * _dot(act, w2_ref[...])


def _shared_call(h2p, x1, mod6, w1, w2):
    tm = TM_SHARED
    rb = SEQ // tm
    return pl.pallas_call(
        _shared_kernel,
        grid=(N_TOK // tm,),
        in_specs=[
            pl.BlockSpec((tm, D), lambda i: (i, 0)),
            pl.BlockSpec((tm, D), lambda i: (i, 0)),
            pl.BlockSpec((1, 1, 1, D), lambda i: (i // rb, 5, 0, 0)),
            pl.BlockSpec((D, 2 * D_EXP), lambda i: (0, 0)),
            pl.BlockSpec((D_EXP, D), lambda i: (0, 0)),
        ],
        out_specs=pl.BlockSpec((tm, D), lambda i: (i, 0)),
        out_shape=jax.ShapeDtypeStruct((N_TOK, D), f32),
        compiler_params=_cparams(("arbitrary",)),
        name="shared",
    )(h2p, x1, mod6, w1, w2)


def _route_kernel(lg_ref, bias_ref, dest_ref, gate_ref, bexp_ref, cnt_ref, pst_ref, bend_ref,
                  idx_scr, pos_scr, cnt_scr, *, n_tiles):
    j = pl.program_id(0)
    tt = TT_ROUTE
    neg = -jnp.inf

    @pl.when(j == 0)
    def _():
        cnt_scr[...] = jnp.zeros_like(cnt_scr)

    s = _sigmoid(lg_ref[...])
    biased = s + bias_ref[...]
    v3 = biased.reshape(N_GRP, GRP_SZ, tt)
    io3 = lax.broadcasted_iota(i32, (N_GRP, GRP_SZ, tt), 1)
    m1 = jnp.max(v3, axis=1, keepdims=True)
    i1 = jnp.min(jnp.where(v3 == m1, io3, GRP_SZ), axis=1, keepdims=True)
    m2 = jnp.max(jnp.where(io3 == i1, neg, v3), axis=1, keepdims=True)
    gs = m1 + m2
    gio = lax.broadcasted_iota(i32, (N_GRP, 1, tt), 0)
    sel = jnp.zeros((N_GRP, 1, tt), dtype=jnp.bool_)
    cur_g = gs
    for _ in range(TOP_GRP):
        m = jnp.max(cur_g, axis=0, keepdims=True)
        gi = jnp.min(jnp.where(cur_g == m, gio, N_GRP), axis=0, keepdims=True)
        hit = gio == gi
        sel = jnp.logical_or(sel, hit)
        cur_g = jnp.where(hit, neg, cur_g)
    cur = jnp.where(sel, v3, neg).reshape(N_EXP, tt)

    eio = lax.broadcasted_iota(i32, (N_EXP, tt), 0)
    onehot = jnp.zeros((N_EXP, tt), f32)
    idxs, gts = [], []
    for _ in range(TOP_K):
        m = jnp.max(cur, axis=0, keepdims=True)
        ei = jnp.min(jnp.where(cur == m, eio, N_EXP), axis=0, keepdims=True)
        hit = eio == ei
        gts.append(jnp.sum(jnp.where(hit, s, 0.0), axis=0, keepdims=True))
        idxs.append(ei)
        onehot = jnp.where(hit, 1.0, onehot)
        cur = jnp.where(hit, neg, cur)
    gsum = gts[0]
    for k in range(1, TOP_K):
        gsum = gsum + gts[k]

    r_io = lax.broadcasted_iota(i32, (tt, tt), 0)
    c_io = lax.broadcasted_iota(i32, (tt, tt), 1)
    upper = (r_io <= c_io).astype(f32).astype(bf16)
    oh = onehot.astype(bf16)
    incl = _dot(oh, upper)
    base = cnt_scr[...]
    pos = incl - 1.0 + jnp.concatenate([base] * (tt // LANES), axis=1)
    cnt_scr[...] = base + _dot(oh, jnp.ones((tt, LANES), bf16))
    c0 = pl.multiple_of(j * tt, tt)
    for k in range(TOP_K):
        hit = eio == idxs[k]
        idx_scr[k:k + 1, pl.ds(c0, tt)] = idxs[k]
        pos_scr[k:k + 1, pl.ds(c0, tt)] = jnp.sum(jnp.where(hit, pos, 0.0), axis=0, keepdims=True)
        gate_ref[k:k + 1, pl.ds(c0, tt)] = gts[k] / gsum * ROUTED_SCALE

    @pl.when(j == n_tiles - 1)
    def _():
        cnt = cnt_scr[...]
        r = float(ROWS_PER_BLOCK)
        nblk = jnp.floor((cnt + (r - 1.0)) * (1.0 / r))
        er = lax.broadcasted_iota(i32, (N_EXP, N_EXP), 0)
        ec = lax.broadcasted_iota(i32, (N_EXP, N_EXP), 1)
        lower = (er >= ec).astype(f32).astype(bf16)
        bend = _dot(lower, nblk.astype(bf16))
        pstart = (bend - nblk) * r
        pstart_t = jnp.concatenate([pstart] * (tt // LANES), axis=1)
        def slots(t, carry):
            c1 = pl.multiple_of(t * tt, tt)
            for k in range(TOP_K):
                hit = eio == idx_scr[k:k + 1, pl.ds(c1, tt)]
                d = jnp.sum(jnp.where(hit, pstart_t, 0.0), axis=0, keepdims=True) + pos_scr[k:k + 1, pl.ds(c1, tt)]
                dest_ref[k:k + 1, pl.ds(c1, tt)] = d.astype(i32)
            return carry

        lax.fori_loop(0, n_tiles, slots, 0)
        bio = lax.broadcasted_iota(i32, (N_EXP, N_BLOCKS), 1).astype(f32)
        bend_t = jnp.concatenate([bend] * (N_BLOCKS // LANES), axis=1)
        be = jnp.sum(jnp.where(bend_t <= bio, 1.0, 0.0), axis=0, keepdims=True)
        bexp_ref[...] = jnp.minimum(be, float(N_EXP - 1)).astype(i32)
        cnt_ref[...] = cnt.astype(i32)
        pst_ref[...] = pstart.astype(i32)
        bend_ref[...] = bend.astype(i32)


def _route_call(logits_t, bias_col):
    tt = TT_ROUTE
    n_tiles = N_TOK // tt
    return pl.pallas_call(
        functools.partial(_route_kernel, n_tiles=n_tiles),
        grid=(n_tiles,),
        in_specs=[
            pl.BlockSpec((N_EXP, tt), lambda j: (0, j)),
            pl.BlockSpec((N_EXP, 1), lambda j: (0, 0)),
        ],
        out_specs=[
            pl.BlockSpec((TOP_K, N_TOK), lambda j: (0, 0)),
            pl.BlockSpec((TOP_K, N_TOK), lambda j: (0, 0)),
            pl.BlockSpec((1, N_BLOCKS), lambda j: (0, 0)),
            pl.BlockSpec((N_EXP, LANES), lambda j: (0, 0)),
            pl.BlockSpec((N_EXP, LANES), lambda j: (0, 0)),
            pl.BlockSpec((N_EXP, LANES), lambda j: (0, 0)),
        ],
        out_shape=[
            jax.ShapeDtypeStruct((TOP_K, N_TOK), i32),
            jax.ShapeDtypeStruct((TOP_K, N_TOK), f32),
            jax.ShapeDtypeStruct((1, N_BLOCKS), i32),
            jax.ShapeDtypeStruct((N_EXP, LANES), i32),
            jax.ShapeDtypeStruct((N_EXP, LANES), i32),
            jax.ShapeDtypeStruct((N_EXP, LANES), i32),
        ],
        scratch_shapes=[
            pltpu.VMEM((TOP_K, N_TOK), i32),
            pltpu.VMEM((TOP_K, N_TOK), f32),
            pltpu.VMEM((N_EXP, LANES), f32),
        ],
        compiler_params=_cparams(("arbitrary",)),
        name="route",
    )(logits_t, bias_col)


def _dispatch_kernel(cnt_ref, pst_ref, nact_ref, dest_ref, h_hbm, zrow_hbm, zblk_hbm, xs_hbm, sem, psem, bsem):
    i = pl.program_id(0)
    n_steps = pl.num_programs(0)
    tt = TT_DISP
    r = ROWS_PER_BLOCK
    base = pl.multiple_of(i * tt, tt)
    slot = i % 2

    def pad_range(e):
        n = cnt_ref[e]
        lo = pst_ref[e] + n
        return lo, pst_ref[e] + (((n + (r - 1)) // r) * r)

    def pad_copy(s):
        return pltpu.make_async_copy(zrow_hbm.at[pl.ds(0, 1), :], xs_hbm.at[pl.ds(s, 1), :], psem)

    def blk_copy(g):
        return pltpu.make_async_copy(zblk_hbm, xs_hbm.at[pl.ds(pl.multiple_of(g * r, r), r), :], bsem)

    def for_pads(fn):
        def per_expert(e, c):
            lo, hi = pad_range(e)
            lax.fori_loop(lo, hi, lambda s, c2: (fn(s), c2)[1], 0)
            return c
        lax.fori_loop(0, N_EXP, per_expert, 0)

    @pl.when(i == 0)
    def _():
        for_pads(lambda s: pad_copy(s).start())
        lax.fori_loop(nact_ref[0], N_BLOCKS, lambda g, c: (blk_copy(g).start(), c)[1], 0)

    for rr in range(tt):
        for k in range(TOP_K):
            d = dest_ref[0, 0, k * tt + rr]
            pltpu.make_async_copy(h_hbm.at[pl.ds(base + rr, 1), :], xs_hbm.at[pl.ds(d, 1), :], sem.at[slot]).start()

    def wait_rows(s):
        def body(q, c):
            pltpu.make_async_copy(h_hbm.at[pl.ds(0, 1), :], xs_hbm.at[pl.ds(0, 1), :], sem.at[s]).wait()
            return c
        lax.fori_loop(0, tt * TOP_K, body, 0, unroll=8)

    @pl.when(i > 0)
    def _():
        wait_rows(1 - slot)

    @pl.when(i == n_steps - 1)
    def _():
        wait_rows(slot)
        for_pads(lambda s: pad_copy(s).wait())
        lax.fori_loop(nact_ref[0], N_BLOCKS, lambda g, c: (blk_copy(g).wait(), c)[1], 0)


def _dispatch_call(cnt, pst, nact, dest3, h2):
    tt = TT_DISP
    grid_spec = pltpu.PrefetchScalarGridSpec(
        num_scalar_prefetch=3,
        grid=(N_TOK // tt,),
        in_specs=[
            pl.BlockSpec((1, 1, TOP_K * tt), lambda i, *_: (i, 0, 0), memory_space=pltpu.SMEM),
            pl.BlockSpec(memory_space=pl.ANY),
            pl.BlockSpec(memory_space=pl.ANY),
            pl.BlockSpec(memory_space=pl.ANY),
        ],
        out_specs=pl.BlockSpec(memory_space=pl.ANY),
        scratch_shapes=[
            pltpu.SemaphoreType.DMA((2,)),
            pltpu.SemaphoreType.DMA(()),
            pltpu.SemaphoreType.DMA(()),
        ],
    )
    return pl.pallas_call(
        _dispatch_kernel,
        grid_spec=grid_spec,
        out_shape=jax.ShapeDtypeStruct((N_BLOCKS * ROWS_PER_BLOCK, D), f32),
        compiler_params=_cparams(("arbitrary",)),
        name="dispatch",
    )(cnt, pst, nact, dest3, h2, jnp.zeros((SUBLANES, D), f32), jnp.zeros((ROWS_PER_BLOCK, D), f32))


def _experts_kernel(bexp_ref, bend_ref, nact_ref, x_ref, wgu_hbm, wd_hbm, o_ref,
                    wgu_buf, wd_buf, sem, wgu_scr, wd_scr, ord_scr):
    g = pl.program_id(0)
    na = nact_ref[0]
    e = bexp_ref[g]

    def fetch(ex, slot):
        return (pltpu.make_async_copy(wgu_hbm.at[ex], wgu_buf.at[slot], sem.at[slot, 0]),
                pltpu.make_async_copy(wd_hbm.at[ex], wd_buf.at[slot], sem.at[slot, 1]))

    @pl.when(g == 0)
    def _():
        ord_scr[0] = 0
        for cp in fetch(e, 0):
            cp.start()

    first = jnp.logical_or(g == 0, e != bexp_ref[jnp.maximum(g - 1, 0)])

    @pl.when(jnp.logical_and(g < na, first))
    def _():
        k = ord_scr[0]
        slot = k % 2
        for cp in fetch(e, slot):
            cp.wait()
        nxt = bend_ref[e]

        @pl.when(nxt < na)
        def _():
            for cp in fetch(bexp_ref[jnp.minimum(nxt, N_BLOCKS - 1)], 1 - slot):
                cp.start()

        wgu_scr[...] = wgu_buf[slot].astype(bf16)
        wd_scr[...] = wd_buf[slot].astype(bf16)
        ord_scr[0] = k + 1

    @pl.when(g < na)
    def _():
        gu = _dot(x_ref[...].astype(bf16), wgu_scr[...])
        act = (_silu(gu[:, :D_EXP]) * gu[:, D_EXP:]).astype(bf16)
        o_ref[...] = _dot(act, wd_scr[...])

    @pl.when(g >= na)
    def _():
        o_ref[...] = jnp.zeros_like(o_ref)


def _experts_call(bexp, bend, nact, xs, w_gate_up, w_down):
    r = ROWS_PER_BLOCK
    grid_spec = pltpu.PrefetchScalarGridSpec(
        num_scalar_prefetch=3,
        grid=(N_BLOCKS,),
        in_specs=[
            pl.BlockSpec((r, D), lambda g, be, bd, na: (jnp.minimum(g, na[0] - 1), 0)),
            pl.BlockSpec(memory_space=pl.ANY),
            pl.BlockSpec(memory_space=pl.ANY),
        ],
        out_specs=pl.BlockSpec((r, D), lambda g, be, bd, na: (g, 0)),
        scratch_shapes=[
            pltpu.VMEM((2, D, 2 * D_EXP), f32),
            pltpu.VMEM((2, D_EXP, D), f32),
            pltpu.SemaphoreType.DMA((2, 2)),
            pltpu.VMEM((D, 2 * D_EXP), bf16),
            pltpu.VMEM((D_EXP, D), bf16),
            pltpu.SMEM((1,), i32),
        ],
    )
    return pl.pallas_call(
        _experts_kernel,
        grid_spec=grid_spec,
        out_shape=jax.ShapeDtypeStruct((N_BLOCKS * r, D), f32),
        compiler_params=_cparams(("arbitrary",)),
        name="experts",
    )(bexp, bend, nact, xs, w_gate_up, w_down)


def _gather_rows(idx_ref, src_hbm, dst, sem, n_rows):
    for r in range(n_rows):
        t = idx_ref[0, 0, r]
        pltpu.make_async_copy(src_hbm.at[pl.ds(t, 1), :], dst.at[pl.ds(r, 1), :], sem).start(priority=r % 2)


def _gather_rows_wait(src_hbm, dst, sem, n_rows):
    def body(r, carry):
        pltpu.make_async_copy(src_hbm.at[pl.ds(0, 1), :], dst.at[pl.ds(0, 1), :], sem).wait()
        return carry
    lax.fori_loop(0, n_rows, body, 0, unroll=8)


def _combine_kernel(d0_ref, dn_ref, ys_hbm, gate_ref, base_ref, g5_ref, nw_ref, o_ref, buf, sem, *, n_tiles):
    i = pl.program_id(0)
    tt = TT_COMB
    n_rows = TOP_K * tt
    slot = i % 2

    @pl.when(i == 0)
    def _():
        _gather_rows(d0_ref, ys_hbm, buf.at[0], sem.at[0], n_rows)

    @pl.when(i + 1 < n_tiles)
    def _():
        _gather_rows(dn_ref, ys_hbm, buf.at[1 - slot], sem.at[1 - slot], n_rows)

    _gather_rows_wait(ys_hbm, buf.at[slot], sem.at[slot], n_rows)
    g = gate_ref[...]
    acc = g[:, 0:1] * buf[slot, 0:tt, :]
    for k in range(1, TOP_K):
        acc = acc + g[:, k:k + 1] * buf[slot, k * tt:(k + 1) * tt, :]
    x2 = base_ref[...] + g5_ref[0, 0] * acc
    o_ref[...] = x2 * lax.rsqrt(jnp.mean(x2 * x2, axis=-1, keepdims=True) + EPS) * nw_ref[...]


def _combine_call(dest3, ys, gates_tk, base, mod6, final_w):
    tt = TT_COMB
    n_tiles = N_TOK // tt
    rb = SEQ // tt
    return pl.pallas_call(
        functools.partial(_combine_kernel, n_tiles=n_tiles),
        grid=(n_tiles,),
        in_specs=[
            pl.BlockSpec((1, 1, TOP_K * tt), lambda i: (0, 0, 0), memory_space=pltpu.SMEM),
            pl.BlockSpec((1, 1, TOP_K * tt), lambda i: (jnp.minimum(i + 1, n_tiles - 1), 0, 0),
                         memory_space=pltpu.SMEM),
            pl.BlockSpec(memory_space=pl.ANY),
            pl.BlockSpec((tt, TOP_K), lambda i: (i, 0)),
            pl.BlockSpec((tt, D), lambda i: (i, 0)),
            pl.BlockSpec((1, 1, 1, D), lambda i: (i // rb, 5, 0, 0)),
            pl.BlockSpec((1, D), lambda i: (0, 0)),
        ],
        out_specs=pl.BlockSpec((tt, D), lambda i: (i, 0)),
        out_shape=jax.ShapeDtypeStruct((N_TOK, D), f32),
        scratch_shapes=[
            pltpu.VMEM((2, TOP_K * tt, D), f32),
            pltpu.SemaphoreType.DMA((2,)),
        ],
        compiler_params=_cparams(("arbitrary",)),
        name="combine",
    )(dest3, dest3, ys, gates_tk, base, mod6, final_w)


def _tile_major(a, tt):
    n = a.shape[1] // tt
    return a.reshape(TOP_K, n, tt).transpose(1, 0, 2).reshape(n, 1, TOP_K * tt)


def kernel(x, c, ctx, c_ctx, w_mod, b_mod, norm1_w, w_in, conv_w, conv_b, dt_bias_f, dt_bias_b, a_log_f, a_log_b, d_skip, ssd_norm_w, cmlp_norm_w, w_spatial, b_spatial, w_out, norm2_w, w_router, router_bias, w_gate_up, w_down, w_shared_gate_up, w_shared_down, final_norm_w):
    assert x.shape == (NB_BATCH, SEQ, D) and ctx.shape == (NB_BATCH, CTX, D) and w_mod.shape[0] == 1

    cs = jnp.concatenate([c, c_ctx[None, :], jnp.zeros((SUBLANES - NB_BATCH - 1, D), f32)], axis=0)
    mod6 = _mod_call(cs, w_mod[0], b_mod).reshape(SUBLANES, N_MOD, 1, D)

    wi = w_in[0]
    w_main = jnp.concatenate([wi[:, OFF_Z:OFF_Z + SSD_W], wi[:, 0:OFF_Z], wi[:, OFF_DTB:OFF_U], wi[:, OFF_U:],
                              wi[:, OFF_Z + SSD_W:OFF_XBC]], axis=1).astype(bf16)
    w_dt = jnp.pad(wi[:, OFF_XBC:OFF_DTB], ((0, 0), (0, DT_W - 2 * SSD_H))).astype(bf16)
    pad_h = DT_W - SSD_H
    dtb = jnp.stack([jnp.pad(dt_bias_f[0], (0, pad_h)), jnp.pad(dt_bias_b[0], (0, pad_h))])[:, None, :]
    alog = jnp.stack([jnp.pad(a_log_f[0], (0, pad_h)), jnp.pad(a_log_b[0], (0, pad_h))])[:, None, :]
    e_rows = lax.broadcasted_iota(i32, (2 * LANES, SSD_W), 0) % LANES
    e_cols = lax.broadcasted_iota(i32, (2 * LANES, SSD_W), 1) // SSD_P
    e2 = (e_rows == e_cols).astype(bf16)
    ssd_consts = (conv_w[0][:, :SSD_W], conv_b[0][None, :SSD_W], conv_w[0][:, SSD_W:], conv_b[0][None, SSD_W:],
                  dtb, alog, e2)

    n1 = norm1_w[0][None, :]
    pm_c, dt_c = _inproj_call(ctx.reshape(NB_BATCH * CTX, D), mod6, n1, w_main, w_dt, lambda i: 2)
    x2d = x.reshape(N_TOK, D)
    pm_l, dt_l = _inproj_call(x2d, mod6, n1, w_main, w_dt, lambda i: i // (SEQ // TM_INPROJ))

    h0 = _ctx_state_call(pm_c, dt_c, ssd_consts)
    dsk = jnp.repeat(d_skip[0], SSD_P)[None, :]
    bs_e = jnp.repeat(jnp.swapaxes(b_spatial[0], 0, 1), CM_HD, axis=1)
    ycat = _ssd_main_call(pm_l, dt_l, ssd_consts, h0, dsk, ssd_norm_w[0][None, :], cmlp_norm_w[0][None, :],
                          w_spatial[0].astype(bf16), bs_e)

    x1, h2, logits_t = _outproj_call(ycat, x2d, mod6, norm2_w[0][None, :], w_out[0].astype(bf16),
                                     jnp.swapaxes(w_router[0], 0, 1).astype(bf16))
    base = _shared_call(h2, x1, mod6, w_shared_gate_up[0].astype(bf16), w_shared_down[0].astype(bf16))

    dest, gates, bexp, cnt, pst, bend = _route_call(logits_t, router_bias[0][:, None])
    cnt, pst, bend = cnt[:, 0], pst[:, 0], bend[:, 0]
    nact = bend[N_EXP - 1:]
    xs = _dispatch_call(cnt, pst, nact, _tile_major(dest, TT_DISP), h2)
    ys = _experts_call(bexp.reshape(N_BLOCKS), bend, nact, xs, w_gate_up[0], w_down[0])
    out = _combine_call(_tile_major(dest, TT_COMB), ys, gates.T, base, mod6, final_norm_w[None, :])
    return out.reshape(NB_BATCH, SEQ, D)
```

```python
import functools

import jax
import jax.numpy as jnp
from jax import lax
from jax.experimental import pallas as pl
from jax.experimental.pallas import tpu as pltpu

f32 = jnp.float32
bf16 = jnp.bfloat16
i32 = jnp.int32

D = 2048
NB_BATCH = 2
SEQ = 4096
CTX = 256
N_MOD = 6
SSD_W = 1024
SSD_H = 16
SSD_P = 64
SSD_G = 2
D_STATE = 128
CHUNK = 128
D_CONV = 4
BC_W = 2 * SSD_G * D_STATE
CM_W = 1024
CM_H = 8
CM_HD = 128
OFF_Z = SSD_W
OFF_XBC = OFF_Z + SSD_W + BC_W
OFF_DTF = OFF_XBC + SSD_H
OFF_DTB = OFF_DTF + SSD_H
OFF_U = OFF_DTB + CM_W
IN_PROJ_DIM = OFF_U + CM_W
N_EXP = 256
TOP_K = 8
N_GRP = 8
TOP_GRP = 4
GRP_SZ = N_EXP // N_GRP
D_EXP = 512
ROUTED_SCALE = 2.5
EPS = 1e-6

P_MAIN = 4 * 1024 + BC_W
DT_W = 128

LANES = 128
SUBLANES = 8
VMEM_LIMIT = 56 * 1024 * 1024

TM_INPROJ = 1024
TN_INPROJ = 512
TM_OUT = 256
TM_SHARED = 512
TT_ROUTE = 512
ROWS_PER_BLOCK = 128
N_TOK = NB_BATCH * SEQ
N_BLOCKS = N_TOK * TOP_K // ROWS_PER_BLOCK + N_EXP
TT_COMB = 64


def _cparams(sem):
    return pltpu.CompilerParams(dimension_semantics=sem, vmem_limit_bytes=VMEM_LIMIT)


def _sigmoid(x):
    return 1.0 / (1.0 + jnp.exp(-x))


def _silu(x):
    return x * _sigmoid(x)


def _dot(a, b):
    return jnp.dot(a, b, preferred_element_type=f32)


def _mod_kernel(c_ref, w_ref, b_ref, o_ref):
    a = _silu(c_ref[...]).astype(bf16)
    o_ref[...] = _dot(a, w_ref[...].astype(bf16)) + b_ref[...]


def _mod_call(cs, w_mod, b_mod):
    tn = 1024
    n = w_mod.shape[1]
    return pl.pallas_call(
        _mod_kernel,
        grid=(n // tn,),
        in_specs=[
            pl.BlockSpec((SUBLANES, D), lambda j: (0, 0)),
            pl.BlockSpec((D, tn), lambda j: (0, j)),
            pl.BlockSpec((1, tn), lambda j: (0, j)),
        ],
        out_specs=pl.BlockSpec((SUBLANES, tn), lambda j: (0, j)),
        out_shape=jax.ShapeDtypeStruct((SUBLANES, n), f32),
        compiler_params=_cparams(("arbitrary",)),
        name="mod",
    )(cs, w_mod, b_mod)


def _inproj_kernel(x_ref, sh_ref, sc_ref, nw_ref, w_ref, wdt_ref, p_ref, dt_ref, h_scr):
    @pl.when(pl.program_id(1) == 0)
    def _():
        x = x_ref[...]
        y = x * lax.rsqrt(jnp.mean(x * x, axis=-1, keepdims=True) + EPS) * nw_ref[...]
        h = (y * (1.0 + sc_ref[0, 0]) + sh_ref[0, 0]).astype(bf16)
        h_scr[...] = h
        dt_ref[...] = _dot(h, wdt_ref[...])

    p_ref[...] = _dot(h_scr[...], w_ref[...])


def _inproj_call(x2d, mod6, norm_w, w_main, w_dt, mod_row_fn):
    m = x2d.shape[0]
    tm, tn = min(TM_INPROJ, m), TN_INPROJ
    return pl.pallas_call(
        _inproj_kernel,
        grid=(m // tm, P_MAIN // tn),
        in_specs=[
            pl.BlockSpec((tm, D), lambda i, j: (i, 0)),
            pl.BlockSpec((1, 1, 1, D), lambda i, j: (mod_row_fn(i), 0, 0, 0)),
            pl.BlockSpec((1, 1, 1, D), lambda i, j: (mod_row_fn(i), 1, 0, 0)),
            pl.BlockSpec((1, D), lambda i, j: (0, 0)),
            pl.BlockSpec((D, tn), lambda i, j: (0, j)),
            pl.BlockSpec((D, DT_W), lambda i, j: (0, 0)),
        ],
        out_specs=[
            pl.BlockSpec((tm, tn), lambda i, j: (i, j)),
            pl.BlockSpec((tm, DT_W), lambda i, j: (i, 0)),
        ],
        out_shape=[
            jax.ShapeDtypeStruct((m, P_MAIN), f32),
            jax.ShapeDtypeStruct((m, DT_W), f32),
        ],
        scratch_shapes=[pltpu.VMEM((tm, D), bf16)],
        compiler_params=_cparams(("arbitrary", "arbitrary")),
        name="inproj",
    )(x2d, mod6, mod6, norm_w, w_main, w_dt)


def _split3(q):
    hi = q.astype(bf16)
    r1 = q - hi.astype(f32)
    mid = r1.astype(bf16)
    lo = (r1 - mid.astype(f32)).astype(bf16)
    return hi, mid, lo


def _split2_cat(q):
    hi = q.astype(bf16)
    lo = (q - hi.astype(f32)).astype(bf16)
    return jnp.concatenate([hi, lo], axis=1)


def _conv_silu(buf, w_ref, b_ref):
    w = w_ref[...]
    y = (w[0:1] * buf[7:135, :] + w[1:2] * buf[8:136, :] + w[2:3] * buf[9:137, :]
         + w[3:4] * buf[10:138, :] + b_ref[...])
    return _silu(y)


def _ssd_chunk(ph, first, last, xm_ref, xp_ref, xn_ref, bm_ref, bp_ref, bn_ref, dt_ref,
               cwx_ref, cbx_ref, cwb_ref, cbb_ref, dtb_ref, alog_ref, e2_ref,
               s_scr, xbuf, bcbuf, with_output):
    zero8x = jnp.zeros((SUBLANES, SSD_W), f32)
    zero8b = jnp.zeros((SUBLANES, BC_W), f32)
    xbuf[0:8, :] = jnp.where(first, zero8x, xp_ref[...])
    xbuf[8:136, :] = xm_ref[...]
    xbuf[136:144, :] = jnp.where(last, zero8x, xn_ref[...])
    bcbuf[0:8, :] = jnp.where(first, zero8b, bp_ref[...])
    bcbuf[8:136, :] = bm_ref[...]
    bcbuf[136:144, :] = jnp.where(last, zero8b, bn_ref[...])
    xs = _conv_silu(xbuf, cwx_ref, cbx_ref)
    bcv = _conv_silu(bcbuf, cwb_ref, cbb_ref)

    dtr = dt_ref[...]
    dtr = jnp.where(ph == 0, dtr, pltpu.roll(dtr, LANES - SSD_H, axis=1))
    pre = dtr + dtb_ref[0]
    dt = jnp.maximum(pre, 0.0) + jnp.log(1.0 + jnp.exp(-jnp.abs(pre)))
    d_a = dt * (-jnp.exp(alog_ref[0]))

    row = lax.broadcasted_iota(i32, (CHUNK, CHUNK), 0)
    col = lax.broadcasted_iota(i32, (CHUNK, CHUNK), 1)
    mask = jnp.where(ph == 0, row - col, col - row) >= 0
    tri = mask.astype(f32).astype(bf16)
    hi, mid, lo = _split3(d_a)
    acs = _dot(tri, hi) + _dot(tri, mid) + _dot(tri, lo)
    tot = jnp.sum(d_a, axis=0, keepdims=True)
    e_a = jnp.exp(acs)
    dte = jnp.exp(tot - acs)
    cd = jnp.broadcast_to(jnp.exp(tot), (SUBLANES, LANES))
    ex = _dot(jnp.concatenate([_split2_cat(dt), _split2_cat(e_a), _split2_cat(dte), _split2_cat(cd)], axis=0),
              e2_ref[...])
    dt_e = ex[0:128]
    ea_e = ex[128:256]
    dte_e = ex[256:384]
    cd_e = ex[384:385]
    xdt = xs * dt_e

    y_parts = []
    if with_output:
        acs_t = acs.T
        lane_lo = col < SSD_P
        g_mats = []
        for g in range(SSD_G):
            b_g = bcv[:, g * D_STATE:(g + 1) * D_STATE].astype(bf16)
            c_g = bcv[:, (SSD_G + g) * D_STATE:(SSD_G + g + 1) * D_STATE].astype(bf16)
            g_mats.append(lax.dot_general(c_g, b_g, (((1,), (1,)), ((), ())), preferred_element_type=f32))
        for p in range(SSD_H // 2):
            g = (2 * p) // (SSD_H // SSD_G)
            xp = xdt[:, p * LANES:(p + 1) * LANES]
            acc = None
            for k in range(2):
                h = 2 * p + k
                seg = jnp.broadcast_to(acs[:, h:h + 1], (CHUNK, CHUNK)) - jnp.broadcast_to(acs_t[h:h + 1, :], (CHUNK, CHUNK))
                dec = jnp.where(mask, jnp.exp(jnp.minimum(seg, 0.0)), 0.0)
                m_h = (g_mats[g] * dec).astype(bf16)
                x_h = jnp.where(lane_lo if k == 0 else jnp.logical_not(lane_lo), xp, 0.0).astype(bf16)
                t = _dot(m_h, x_h)
                acc = t if acc is None else acc + t
            y_parts.append(acc)

    gw = SSD_W // SSD_G
    y_off = []
    for g in range(SSD_G):
        sl = slice(g * gw, (g + 1) * gw)
        b_g = bcv[:, g * D_STATE:(g + 1) * D_STATE].astype(bf16)
        s_g = s_scr[g]
        if with_output:
            c_g = bcv[:, (SSD_G + g) * D_STATE:(SSD_G + g + 1) * D_STATE].astype(bf16)
            y_off.append(_dot(c_g, s_g.astype(bf16)) * ea_e[:, sl])
        upd = lax.dot_general(b_g, (xdt[:, sl] * dte_e[:, sl]).astype(bf16), (((0,), (0,)), ((), ())),
                              preferred_element_type=f32)
        s_scr[g] = s_g * cd_e[:, sl] + upd
    if not with_output:
        return xs, None
    y = jnp.concatenate(y_parts, axis=1) + jnp.concatenate(y_off, axis=1)
    return xs, y


def _ctx_state_kernel(xm_ref, xp_ref, xn_ref, bm_ref, bp_ref, bn_ref, dt_ref,
                      cwx_ref, cbx_ref, cwb_ref, cbb_ref, dtb_ref, alog_ref, e2_ref,
                      h_ref, s_scr, xbuf, bcbuf, *, nc):
    ph = pl.program_id(1)
    c = pl.program_id(2)
    ci = jnp.where(ph == 0, c, nc - 1 - c)

    @pl.when(c == 0)
    def _():
        s_scr[...] = jnp.zeros_like(s_scr)

    _ssd_chunk(ph, ci == 0, ci == nc - 1, xm_ref, xp_ref, xn_ref, bm_ref, bp_ref, bn_ref, dt_ref,
               cwx_ref, cbx_ref, cwb_ref, cbb_ref, dtb_ref, alog_ref, e2_ref, s_scr, xbuf, bcbuf, False)

    @pl.when(c == nc - 1)
    def _():
        h_ref[0, 0] = s_scr[...]


def _gelu_tanh(x):
    return 0.5 * x * (1.0 + jnp.tanh(0.7978845608028654 * (x + 0.044715 * x * x * x)))


def _ssd_main_kernel(xm_ref, xp_ref, xn_ref, bm_ref, bp_ref, bn_ref, dt_ref,
                     cwx_ref, cbx_ref, cwb_ref, cbb_ref, dtb_ref, alog_ref, e2_ref,
                     h0_ref, z_ref, u_ref, v_ref, dsk_ref, snw_ref, cnw_ref, ws_ref, bs_ref,
                     o_ref, s_scr, xbuf, bcbuf, yf_scr, *, nc):
    ph = pl.program_id(1)
    c = pl.program_id(2)
    ci = jnp.where(ph == 0, c, nc - 1 - c)

    @pl.when(c == 0)
    def _():
        s_scr[...] = h0_ref[0, 0]

    xs, y = _ssd_chunk(ph, ci == 0, ci == nc - 1, xm_ref, xp_ref, xn_ref, bm_ref, bp_ref, bn_ref, dt_ref,
                       cwx_ref, cbx_ref, cwb_ref, cbb_ref, dtb_ref, alog_ref, e2_ref, s_scr, xbuf, bcbuf, True)
    r0 = pl.multiple_of(ci * CHUNK, CHUNK)

    @pl.when(ph == 0)
    def _():
        yf_scr[pl.ds(r0, CHUNK), :] = y

    @pl.when(ph == 1)
    def _():
        yt = (yf_scr[pl.ds(r0, CHUNK), :] + y + dsk_ref[...] * xs) * _silu(z_ref[...])
        yt = yt * lax.rsqrt(jnp.mean(yt * yt, axis=-1, keepdims=True) + EPS) * snw_ref[...]
        o_ref[:, 0:SSD_W] = yt.astype(bf16)
        u = _gelu_tanh(u_ref[...])
        v = _gelu_tanh(v_ref[...])
        vn = (v * lax.rsqrt(jnp.mean(v * v, axis=-1, keepdims=True) + EPS) * cnw_ref[...]).astype(bf16)
        mixed = [_dot(ws_ref[h], vn[:, h * CM_HD:(h + 1) * CM_HD]) for h in range(CM_H)]
        o_ref[:, SSD_W:SSD_W + CM_W] = (u * (jnp.concatenate(mixed, axis=1) + bs_ref[...])).astype(bf16)


def _ssd_common_specs(nc, row_blocks_total):
    cpb = nc
    sub = CHUNK // SUBLANES

    def ci_of(ph, c):
        return jnp.where(ph == 0, c, nc - 1 - c)

    def main_i(b, ph, c):
        return b * cpb + ci_of(ph, c)

    def prev_i(b, ph, c):
        return jnp.maximum(main_i(b, ph, c) * sub - 1, 0)

    def next_i(b, ph, c):
        return jnp.minimum(main_i(b, ph, c) * sub + sub, row_blocks_total * sub - 1)

    bc_col = (4 * 1024) // BC_W
    specs = [
        pl.BlockSpec((CHUNK, SSD_W), lambda b, ph, c: (main_i(b, ph, c), 0)),
        pl.BlockSpec((SUBLANES, SSD_W), lambda b, ph, c: (prev_i(b, ph, c), 0)),
        pl.BlockSpec((SUBLANES, SSD_W), lambda b, ph, c: (next_i(b, ph, c), 0)),
        pl.BlockSpec((CHUNK, BC_W), lambda b, ph, c: (main_i(b, ph, c), bc_col)),
        pl.BlockSpec((SUBLANES, BC_W), lambda b, ph, c: (prev_i(b, ph, c), bc_col)),
        pl.BlockSpec((SUBLANES, BC_W), lambda b, ph, c: (next_i(b, ph, c), bc_col)),
        pl.BlockSpec((CHUNK, DT_W), lambda b, ph, c: (main_i(b, ph, c), 0)),
        pl.BlockSpec((D_CONV, SSD_W), lambda b, ph, c: (0, 0)),
        pl.BlockSpec((1, SSD_W), lambda b, ph, c: (0, 0)),
        pl.BlockSpec((D_CONV, BC_W), lambda b, ph, c: (0, 0)),
        pl.BlockSpec((1, BC_W), lambda b, ph, c: (0, 0)),
        pl.BlockSpec((1, 1, DT_W), lambda b, ph, c: (ph, 0, 0)),
        pl.BlockSpec((1, 1, DT_W), lambda b, ph, c: (ph, 0, 0)),
        pl.BlockSpec((2 * LANES, SSD_W), lambda b, ph, c: (0, 0)),
    ]
    return specs


def _ssd_scratch():
    return [
        pltpu.VMEM((SSD_G, D_STATE, SSD_W // SSD_G), f32),
        pltpu.VMEM((CHUNK + 2 * SUBLANES, SSD_W), f32),
        pltpu.VMEM((CHUNK + 2 * SUBLANES, BC_W), f32),
    ]


def _ctx_state_call(pm, dtm, ssd_consts):
    nc = CTX // CHUNK
    specs = _ssd_common_specs(nc, NB_BATCH * nc)
    return pl.pallas_call(
        functools.partial(_ctx_state_kernel, nc=nc),
        grid=(NB_BATCH, 2, nc),
        in_specs=specs,
        out_specs=pl.BlockSpec((1, 1, SSD_G, D_STATE, SSD_W // SSD_G), lambda b, ph, c: (b, ph, 0, 0, 0)),
        out_shape=jax.ShapeDtypeStruct((NB_BATCH, 2, SSD_G, D_STATE, SSD_W // SSD_G), f32),
        scratch_shapes=_ssd_scratch(),
        compiler_params=_cparams(("arbitrary", "arbitrary", "arbitrary")),
        name="ssd_ctx",
    )(pm, pm, pm, pm, pm, pm, dtm, *ssd_consts)


def _ssd_main_call(pm, dtm, ssd_consts, h0, dsk, snw, cnw, ws, bs_e):
    nc = SEQ // CHUNK
    specs = _ssd_common_specs(nc, NB_BATCH * nc)

    def zrow(b, ph, c):
        return b * nc + jnp.where(ph == 0, nc - 1, nc - 1 - c)

    specs += [
        pl.BlockSpec((1, 1, SSD_G, D_STATE, SSD_W // SSD_G), lambda b, ph, c: (b, ph, 0, 0, 0)),
        pl.BlockSpec((CHUNK, SSD_W), lambda b, ph, c: (zrow(b, ph, c), 1)),
        pl.BlockSpec((CHUNK, CM_W), lambda b, ph, c: (zrow(b, ph, c), 2)),
        pl.BlockSpec((CHUNK, CM_W), lambda b, ph, c: (zrow(b, ph, c), 3)),
        pl.BlockSpec((1, SSD_W), lambda b, ph, c: (0, 0)),
        pl.BlockSpec((1, SSD_W), lambda b, ph, c: (0, 0)),
        pl.BlockSpec((1, CM_W), lambda b, ph, c: (0, 0)),
        pl.BlockSpec((CM_H, CHUNK, CHUNK), lambda b, ph, c: (0, 0, 0)),
        pl.BlockSpec((CHUNK, CM_W), lambda b, ph, c: (0, 0)),
    ]
    return pl.pallas_call(
        functools.partial(_ssd_main_kernel, nc=nc),
        grid=(NB_BATCH, 2, nc),
        in_specs=specs,
        out_specs=pl.BlockSpec((CHUNK, SSD_W + CM_W), lambda b, ph, c: (zrow(b, ph, c), 0)),
        out_shape=jax.ShapeDtypeStruct((N_TOK, SSD_W + CM_W), bf16),
        scratch_shapes=_ssd_scratch() + [pltpu.VMEM((SEQ, SSD_W), f32)],
        compiler_params=_cparams(("arbitrary", "arbitrary", "arbitrary")),
        name="ssd_main",
    )(pm, pm, pm, pm, pm, pm, dtm, *ssd_consts, h0, pm, pm, pm, dsk, snw, cnw, ws, bs_e)


def _outproj_kernel(y_ref, x_ref, g_ref, sh_ref, sc_ref, nw_ref, wo_ref, wr_ref, x1_ref, h_ref, lg_ref):
    x1 = x_ref[...] + g_ref[0, 0] * _dot(y_ref[...], wo_ref[...])
    x1_ref[...] = x1
    y = x1 * lax.rsqrt(jnp.mean(x1 * x1, axis=-1, keepdims=True) + EPS) * nw_ref[...]
    h = y * (1.0 + sc_ref[0, 0]) + sh_ref[0, 0]
    h_ref[...] = h
    lg_ref[...] = lax.dot_general(wr_ref[...], h.astype(bf16), (((1,), (1,)), ((), ())),
                                  preferred_element_type=f32)


def _outproj_call(ycat, x2d, mod6, norm_w, w_out, w_router_t):
    tm = TM_OUT
    rb = SEQ // tm
    return pl.pallas_call(
        _outproj_kernel,
        grid=(N_TOK // tm,),
        in_specs=[
            pl.BlockSpec((tm, D), lambda i: (i, 0)),
            pl.BlockSpec((tm, D), lambda i: (i, 0)),
            pl.BlockSpec((1, 1, 1, D), lambda i: (i // rb, 2, 0, 0)),
            pl.BlockSpec((1, 1, 1, D), lambda i: (i // rb, 3, 0, 0)),
            pl.BlockSpec((1, 1, 1, D), lambda i: (i // rb, 4, 0, 0)),
            pl.BlockSpec((1, D), lambda i: (0, 0)),
            pl.BlockSpec((D, D), lambda i: (0, 0)),
            pl.BlockSpec((N_EXP, D), lambda i: (0, 0)),
        ],
        out_specs=[
            pl.BlockSpec((tm, D), lambda i: (i, 0)),
            pl.BlockSpec((tm, D), lambda i: (i, 0)),
            pl.BlockSpec((N_EXP, tm), lambda i: (0, i)),
        ],
        out_shape=[
            jax.ShapeDtypeStruct((N_TOK, D), f32),
            jax.ShapeDtypeStruct((N_TOK, D), f32),
            jax.ShapeDtypeStruct((N_EXP, N_TOK), f32),
        ],
        compiler_params=_cparams(("arbitrary",)),
        name="outproj",
    )(ycat, x2d, mod6, mod6, mod6, norm_w, w_out, w_router_t)


def _shared_kernel(h_ref, x1_ref, g_ref, w1_ref, w2_ref, o_ref):
    gu = _dot(h_ref[...].astype(bf16), w1_ref[...])
    act = (_silu(gu[:, :D_EXP]) * gu[:, D_EXP:]).astype(bf16)
    o_ref[...] = x1_ref[...] + g_ref[0, 0] * _dot(act, w2_ref[...])


def _shared_call(h2p, x1, mod6, w1, w2):
    tm = TM_SHARED
    rb = SEQ // tm
    return pl.pallas_call(
        _shared_kernel,
        grid=(N_TOK // tm,),
        in_specs=[
            pl.BlockSpec((tm, D), lambda i: (i, 0)),
            pl.BlockSpec((tm, D), lambda i: (i, 0)),
            pl.BlockSpec((1, 1, 1, D), lambda i: (i // rb, 5, 0, 0)),
            pl.BlockSpec((D, 2 * D_EXP), lambda i: (0, 0)),
            pl.BlockSpec((D_EXP, D), lambda i: (0, 0)),
        ],
        out_specs=pl.BlockSpec((tm, D), lambda i: (i, 0)),
        out_shape=jax.ShapeDtypeStruct((N_TOK, D), f32),
        compiler_params=_cparams(("arbitrary",)),
        name="shared",
    )(h2p, x1, mod6, w1, w2)


def _route_kernel(lg_ref, bias_ref, dest_ref, gate_ref, bexp_ref, bend_ref,
                  idx_scr, pos_scr, cnt_scr, *, n_tiles):
    j = pl.program_id(0)
    tt = TT_ROUTE
    neg = -jnp.inf

    @pl.when(j == 0)
    def _():
        cnt_scr[...] = jnp.zeros_like(cnt_scr)

    s = _sigmoid(lg_ref[...])
    biased = s + bias_ref[...]
    v3 = biased.reshape(N_GRP, GRP_SZ, tt)
    io3 = lax.broadcasted_iota(i32, (N_GRP, GRP_SZ, tt), 1)
    m1 = jnp.max(v3, axis=1, keepdims=True)
    i1 = jnp.min(jnp.where(v3 == m1, io3, GRP_SZ), axis=1, keepdims=True)
    m2 = jnp.max(jnp.where(io3 == i1, neg, v3), axis=1, keepdims=True)
    gs = m1 + m2
    gio = lax.broadcasted_iota(i32, (N_GRP, 1, tt), 0)
    sel = jnp.zeros((N_GRP, 1, tt), dtype=jnp.bool_)
    cur_g = gs
    for _ in range(TOP_GRP):
        m = jnp.max(cur_g, axis=0, keepdims=True)
        gi = jnp.min(jnp.where(cur_g == m, gio, N_GRP), axis=0, keepdims=True)
        hit = gio == gi
        sel = jnp.logical_or(sel, hit)
        cur_g = jnp.where(hit, neg, cur_g)
    cur = jnp.where(sel, v3, neg).reshape(N_EXP, tt)

    eio = lax.broadcasted_iota(i32, (N_EXP, tt), 0)
    onehot = jnp.zeros((N_EXP, tt), f32)
    idxs, gts = [], []
    for _ in range(TOP_K):
        m = jnp.max(cur, axis=0, keepdims=True)
        ei = jnp.min(jnp.where(cur == m, eio, N_EXP), axis=0, keepdims=True)
        hit = eio == ei
        gts.append(jnp.sum(jnp.where(hit, s, 0.0), axis=0, keepdims=True))
        idxs.append(ei)
        onehot = jnp.where(hit, 1.0, onehot)
        cur = jnp.where(hit, neg, cur)
    gsum = gts[0]
    for k in range(1, TOP_K):
        gsum = gsum + gts[k]

    r_io = lax.broadcasted_iota(i32, (tt, tt), 0)
    c_io = lax.broadcasted_iota(i32, (tt, tt), 1)
    upper = (r_io <= c_io).astype(f32).astype(bf16)
    oh = onehot.astype(bf16)
    incl = _dot(oh, upper)
    base = cnt_scr[...]
    pos = incl - 1.0 + jnp.concatenate([base] * (tt // LANES), axis=1)
    cnt_scr[...] = base + _dot(oh, jnp.ones((tt, LANES), bf16))
    c0 = pl.multiple_of(j * tt, tt)
    for k in range(TOP_K):
        hit = eio == idxs[k]
        idx_scr[k:k + 1, pl.ds(c0, tt)] = idxs[k]
        pos_scr[k:k + 1, pl.ds(c0, tt)] = jnp.sum(jnp.where(hit, pos, 0.0), axis=0, keepdims=True)
        gate_ref[k:k + 1, pl.ds(c0, tt)] = gts[k] / gsum * ROUTED_SCALE

    @pl.when(j == n_tiles - 1)
    def _():
        cnt = cnt_scr[...]
        r = float(ROWS_PER_BLOCK)
        nblk = jnp.floor((cnt + (r - 1.0)) * (1.0 / r))
        er = lax.broadcasted_iota(i32, (N_EXP, N_EXP), 0)
        ec = lax.broadcasted_iota(i32, (N_EXP, N_EXP), 1)
        lower = (er >= ec).astype(f32).astype(bf16)
        bend = _dot(lower, nblk.astype(bf16))
        pstart = (bend - nblk) * r
        pstart_t = jnp.concatenate([pstart] * (tt // LANES), axis=1)
        def slots(t, carry):
            c1 = pl.multiple_of(t * tt, tt)
            for k in range(TOP_K):
                hit = eio == idx_scr[k:k + 1, pl.ds(c1, tt)]
                d = jnp.sum(jnp.where(hit, pstart_t, 0.0), axis=0, keepdims=True) + pos_scr[k:k + 1, pl.ds(c1, tt)]
                dest_ref[k:k + 1, pl.ds(c1, tt)] = d.astype(i32)
            return carry

        lax.fori_loop(0, n_tiles, slots, 0)
        bio = lax.broadcasted_iota(i32, (N_EXP, N_BLOCKS), 1).astype(f32)
        bend_t = jnp.concatenate([bend] * (N_BLOCKS // LANES), axis=1)
        be = jnp.sum(jnp.where(bend_t <= bio, 1.0, 0.0), axis=0, keepdims=True)
        bexp_ref[...] = jnp.minimum(be, float(N_EXP - 1)).astype(i32)
        bend_ref[...] = bend.astype(i32)


def _route_call(logits_t, bias_col):
    tt = TT_ROUTE
    n_tiles = N_TOK // tt
    return pl.pallas_call(
        functools.partial(_route_kernel, n_tiles=n_tiles),
        grid=(n_tiles,),
        in_specs=[
            pl.BlockSpec((N_EXP, tt), lambda j: (0, j)),
            pl.BlockSpec((N_EXP, 1), lambda j: (0, 0)),
        ],
        out_specs=[
            pl.BlockSpec((TOP_K, N_TOK), lambda j: (0, 0)),
            pl.BlockSpec((TOP_K, N_TOK), lambda j: (0, 0)),
            pl.BlockSpec((1, N_BLOCKS), lambda j: (0, 0)),
            pl.BlockSpec((N_EXP, LANES), lambda j: (0, 0)),
        ],
        out_shape=[
            jax.ShapeDtypeStruct((TOP_K, N_TOK), i32),
            jax.ShapeDtypeStruct((TOP_K, N_TOK), f32),
            jax.ShapeDtypeStruct((1, N_BLOCKS), i32),
            jax.ShapeDtypeStruct((N_EXP, LANES), i32),
        ],
        scratch_shapes=[
            pltpu.VMEM((TOP_K, N_TOK), i32),
            pltpu.VMEM((TOP_K, N_TOK), f32),
            pltpu.VMEM((N_EXP, LANES), f32),
        ],
        compiler_params=_cparams(("arbitrary",)),
        name="route",
    )(logits_t, bias_col)


def _gather_rows(idx_ref, src_hbm, dst, sem, n_rows):
    for r in range(n_rows):
        t = idx_ref[0, 0, r]
        pltpu.make_async_copy(src_hbm.at[pl.ds(t, 1), :], dst.at[pl.ds(r, 1), :], sem).start(priority=r % 2)


def _gather_rows_wait(src_hbm, dst, sem, n_rows):
    def body(r, carry):
        pltpu.make_async_copy(src_hbm.at[pl.ds(0, 1), :], dst.at[pl.ds(0, 1), :], sem).wait()
        return carry
    lax.fori_loop(0, n_rows, body, 0, unroll=8)


W_CHUNKS = 4


def _experts_kernel(bexp_ref, bend_ref, nact_ref, tok0_ref, tokn_ref, h_hbm, wgu_hbm, wd_hbm, o_ref,
                    xbuf, xsem, wgu_buf, wd_buf, wsem, ord_scr):
    g = pl.program_id(0)
    na = nact_ref[0]
    e = bexp_ref[g]
    r = ROWS_PER_BLOCK
    xslot = g % 2

    def fetch(ex, slot):
        cps = []
        for q in range(W_CHUNKS):
            a = D // W_CHUNKS
            cps.append(pltpu.make_async_copy(wgu_hbm.at[ex, pl.ds(q * a, a), :],
                                             wgu_buf.at[slot, pl.ds(q * a, a), :], wsem.at[slot]))
            a = D_EXP // W_CHUNKS
            cps.append(pltpu.make_async_copy(wd_hbm.at[ex, pl.ds(q * a, a), :],
                                             wd_buf.at[slot, pl.ds(q * a, a), :], wsem.at[slot]))
        return cps

    def start_fetch(ex, slot):
        for n, cp in enumerate(fetch(ex, slot)):
            cp.start(priority=n % 2)

    @pl.when(g == 0)
    def _():
        ord_scr[0] = 0
        start_fetch(e, 0)
        _gather_rows(tok0_ref, h_hbm, xbuf.at[0], xsem.at[0], r)

    @pl.when(g + 1 < na)
    def _():
        _gather_rows(tokn_ref, h_hbm, xbuf.at[1 - xslot], xsem.at[1 - xslot], r)

    first = jnp.logical_or(g == 0, e != bexp_ref[jnp.maximum(g - 1, 0)])

    @pl.when(jnp.logical_and(g < na, first))
    def _():
        k = ord_scr[0]
        slot = k % 2
        for cp in fetch(e, slot):
            cp.wait()
        nxt = bend_ref[e]

        @pl.when(nxt < na)
        def _():
            start_fetch(bexp_ref[jnp.minimum(nxt, N_BLOCKS - 1)], 1 - slot)

        ord_scr[0] = k + 1

    @pl.when(g < na)
    def _():
        _gather_rows_wait(h_hbm, xbuf.at[xslot], xsem.at[xslot], r)
        ws = (ord_scr[0] + 1) % 2
        dn = (((1,), (0,)), ((), ()))
        gu = lax.dot_general(xbuf[xslot].astype(bf16), wgu_buf[ws], dn, preferred_element_type=f32)
        act = (_silu(gu[:, :D_EXP]) * gu[:, D_EXP:]).astype(bf16)
        o_ref[...] = lax.dot_general(act, wd_buf[ws], dn, preferred_element_type=f32)

    @pl.when(g >= na)
    def _():
        o_ref[...] = jnp.zeros_like(o_ref)


def _experts_call(bexp, bend, nact, slot_tok3, h2, w_gate_up, w_down):
    r = ROWS_PER_BLOCK
    nb = N_BLOCKS
    grid_spec = pltpu.PrefetchScalarGridSpec(
        num_scalar_prefetch=3,
        grid=(nb,),
        in_specs=[
            pl.BlockSpec((1, 1, r), lambda g, *_: (0, 0, 0), memory_space=pltpu.SMEM),
            pl.BlockSpec((1, 1, r), lambda g, *_: (jnp.minimum(g + 1, nb - 1), 0, 0), memory_space=pltpu.SMEM),
            pl.BlockSpec(memory_space=pl.ANY),
            pl.BlockSpec(memory_space=pl.ANY),
            pl.BlockSpec(memory_space=pl.ANY),
        ],
        out_specs=pl.BlockSpec((r, D), lambda g, *_: (g, 0)),
        scratch_shapes=[
            pltpu.VMEM((2, r, D), f32),
            pltpu.SemaphoreType.DMA((2,)),
            pltpu.VMEM((2, D, 2 * D_EXP), f32),
            pltpu.VMEM((2, D_EXP, D), f32),
            pltpu.SemaphoreType.DMA((2,)),
            pltpu.SMEM((1,), i32),
        ],
    )
    return pl.pallas_call(
        _experts_kernel,
        grid_spec=grid_spec,
        out_shape=jax.ShapeDtypeStruct((nb * r, D), f32),
        compiler_params=_cparams(("arbitrary",)),
        name="experts",
    )(bexp, bend, nact, slot_tok3, slot_tok3, h2, w_gate_up, w_down)


def _combine_kernel(d0_ref, dn_ref, ys_hbm, gate_ref, base_ref, g5_ref, nw_ref, o_ref, buf, sem, *, n_tiles):
    i = pl.program_id(0)
    tt = TT_COMB
    n_rows = TOP_K * tt
    slot = i % 2

    @pl.when(i == 0)
    def _():
        _gather_rows(d0_ref, ys_hbm, buf.at[0], sem.at[0], n_rows)

    @pl.when(i + 1 < n_tiles)
    def _():
        _gather_rows(dn_ref, ys_hbm, buf.at[1 - slot], sem.at[1 - slot], n_rows)

    _gather_rows_wait(ys_hbm, buf.at[slot], sem.at[slot], n_rows)
    g = gate_ref[...]
    acc = g[:, 0:1] * buf[slot, 0:tt, :]
    for k in range(1, TOP_K):
        acc = acc + g[:, k:k + 1] * buf[slot, k * tt:(k + 1) * tt, :]
    x2 = base_ref[...] + g5_ref[0, 0] * acc
    o_ref[...] = x2 * lax.rsqrt(jnp.mean(x2 * x2, axis=-1, keepdims=True) + EPS) * nw_ref[...]


def _combine_call(dest3, ys, gates_tk, base, mod6, final_w):
    tt = TT_COMB
    n_tiles = N_TOK // tt
    rb = SEQ // tt
    return pl.pallas_call(
        functools.partial(_combine_kernel, n_tiles=n_tiles),
        grid=(n_tiles,),
        in_specs=[
            pl.BlockSpec((1, 1, TOP_K * tt), lambda i: (0, 0, 0), memory_space=pltpu.SMEM),
            pl.BlockSpec((1, 1, TOP_K * tt), lambda i: (jnp.minimum(i + 1, n_tiles - 1), 0, 0),
                         memory_space=pltpu.SMEM),
            pl.BlockSpec(memory_space=pl.ANY),
            pl.BlockSpec((tt, TOP_K), lambda i: (i, 0)),
            pl.BlockSpec((tt, D), lambda i: (i, 0)),
            pl.BlockSpec((1, 1, 1, D), lambda i: (i // rb, 5, 0, 0)),
            pl.BlockSpec((1, D), lambda i: (0, 0)),
        ],
        out_specs=pl.BlockSpec((tt, D), lambda i: (i, 0)),
        out_shape=jax.ShapeDtypeStruct((N_TOK, D), f32),
        scratch_shapes=[
            pltpu.VMEM((2, TOP_K * tt, D), f32),
            pltpu.SemaphoreType.DMA((2,)),
        ],
        compiler_params=_cparams(("arbitrary",)),
        name="combine",
    )(dest3, dest3, ys, gates_tk, base, mod6, final_w)


def _tile_major(a, tt):
    n = a.shape[1] // tt
    return a.reshape(TOP_K, n, tt).transpose(1, 0, 2).reshape(n, 1, TOP_K * tt)


def kernel(x, c, ctx, c_ctx, w_mod, b_mod, norm1_w, w_in, conv_w, conv_b, dt_bias_f, dt_bias_b, a_log_f, a_log_b, d_skip, ssd_norm_w, cmlp_norm_w, w_spatial, b_spatial, w_out, norm2_w, w_router, router_bias, w_gate_up, w_down, w_shared_gate_up, w_shared_down, final_norm_w):
    assert x.shape == (NB_BATCH, SEQ, D) and ctx.shape == (NB_BATCH, CTX, D) and w_mod.shape[0] == 1

    cs = jnp.concatenate([c, c_ctx[None, :], jnp.zeros((SUBLANES - NB_BATCH - 1, D), f32)], axis=0)
    mod6 = _mod_call(cs, w_mod[0], b_mod).reshape(SUBLANES, N_MOD, 1, D)

    wi = w_in[0]
    w_main = jnp.concatenate([wi[:, OFF_Z:OFF_Z + SSD_W], wi[:, 0:OFF_Z], wi[:, OFF_DTB:OFF_U], wi[:, OFF_U:],
                              wi[:, OFF_Z + SSD_W:OFF_XBC]], axis=1).astype(bf16)
    w_dt = jnp.pad(wi[:, OFF_XBC:OFF_DTB], ((0, 0), (0, DT_W - 2 * SSD_H))).astype(bf16)
    pad_h = DT_W - SSD_H
    dtb = jnp.stack([jnp.pad(dt_bias_f[0], (0, pad_h)), jnp.pad(dt_bias_b[0], (0, pad_h))])[:, None, :]
    alog = jnp.stack([jnp.pad(a_log_f[0], (0, pad_h)), jnp.pad(a_log_b[0], (0, pad_h))])[:, None, :]
    e_rows = lax.broadcasted_iota(i32, (2 * LANES, SSD_W), 0) % LANES
    e_cols = lax.broadcasted_iota(i32, (2 * LANES, SSD_W), 1) // SSD_P
    e2 = (e_rows == e_cols).astype(bf16)
    ssd_consts = (conv_w[0][:, :SSD_W], conv_b[0][None, :SSD_W], conv_w[0][:, SSD_W:], conv_b[0][None, SSD_W:],
                  dtb, alog, e2)

    n1 = norm1_w[0][None, :]
    pm_c, dt_c = _inproj_call(ctx.reshape(NB_BATCH * CTX, D), mod6, n1, w_main, w_dt, lambda i: 2)
    x2d = x.reshape(N_TOK, D)
    pm_l, dt_l = _inproj_call(x2d, mod6, n1, w_main, w_dt, lambda i: i // (SEQ // TM_INPROJ))

    h0 = _ctx_state_call(pm_c, dt_c, ssd_consts)
    dsk = jnp.repeat(d_skip[0], SSD_P)[None, :]
    bs_e = jnp.repeat(jnp.swapaxes(b_spatial[0], 0, 1), CM_HD, axis=1)
    ycat = _ssd_main_call(pm_l, dt_l, ssd_consts, h0, dsk, ssd_norm_w[0][None, :], cmlp_norm_w[0][None, :],
                          w_spatial[0].astype(bf16), bs_e)

    x1, h2, logits_t = _outproj_call(ycat, x2d, mod6, norm2_w[0][None, :], w_out[0].astype(bf16),
                                     jnp.swapaxes(w_router[0], 0, 1).astype(bf16))
    base = _shared_call(h2, x1, mod6, w_shared_gate_up[0].astype(bf16), w_shared_down[0].astype(bf16))

    dest, gates, bexp, bend = _route_call(logits_t, router_bias[0][:, None])
    bend = bend[:, 0]
    nact = bend[N_EXP - 1:]
    tok = jnp.broadcast_to(jnp.arange(N_TOK, dtype=i32)[None, :], (TOP_K, N_TOK))
    slot_tok = jnp.zeros((N_BLOCKS * ROWS_PER_BLOCK,), i32).at[dest.reshape(-1)].set(
        tok.reshape(-1), unique_indices=True, mode="promise_in_bounds")
    ys = _experts_call(bexp.reshape(N_BLOCKS), bend, nact, slot_tok.reshape(N_BLOCKS, 1, ROWS_PER_BLOCK),
                       h2, w_gate_up[0], w_down[0])
    out = _combine_call(_tile_major(dest, TT_COMB), ys, gates.T, base, mod6, final_norm_w[None, :])
    return out.reshape(NB_BATCH, SEQ, D)
```

```python
import functools

import jax
import jax.numpy as jnp
from jax import lax
from jax.experimental import pallas as pl
from jax.experimental.pallas import tpu as pltpu

f32 = jnp.float32
bf16 = jnp.bfloat16
i32 = jnp.int32

D = 2048
NB_BATCH = 2
SEQ = 4096
CTX = 256
N_MOD = 6
SSD_W = 1024
SSD_H = 16
SSD_P = 64
SSD_G = 2
D_STATE = 128
CHUNK = 128
D_CONV = 4
BC_W = 2 * SSD_G * D_STATE
CM_W = 1024
CM_H = 8
CM_HD = 128
OFF_Z = SSD_W
OFF_XBC = OFF_Z + SSD_W + BC_W
OFF_DTF = OFF_XBC + SSD_H
OFF_DTB = OFF_DTF + SSD_H
OFF_U = OFF_DTB + CM_W
IN_PROJ_DIM = OFF_U + CM_W
N_EXP = 256
TOP_K = 8
N_GRP = 8
TOP_GRP = 4
GRP_SZ = N_EXP // N_GRP
D_EXP = 512
ROUTED_SCALE = 2.5
EPS = 1e-6

P_MAIN = 4 * 1024 + BC_W
DT_W = 128

LANES = 128
SUBLANES = 8
VMEM_LIMIT = 56 * 1024 * 1024

TM_INPROJ = 1024
TN_INPROJ = 512
TM_OUT = 256
TM_SHARED = 512
TT_ROUTE = 512
ROWS_PER_BLOCK = 128
N_TOK = NB_BATCH * SEQ
N_BLOCKS = N_TOK * TOP_K // ROWS_PER_BLOCK + N_EXP
TT_COMB = 64


def _cparams(sem):
    return pltpu.CompilerParams(dimension_semantics=sem, vmem_limit_bytes=VMEM_LIMIT)


def _sigmoid(x):
    return 1.0 / (1.0 + jnp.exp(-x))


def _silu(x):
    return x * _sigmoid(x)


def _dot(a, b):
    return jnp.dot(a, b, preferred_element_type=f32)


def _mod_kernel(c_ref, w_ref, b_ref, o_ref):
    a = _silu(c_ref[...]).astype(bf16)
    o_ref[...] = _dot(a, w_ref[...].astype(bf16)) + b_ref[...]


def _mod_call(cs, w_mod, b_mod):
    tn = 1024
    n = w_mod.shape[1]
    return pl.pallas_call(
        _mod_kernel,
        grid=(n // tn,),
        in_specs=[
            pl.BlockSpec((SUBLANES, D), lambda j: (0, 0)),
            pl.BlockSpec((D, tn), lambda j: (0, j)),
            pl.BlockSpec((1, tn), lambda j: (0, j)),
        ],
        out_specs=pl.BlockSpec((SUBLANES, tn), lambda j: (0, j)),
        out_shape=jax.ShapeDtypeStruct((SUBLANES, n), f32),
        compiler_params=_cparams(("arbitrary",)),
        name="mod",
    )(cs, w_mod, b_mod)


def _inproj_kernel(x_ref, sh_ref, sc_ref, nw_ref, w_ref, wdt_ref, p_ref, dt_ref, h_scr):
    @pl.when(pl.program_id(1) == 0)
    def _():
        x = x_ref[...]
        y = x * lax.rsqrt(jnp.mean(x * x, axis=-1, keepdims=True) + EPS) * nw_ref[...]
        h = (y * (1.0 + sc_ref[0, 0]) + sh_ref[0, 0]).astype(bf16)
        h_scr[...] = h
        dt_ref[...] = _dot(h, wdt_ref[...])

    p_ref[...] = _dot(h_scr[...], w_ref[...])


def _inproj_call(x2d, mod6, norm_w, w_main, w_dt, mod_row_fn):
    m = x2d.shape[0]
    tm, tn = min(TM_INPROJ, m), TN_INPROJ
    return pl.pallas_call(
        _inproj_kernel,
        grid=(m // tm, P_MAIN // tn),
        in_specs=[
            pl.BlockSpec((tm, D), lambda i, j: (i, 0)),
            pl.BlockSpec((1, 1, 1, D), lambda i, j: (mod_row_fn(i), 0, 0, 0)),
            pl.BlockSpec((1, 1, 1, D), lambda i, j: (mod_row_fn(i), 1, 0, 0)),
            pl.BlockSpec((1, D), lambda i, j: (0, 0)),
            pl.BlockSpec((D, tn), lambda i, j: (0, j)),
            pl.BlockSpec((D, DT_W), lambda i, j: (0, 0)),
        ],
        out_specs=[
            pl.BlockSpec((tm, tn), lambda i, j: (i, j)),
            pl.BlockSpec((tm, DT_W), lambda i, j: (i, 0)),
        ],
        out_shape=[
            jax.ShapeDtypeStruct((m, P_MAIN), f32),
            jax.ShapeDtypeStruct((m, DT_W), f32),
        ],
        scratch_shapes=[pltpu.VMEM((tm, D), bf16)],
        compiler_params=_cparams(("arbitrary", "arbitrary")),
        name="inproj",
    )(x2d, mod6, mod6, norm_w, w_main, w_dt)


def _split3(q):
    hi = q.astype(bf16)
    r1 = q - hi.astype(f32)
    mid = r1.astype(bf16)
    lo = (r1 - mid.astype(f32)).astype(bf16)
    return hi, mid, lo


def _split2_cat(q):
    hi = q.astype(bf16)
    lo = (q - hi.astype(f32)).astype(bf16)
    return jnp.concatenate([hi, lo], axis=1)


def _conv_silu(buf, w_ref, b_ref):
    w = w_ref[...]
    y = (w[0:1] * buf[7:135, :] + w[1:2] * buf[8:136, :] + w[2:3] * buf[9:137, :]
         + w[3:4] * buf[10:138, :] + b_ref[...])
    return _silu(y)


def _ssd_chunk(ph, first, last, xm_ref, xp_ref, xn_ref, bm_ref, bp_ref, bn_ref, dt_ref,
               cwx_ref, cbx_ref, cwb_ref, cbb_ref, dtb_ref, alog_ref, e2_ref,
               s_scr, xbuf, bcbuf, with_output):
    zero8x = jnp.zeros((SUBLANES, SSD_W), f32)
    zero8b = jnp.zeros((SUBLANES, BC_W), f32)
    xbuf[0:8, :] = jnp.where(first, zero8x, xp_ref[...])
    xbuf[8:136, :] = xm_ref[...]
    xbuf[136:144, :] = jnp.where(last, zero8x, xn_ref[...])
    bcbuf[0:8, :] = jnp.where(first, zero8b, bp_ref[...])
    bcbuf[8:136, :] = bm_ref[...]
    bcbuf[136:144, :] = jnp.where(last, zero8b, bn_ref[...])
    xs = _conv_silu(xbuf, cwx_ref, cbx_ref)
    bcv = _conv_silu(bcbuf, cwb_ref, cbb_ref)

    dtr = dt_ref[...]
    dtr = jnp.where(ph == 0, dtr, pltpu.roll(dtr, LANES - SSD_H, axis=1))
    pre = dtr + dtb_ref[0]
    dt = jnp.maximum(pre, 0.0) + jnp.log(1.0 + jnp.exp(-jnp.abs(pre)))
    d_a = dt * (-jnp.exp(alog_ref[0]))

    row = lax.broadcasted_iota(i32, (CHUNK, CHUNK), 0)
    col = lax.broadcasted_iota(i32, (CHUNK, CHUNK), 1)
    mask = jnp.where(ph == 0, row - col, col - row) >= 0
    tri = mask.astype(f32).astype(bf16)
    hi, mid, lo = _split3(d_a)
    acs = _dot(tri, hi) + _dot(tri, mid) + _dot(tri, lo)
    tot = jnp.sum(d_a, axis=0, keepdims=True)
    e_a = jnp.exp(acs)
    dte = jnp.exp(tot - acs)
    cd = jnp.broadcast_to(jnp.exp(tot), (SUBLANES, LANES))
    ex = _dot(jnp.concatenate([_split2_cat(dt), _split2_cat(e_a), _split2_cat(dte), _split2_cat(cd)], axis=0),
              e2_ref[...])
    dt_e = ex[0:128]
    ea_e = ex[128:256]
    dte_e = ex[256:384]
    cd_e = ex[384:385]
    xdt = xs * dt_e

    y_parts = []
    if with_output:
        acs_t = acs.T
        lane_lo = col < SSD_P
        g_mats = []
        for g in range(SSD_G):
            b_g = bcv[:, g * D_STATE:(g + 1) * D_STATE].astype(bf16)
            c_g = bcv[:, (SSD_G + g) * D_STATE:(SSD_G + g + 1) * D_STATE].astype(bf16)
            g_mats.append(lax.dot_general(c_g, b_g, (((1,), (1,)), ((), ())), preferred_element_type=f32))
        for p in range(SSD_H // 2):
            g = (2 * p) // (SSD_H // SSD_G)
            xp = xdt[:, p * LANES:(p + 1) * LANES]
            acc = None
            for k in range(2):
                h = 2 * p + k
                seg = jnp.broadcast_to(acs[:, h:h + 1], (CHUNK, CHUNK)) - jnp.broadcast_to(acs_t[h:h + 1, :], (CHUNK, CHUNK))
                dec = jnp.where(mask, jnp.exp(jnp.minimum(seg, 0.0)), 0.0)
                m_h = (g_mats[g] * dec).astype(bf16)
                x_h = jnp.where(lane_lo if k == 0 else jnp.logical_not(lane_lo), xp, 0.0).astype(bf16)
                t = _dot(m_h, x_h)
                acc = t if acc is None else acc + t
            y_parts.append(acc)

    gw = SSD_W // SSD_G
    y_off = []
    for g in range(SSD_G):
        sl = slice(g * gw, (g + 1) * gw)
        b_g = bcv[:, g * D_STATE:(g + 1) * D_STATE].astype(bf16)
        s_g = s_scr[g]
        if with_output:
            c_g = bcv[:, (SSD_G + g) * D_STATE:(SSD_G + g + 1) * D_STATE].astype(bf16)
            y_off.append(_dot(c_g, s_g.astype(bf16)) * ea_e[:, sl])
        upd = lax.dot_general(b_g, (xdt[:, sl] * dte_e[:, sl]).astype(bf16), (((0,), (0,)), ((), ())),
                              preferred_element_type=f32)
        s_scr[g] = s_g * cd_e[:, sl] + upd
    if not with_output:
        return xs, None
    y = jnp.concatenate(y_parts, axis=1) + jnp.concatenate(y_off, axis=1)
    return xs, y


def _ctx_state_kernel(xm_ref, xp_ref, xn_ref, bm_ref, bp_ref, bn_ref, dt_ref,
                      cwx_ref, cbx_ref, cwb_ref, cbb_ref, dtb_ref, alog_ref, e2_ref,
                      h_ref, s_scr, xbuf, bcbuf, *, nc):
    ph = pl.program_id(1)
    c = pl.program_id(2)
    ci = jnp.where(ph == 0, c, nc - 1 - c)

    @pl.when(c == 0)
    def _():
        s_scr[...] = jnp.zeros_like(s_scr)

    _ssd_chunk(ph, ci == 0, ci == nc - 1, xm_ref, xp_ref, xn_ref, bm_ref, bp_ref, bn_ref, dt_ref,
               cwx_ref, cbx_ref, cwb_ref, cbb_ref, dtb_ref, alog_ref, e2_ref, s_scr, xbuf, bcbuf, False)

    @pl.when(c == nc - 1)
    def _():
        h_ref[0, 0] = s_scr[...]


def _gelu_tanh(x):
    return 0.5 * x * (1.0 + jnp.tanh(0.7978845608028654 * (x + 0.044715 * x * x * x)))


def _ssd_main_kernel(xm_ref, xp_ref, xn_ref, bm_ref, bp_ref, bn_ref, dt_ref,
                     cwx_ref, cbx_ref, cwb_ref, cbb_ref, dtb_ref, alog_ref, e2_ref,
                     h0_ref, z_ref, u_ref, v_ref, dsk_ref, snw_ref, cnw_ref, ws_ref, bs_ref,
                     o_ref, s_scr, xbuf, bcbuf, yf_scr, *, nc):
    ph = pl.program_id(1)
    c = pl.program_id(2)
    ci = jnp.where(ph == 0, c, nc - 1 - c)

    @pl.when(c == 0)
    def _():
        s_scr[...] = h0_ref[0, 0]

    xs, y = _ssd_chunk(ph, ci == 0, ci == nc - 1, xm_ref, xp_ref, xn_ref, bm_ref, bp_ref, bn_ref, dt_ref,
                       cwx_ref, cbx_ref, cwb_ref, cbb_ref, dtb_ref, alog_ref, e2_ref, s_scr, xbuf, bcbuf, True)
    r0 = pl.multiple_of(ci * CHUNK, CHUNK)

    @pl.when(ph == 0)
    def _():
        yf_scr[pl.ds(r0, CHUNK), :] = y

    @pl.when(ph == 1)
    def _():
        yt = (yf_scr[pl.ds(r0, CHUNK), :] + y + dsk_ref[...] * xs) * _silu(z_ref[...])
        yt = yt * lax.rsqrt(jnp.mean(yt * yt, axis=-1, keepdims=True) + EPS) * snw_ref[...]
        o_ref[:, 0:SSD_W] = yt.astype(bf16)
        u = _gelu_tanh(u_ref[...])
        v = _gelu_tanh(v_ref[...])
        vn = (v * lax.rsqrt(jnp.mean(v * v, axis=-1, keepdims=True) + EPS) * cnw_ref[...]).astype(bf16)
        mixed = [_dot(ws_ref[h], vn[:, h * CM_HD:(h + 1) * CM_HD]) for h in range(CM_H)]
        o_ref[:, SSD_W:SSD_W + CM_W] = (u * (jnp.concatenate(mixed, axis=1) + bs_ref[...])).astype(bf16)


def _ssd_common_specs(nc, row_blocks_total):
    cpb = nc
    sub = CHUNK // SUBLANES

    def ci_of(ph, c):
        return jnp.where(ph == 0, c, nc - 1 - c)

    def main_i(b, ph, c):
        return b * cpb + ci_of(ph, c)

    def prev_i(b, ph, c):
        return jnp.maximum(main_i(b, ph, c) * sub - 1, 0)

    def next_i(b, ph, c):
        return jnp.minimum(main_i(b, ph, c) * sub + sub, row_blocks_total * sub - 1)

    bc_col = (4 * 1024) // BC_W
    specs = [
        pl.BlockSpec((CHUNK, SSD_W), lambda b, ph, c: (main_i(b, ph, c), 0)),
        pl.BlockSpec((SUBLANES, SSD_W), lambda b, ph, c: (prev_i(b, ph, c), 0)),
        pl.BlockSpec((SUBLANES, SSD_W), lambda b, ph, c: (next_i(b, ph, c), 0)),
        pl.BlockSpec((CHUNK, BC_W), lambda b, ph, c: (main_i(b, ph, c), bc_col)),
        pl.BlockSpec((SUBLANES, BC_W), lambda b, ph, c: (prev_i(b, ph, c), bc_col)),
        pl.BlockSpec((SUBLANES, BC_W), lambda b, ph, c: (next_i(b, ph, c), bc_col)),
        pl.BlockSpec((CHUNK, DT_W), lambda b, ph, c: (main_i(b, ph, c), 0)),
        pl.BlockSpec((D_CONV, SSD_W), lambda b, ph, c: (0, 0)),
        pl.BlockSpec((1, SSD_W), lambda b, ph, c: (0, 0)),
        pl.BlockSpec((D_CONV, BC_W), lambda b, ph, c: (0, 0)),
        pl.BlockSpec((1, BC_W), lambda b, ph, c: (0, 0)),
        pl.BlockSpec((1, 1, DT_W), lambda b, ph, c: (ph, 0, 0)),
        pl.BlockSpec((1, 1, DT_W), lambda b, ph, c: (ph, 0, 0)),
        pl.BlockSpec((2 * LANES, SSD_W), lambda b, ph, c: (0, 0)),
    ]
    return specs


def _ssd_scratch():
    return [
        pltpu.VMEM((SSD_G, D_STATE, SSD_W // SSD_G), f32),
        pltpu.VMEM((CHUNK + 2 * SUBLANES, SSD_W), f32),
        pltpu.VMEM((CHUNK + 2 * SUBLANES, BC_W), f32),
    ]


def _ctx_state_call(pm, dtm, ssd_consts):
    nc = CTX // CHUNK
    specs = _ssd_common_specs(nc, NB_BATCH * nc)
    return pl.pallas_call(
        functools.partial(_ctx_state_kernel, nc=nc),
        grid=(NB_BATCH, 2, nc),
        in_specs=specs,
        out_specs=pl.BlockSpec((1, 1, SSD_G, D_STATE, SSD_W // SSD_G), lambda b, ph, c: (b, ph, 0, 0, 0)),
        out_shape=jax.ShapeDtypeStruct((NB_BATCH, 2, SSD_G, D_STATE, SSD_W // SSD_G), f32),
        scratch_shapes=_ssd_scratch(),
        compiler_params=_cparams(("arbitrary", "arbitrary", "arbitrary")),
        name="ssd_ctx",
    )(pm, pm, pm, pm, pm, pm, dtm, *ssd_consts)


def _ssd_main_call(pm, dtm, ssd_consts, h0, dsk, snw, cnw, ws, bs_e):
    nc = SEQ // CHUNK
    specs = _ssd_common_specs(nc, NB_BATCH * nc)

    def zrow(b, ph, c):
        return b * nc + jnp.where(ph == 0, nc - 1, nc - 1 - c)

    specs += [
        pl.BlockSpec((1, 1, SSD_G, D_STATE, SSD_W // SSD_G), lambda b, ph, c: (b, ph, 0, 0, 0)),
        pl.BlockSpec((CHUNK, SSD_W), lambda b, ph, c: (zrow(b, ph, c), 1)),
        pl.BlockSpec((CHUNK, CM_W), lambda b, ph, c: (zrow(b, ph, c), 2)),
        pl.BlockSpec((CHUNK, CM_W), lambda b, ph, c: (zrow(b, ph, c), 3)),
        pl.BlockSpec((1, SSD_W), lambda b, ph, c: (0, 0)),
        pl.BlockSpec((1, SSD_W), lambda b, ph, c: (0, 0)),
        pl.BlockSpec((1, CM_W), lambda b, ph, c: (0, 0)),
        pl.BlockSpec((CM_H, CHUNK, CHUNK), lambda b, ph, c: (0, 0, 0)),
        pl.BlockSpec((CHUNK, CM_W), lambda b, ph, c: (0, 0)),
    ]
    return pl.pallas_call(
        functools.partial(_ssd_main_kernel, nc=nc),
        grid=(NB_BATCH, 2, nc),
        in_specs=specs,
        out_specs=pl.BlockSpec((CHUNK, SSD_W + CM_W), lambda b, ph, c: (zrow(b, ph, c), 0)),
        out_shape=jax.ShapeDtypeStruct((N_TOK, SSD_W + CM_W), bf16),
        scratch_shapes=_ssd_scratch() + [pltpu.VMEM((SEQ, SSD_W), f32)],
        compiler_params=_cparams(("arbitrary", "arbitrary", "arbitrary")),
        name="ssd_main",
    )(pm, pm, pm, pm, pm, pm, dtm, *ssd_consts, h0, pm, pm, pm, dsk, snw, cnw, ws, bs_e)


ROW_CHUNKS = D // LANES


def _store_token_major(ref, val, n_rows, row0=0):
    for c in range(ROW_CHUNKS):
        ref[pl.ds(row0 + c, n_rows, stride=ROW_CHUNKS), :] = val[:, c * LANES:(c + 1) * LANES]


def _load_token_major_chunk(ref, lead, c, n_rows, row0=0):
    idx = pl.ds(row0 + c, n_rows, stride=ROW_CHUNKS)
    return ref[idx, :] if lead is None else ref[lead, idx, :]


def _outproj_kernel(y_ref, x_ref, g_ref, sh_ref, sc_ref, g5_ref, nw_ref, wo_ref, wr_ref, w1_ref, w2_ref,
                    base_ref, h_ref, lg_ref):
    x1 = x_ref[...] + g_ref[0, 0] * _dot(y_ref[...], wo_ref[...])
    y = x1 * lax.rsqrt(jnp.mean(x1 * x1, axis=-1, keepdims=True) + EPS) * nw_ref[...]
    h = y * (1.0 + sc_ref[0, 0]) + sh_ref[0, 0]
    _store_token_major(h_ref, h, h.shape[0])
    hb = h.astype(bf16)
    lg_ref[...] = lax.dot_general(wr_ref[...], hb, (((1,), (1,)), ((), ())), preferred_element_type=f32)
    gu = _dot(hb, w1_ref[...])
    act = (_silu(gu[:, :D_EXP]) * gu[:, D_EXP:]).astype(bf16)
    base_ref[...] = x1 + g5_ref[0, 0] * _dot(act, w2_ref[...])


def _outproj_call(ycat, x2d, mod6, norm_w, w_out, w_router_t, w1, w2):
    tm = TM_OUT
    rb = SEQ // tm
    const = lambda i: (0, 0)
    return pl.pallas_call(
        _outproj_kernel,
        grid=(N_TOK // tm,),
        in_specs=[
            pl.BlockSpec((tm, D), lambda i: (i, 0)),
            pl.BlockSpec((tm, D), lambda i: (i, 0)),
            pl.BlockSpec((1, 1, 1, D), lambda i: (i // rb, 2, 0, 0)),
            pl.BlockSpec((1, 1, 1, D), lambda i: (i // rb, 3, 0, 0)),
            pl.BlockSpec((1, 1, 1, D), lambda i: (i // rb, 4, 0, 0)),
            pl.BlockSpec((1, 1, 1, D), lambda i: (i // rb, 5, 0, 0)),
            pl.BlockSpec((1, D), const),
            pl.BlockSpec((D, D), const),
            pl.BlockSpec((N_EXP, D), const),
            pl.BlockSpec((D, 2 * D_EXP), const),
            pl.BlockSpec((D_EXP, D), const),
        ],
        out_specs=[
            pl.BlockSpec((tm, D), lambda i: (i, 0)),
            pl.BlockSpec((tm * ROW_CHUNKS, LANES), lambda i: (i, 0)),
            pl.BlockSpec((N_EXP, tm), lambda i: (0, i)),
        ],
        out_shape=[
            jax.ShapeDtypeStruct((N_TOK, D), f32),
            jax.ShapeDtypeStruct((N_TOK * ROW_CHUNKS, LANES), f32),
            jax.ShapeDtypeStruct((N_EXP, N_TOK), f32),
        ],
        compiler_params=_cparams(("arbitrary",)),
        name="outproj",
    )(ycat, x2d, mod6, mod6, mod6, mod6, norm_w, w_out, w_router_t, w1, w2)


def _route_kernel(lg_ref, bias_ref, dest_ref, gate_ref, bexp_ref, bend_ref,
                  idx_scr, pos_scr, cnt_scr, *, n_tiles):
    j = pl.program_id(0)
    tt = TT_ROUTE
    neg = -jnp.inf

    @pl.when(j == 0)
    def _():
        cnt_scr[...] = jnp.zeros_like(cnt_scr)

    s = _sigmoid(lg_ref[...])
    biased = s + bias_ref[...]
    v3 = biased.reshape(N_GRP, GRP_SZ, tt)
    io3 = lax.broadcasted_iota(i32, (N_GRP, GRP_SZ, tt), 1)
    m1 = jnp.max(v3, axis=1, keepdims=True)
    i1 = jnp.min(jnp.where(v3 == m1, io3, GRP_SZ), axis=1, keepdims=True)
    m2 = jnp.max(jnp.where(io3 == i1, neg, v3), axis=1, keepdims=True)
    gs = m1 + m2
    gio = lax.broadcasted_iota(i32, (N_GRP, 1, tt), 0)
    sel = jnp.zeros((N_GRP, 1, tt), dtype=jnp.bool_)
    cur_g = gs
    for _ in range(TOP_GRP):
        m = jnp.max(cur_g, axis=0, keepdims=True)
        gi = jnp.min(jnp.where(cur_g == m, gio, N_GRP), axis=0, keepdims=True)
        hit = gio == gi
        sel = jnp.logical_or(sel, hit)
        cur_g = jnp.where(hit, neg, cur_g)
    cur = jnp.where(sel, v3, neg).reshape(N_EXP, tt)

    eio = lax.broadcasted_iota(i32, (N_EXP, tt), 0)
    onehot = jnp.zeros((N_EXP, tt), f32)
    idxs, gts = [], []
    for _ in range(TOP_K):
        m = jnp.max(cur, axis=0, keepdims=True)
        ei = jnp.min(jnp.where(cur == m, eio, N_EXP), axis=0, keepdims=True)
        hit = eio == ei
        gts.append(jnp.sum(jnp.where(hit, s, 0.0), axis=0, keepdims=True))
        idxs.append(ei)
        onehot = jnp.where(hit, 1.0, onehot)
        cur = jnp.where(hit, neg, cur)
    gsum = gts[0]
    for k in range(1, TOP_K):
        gsum = gsum + gts[k]

    r_io = lax.broadcasted_iota(i32, (tt, tt), 0)
    c_io = lax.broadcasted_iota(i32, (tt, tt), 1)
    upper = (r_io <= c_io).astype(f32).astype(bf16)
    oh = onehot.astype(bf16)
    incl = _dot(oh, upper)
    base = cnt_scr[...]
    pos = incl - 1.0 + jnp.concatenate([base] * (tt // LANES), axis=1)
    cnt_scr[...] = base + _dot(oh, jnp.ones((tt, LANES), bf16))
    c0 = pl.multiple_of(j * tt, tt)
    for k in range(TOP_K):
        hit = eio == idxs[k]
        idx_scr[k:k + 1, pl.ds(c0, tt)] = idxs[k]
        pos_scr[k:k + 1, pl.ds(c0, tt)] = jnp.sum(jnp.where(hit, pos, 0.0), axis=0, keepdims=True)
        gate_ref[k:k + 1, pl.ds(c0, tt)] = gts[k] / gsum * ROUTED_SCALE

    @pl.when(j == n_tiles - 1)
    def _():
        cnt = cnt_scr[...]
        r = float(ROWS_PER_BLOCK)
        nblk = jnp.floor((cnt + (r - 1.0)) * (1.0 / r))
        er = lax.broadcasted_iota(i32, (N_EXP, N_EXP), 0)
        ec = lax.broadcasted_iota(i32, (N_EXP, N_EXP), 1)
        lower = (er >= ec).astype(f32).astype(bf16)
        bend = _dot(lower, nblk.astype(bf16))
        pstart = (bend - nblk) * r
        pstart_t = jnp.concatenate([pstart] * (tt // LANES), axis=1)
        def slots(t, carry):
            c1 = pl.multiple_of(t * tt, tt)
            for k in range(TOP_K):
                hit = eio == idx_scr[k:k + 1, pl.ds(c1, tt)]
                d = jnp.sum(jnp.where(hit, pstart_t, 0.0), axis=0, keepdims=True) + pos_scr[k:k + 1, pl.ds(c1, tt)]
                dest_ref[k:k + 1, pl.ds(c1, tt)] = d.astype(i32)
            return carry

        lax.fori_loop(0, n_tiles, slots, 0)
        bio = lax.broadcasted_iota(i32, (N_EXP, N_BLOCKS), 1).astype(f32)
        bend_t = jnp.concatenate([bend] * (N_BLOCKS // LANES), axis=1)
        be = jnp.sum(jnp.where(bend_t <= bio, 1.0, 0.0), axis=0, keepdims=True)
        bexp_ref[...] = jnp.minimum(be, float(N_EXP - 1)).astype(i32)
        bend_ref[...] = bend.astype(i32)


def _route_call(logits_t, bias_col):
    tt = TT_ROUTE
    n_tiles = N_TOK // tt
    return pl.pallas_call(
        functools.partial(_route_kernel, n_tiles=n_tiles),
        grid=(n_tiles,),
        in_specs=[
            pl.BlockSpec((N_EXP, tt), lambda j: (0, j)),
            pl.BlockSpec((N_EXP, 1), lambda j: (0, 0)),
        ],
        out_specs=[
            pl.BlockSpec((TOP_K, N_TOK), lambda j: (0, 0)),
            pl.BlockSpec((TOP_K, N_TOK), lambda j: (0, 0)),
            pl.BlockSpec((1, N_BLOCKS), lambda j: (0, 0)),
            pl.BlockSpec((N_EXP, LANES), lambda j: (0, 0)),
        ],
        out_shape=[
            jax.ShapeDtypeStruct((TOP_K, N_TOK), i32),
            jax.ShapeDtypeStruct((TOP_K, N_TOK), f32),
            jax.ShapeDtypeStruct((1, N_BLOCKS), i32),
            jax.ShapeDtypeStruct((N_EXP, LANES), i32),
        ],
        scratch_shapes=[
            pltpu.VMEM((TOP_K, N_TOK), i32),
            pltpu.VMEM((TOP_K, N_TOK), f32),
            pltpu.VMEM((N_EXP, LANES), f32),
        ],
        compiler_params=_cparams(("arbitrary",)),
        name="route",
    )(logits_t, bias_col)


def _gather_rows(idx_ref, src_hbm, dst, sem, n_rows):
    for r in range(n_rows):
        t = idx_ref[0, 0, r]
        pltpu.make_async_copy(src_hbm.at[t], dst.at[pl.ds(r * ROW_CHUNKS, ROW_CHUNKS), :], sem).start(priority=r % 2)


def _gather_rows_wait(src_hbm, dst, sem, n_rows):
    def body(r, carry):
        pltpu.make_async_copy(src_hbm.at[0], dst.at[pl.ds(0, ROW_CHUNKS), :], sem).wait()
        return carry
    lax.fori_loop(0, n_rows, body, 0, unroll=8)


W_CHUNKS = 4


def _experts_kernel(bexp_ref, bend_ref, nact_ref, tok0_ref, tokn_ref, h_hbm, wgu_hbm, wd_hbm, o_ref,
                    xbuf, xsem, wgu_buf, wd_buf, wsem, ord_scr):
    g = pl.program_id(0)
    na = nact_ref[0]
    e = bexp_ref[g]
    r = ROWS_PER_BLOCK
    xslot = g % 2

    def fetch(ex, slot):
        cps = []
        for q in range(W_CHUNKS):
            a = D // W_CHUNKS
            cps.append(pltpu.make_async_copy(wgu_hbm.at[ex, pl.ds(q * a, a), :],
                                             wgu_buf.at[slot, pl.ds(q * a, a), :], wsem.at[slot]))
            a = D_EXP // W_CHUNKS
            cps.append(pltpu.make_async_copy(wd_hbm.at[ex, pl.ds(q * a, a), :],
                                             wd_buf.at[slot, pl.ds(q * a, a), :], wsem.at[slot]))
        return cps

    def start_fetch(ex, slot):
        for n, cp in enumerate(fetch(ex, slot)):
            cp.start(priority=n % 2)

    @pl.when(g == 0)
    def _():
        ord_scr[0] = 0
        start_fetch(e, 0)
        _gather_rows(tok0_ref, h_hbm, xbuf.at[0], xsem.at[0], r)

    @pl.when(g + 1 < na)
    def _():
        _gather_rows(tokn_ref, h_hbm, xbuf.at[1 - xslot], xsem.at[1 - xslot], r)

    first = jnp.logical_or(g == 0, e != bexp_ref[jnp.maximum(g - 1, 0)])

    @pl.when(jnp.logical_and(g < na, first))
    def _():
        k = ord_scr[0]
        slot = k % 2
        for cp in fetch(e, slot):
            cp.wait()
        nxt = bend_ref[e]

        @pl.when(nxt < na)
        def _():
            start_fetch(bexp_ref[jnp.minimum(nxt, N_BLOCKS - 1)], 1 - slot)

        ord_scr[0] = k + 1

    @pl.when(g < na)
    def _():
        _gather_rows_wait(h_hbm, xbuf.at[xslot], xsem.at[xslot], r)
        ws = (ord_scr[0] + 1) % 2
        dn = (((1,), (0,)), ((), ()))
        gu = None
        for p in range(ROW_CHUNKS // 2):
            xp = jnp.concatenate([_load_token_major_chunk(xbuf, xslot, 2 * p, r),
                                  _load_token_major_chunk(xbuf, xslot, 2 * p + 1, r)], axis=1).astype(bf16)
            t = lax.dot_general(xp, wgu_buf[ws, 2 * p * LANES:(2 * p + 2) * LANES, :], dn, preferred_element_type=f32)
            gu = t if gu is None else gu + t
        act = (_silu(gu[:, :D_EXP]) * gu[:, D_EXP:]).astype(bf16)
        _store_token_major(o_ref, lax.dot_general(act, wd_buf[ws], dn, preferred_element_type=f32), r)

    @pl.when(g >= na)
    def _():
        o_ref[...] = jnp.zeros_like(o_ref)


def _experts_call(bexp, bend, nact, slot_tok3, h2, w_gate_up, w_down):
    r = ROWS_PER_BLOCK
    nb = N_BLOCKS
    grid_spec = pltpu.PrefetchScalarGridSpec(
        num_scalar_prefetch=3,
        grid=(nb,),
        in_specs=[
            pl.BlockSpec((1, 1, r), lambda g, *_: (0, 0, 0), memory_space=pltpu.SMEM),
            pl.BlockSpec((1, 1, r), lambda g, *_: (jnp.minimum(g + 1, nb - 1), 0, 0), memory_space=pltpu.SMEM),
            pl.BlockSpec(memory_space=pl.ANY),
            pl.BlockSpec(memory_space=pl.ANY),
            pl.BlockSpec(memory_space=pl.ANY),
        ],
        out_specs=pl.BlockSpec((r * ROW_CHUNKS, LANES), lambda g, *_: (g, 0)),
        scratch_shapes=[
            pltpu.VMEM((2, r * ROW_CHUNKS, LANES), f32),
            pltpu.SemaphoreType.DMA((2,)),
            pltpu.VMEM((2, D, 2 * D_EXP), f32),
            pltpu.VMEM((2, D_EXP, D), f32),
            pltpu.SemaphoreType.DMA((2,)),
            pltpu.SMEM((1,), i32),
        ],
    )
    return pl.pallas_call(
        _experts_kernel,
        grid_spec=grid_spec,
        out_shape=jax.ShapeDtypeStruct((nb * r * ROW_CHUNKS, LANES), f32),
        compiler_params=_cparams(("arbitrary",)),
        name="experts",
    )(bexp, bend, nact, slot_tok3, slot_tok3, h2, w_gate_up, w_down)


def _combine_kernel(d0_ref, dn_ref, ys_hbm, gate_ref, base_ref, g5_ref, nw_ref, o_ref, buf, sem, *, n_tiles):
    i = pl.program_id(0)
    tt = TT_COMB
    n_rows = TOP_K * tt
    slot = i % 2

    @pl.when(i == 0)
    def _():
        _gather_rows(d0_ref, ys_hbm, buf.at[0], sem.at[0], n_rows)

    @pl.when(i + 1 < n_tiles)
    def _():
        _gather_rows(dn_ref, ys_hbm, buf.at[1 - slot], sem.at[1 - slot], n_rows)

    _gather_rows_wait(ys_hbm, buf.at[slot], sem.at[slot], n_rows)
    g = gate_ref[...]
    gcols = [jnp.broadcast_to(g[:, k:k + 1], (tt, LANES)) for k in range(TOP_K)]
    parts = []
    for c in range(ROW_CHUNKS):
        a = gcols[0] * _load_token_major_chunk(buf, slot, c, tt)
        for k in range(1, TOP_K):
            a = a + gcols[k] * _load_token_major_chunk(buf, slot, c, tt, row0=k * tt * ROW_CHUNKS)
        parts.append(a)
    x2 = base_ref[...] + g5_ref[0, 0] * jnp.concatenate(parts, axis=1)
    o_ref[...] = x2 * lax.rsqrt(jnp.mean(x2 * x2, axis=-1, keepdims=True) + EPS) * nw_ref[...]


def _combine_call(dest3, ys, gates_tk, base, mod6, final_w):
    tt = TT_COMB
    n_tiles = N_TOK // tt
    rb = SEQ // tt
    return pl.pallas_call(
        functools.partial(_combine_kernel, n_tiles=n_tiles),
        grid=(n_tiles,),
        in_specs=[
            pl.BlockSpec((1, 1, TOP_K * tt), lambda i: (0, 0, 0), memory_space=pltpu.SMEM),
            pl.BlockSpec((1, 1, TOP_K * tt), lambda i: (jnp.minimum(i + 1, n_tiles - 1), 0, 0),
                         memory_space=pltpu.SMEM),
            pl.BlockSpec(memory_space=pl.ANY),
            pl.BlockSpec((tt, TOP_K), lambda i: (i, 0)),
            pl.BlockSpec((tt, D), lambda i: (i, 0)),
            pl.BlockSpec((1, 1, 1, D), lambda i: (i // rb, 5, 0, 0)),
            pl.BlockSpec((1, D), lambda i: (0, 0)),
        ],
        out_specs=pl.BlockSpec((tt, D), lambda i: (i, 0)),
        out_shape=jax.ShapeDtypeStruct((N_TOK, D), f32),
        scratch_shapes=[
            pltpu.VMEM((2, TOP_K * tt * ROW_CHUNKS, LANES), f32),
            pltpu.SemaphoreType.DMA((2,)),
        ],
        compiler_params=_cparams(("arbitrary",)),
        name="combine",
    )(dest3, dest3, ys, gates_tk, base, mod6, final_w)


def _tile_major(a, tt):
    n = a.shape[1] // tt
    return a.reshape(TOP_K, n, tt).transpose(1, 0, 2).reshape(n, 1, TOP_K * tt)


def kernel(x, c, ctx, c_ctx, w_mod, b_mod, norm1_w, w_in, conv_w, conv_b, dt_bias_f, dt_bias_b, a_log_f, a_log_b, d_skip, ssd_norm_w, cmlp_norm_w, w_spatial, b_spatial, w_out, norm2_w, w_router, router_bias, w_gate_up, w_down, w_shared_gate_up, w_shared_down, final_norm_w):
    assert x.shape == (NB_BATCH, SEQ, D) and ctx.shape == (NB_BATCH, CTX, D) and w_mod.shape[0] == 1

    cs = jnp.concatenate([c, c_ctx[None, :], jnp.zeros((SUBLANES - NB_BATCH - 1, D), f32)], axis=0)
    mod6 = _mod_call(cs, w_mod[0], b_mod).reshape(SUBLANES, N_MOD, 1, D)

    wi = w_in[0]
    w_main = jnp.concatenate([wi[:, OFF_Z:OFF_Z + SSD_W], wi[:, 0:OFF_Z], wi[:, OFF_DTB:OFF_U], wi[:, OFF_U:],
                              wi[:, OFF_Z + SSD_W:OFF_XBC]], axis=1).astype(bf16)
    w_dt = jnp.pad(wi[:, OFF_XBC:OFF_DTB], ((0, 0), (0, DT_W - 2 * SSD_H))).astype(bf16)
    pad_h = DT_W - SSD_H
    dtb = jnp.stack([jnp.pad(dt_bias_f[0], (0, pad_h)), jnp.pad(dt_bias_b[0], (0, pad_h))])[:, None, :]
    alog = jnp.stack([jnp.pad(a_log_f[0], (0, pad_h)), jnp.pad(a_log_b[0], (0, pad_h))])[:, None, :]
    e_rows = lax.broadcasted_iota(i32, (2 * LANES, SSD_W), 0) % LANES
    e_cols = lax.broadcasted_iota(i32, (2 * LANES, SSD_W), 1) // SSD_P
    e2 = (e_rows == e_cols).astype(bf16)
    ssd_consts = (conv_w[0][:, :SSD_W], conv_b[0][None, :SSD_W], conv_w[0][:, SSD_W:], conv_b[0][None, SSD_W:],
                  dtb, alog, e2)

    n1 = norm1_w[0][None, :]
    pm_c, dt_c = _inproj_call(ctx.reshape(NB_BATCH * CTX, D), mod6, n1, w_main, w_dt, lambda i: 2)
    x2d = x.reshape(N_TOK, D)
    pm_l, dt_l = _inproj_call(x2d, mod6, n1, w_main, w_dt, lambda i: i // (SEQ // TM_INPROJ))

    h0 = _ctx_state_call(pm_c, dt_c, ssd_consts)
    dsk = jnp.repeat(d_skip[0], SSD_P)[None, :]
    bs_e = jnp.repeat(jnp.swapaxes(b_spatial[0], 0, 1), CM_HD, axis=1)
    ycat = _ssd_main_call(pm_l, dt_l, ssd_consts, h0, dsk, ssd_norm_w[0][None, :], cmlp_norm_w[0][None, :],
                          w_spatial[0].astype(bf16), bs_e)

    base, h2, logits_t = _outproj_call(ycat, x2d, mod6, norm2_w[0][None, :], w_out[0].astype(bf16),
                                       jnp.swapaxes(w_router[0], 0, 1).astype(bf16),
                                       w_shared_gate_up[0].astype(bf16), w_shared_down[0].astype(bf16))
    h2 = h2.reshape(N_TOK, ROW_CHUNKS, LANES)

    dest, gates, bexp, bend = _route_call(logits_t, router_bias[0][:, None])
    bend = bend[:, 0]
    nact = bend[N_EXP - 1:]
    tok = jnp.broadcast_to(jnp.arange(N_TOK, dtype=i32)[None, :], (TOP_K, N_TOK))
    slot_tok = jnp.zeros((N_BLOCKS * ROWS_PER_BLOCK,), i32).at[dest.reshape(-1)].set(
        tok.reshape(-1), unique_indices=True, mode="promise_in_bounds")
    ys = _experts_call(bexp.reshape(N_BLOCKS), bend, nact, slot_tok.reshape(N_BLOCKS, 1, ROWS_PER_BLOCK),
                       h2, w_gate_up[0], w_down[0]).reshape(N_BLOCKS * ROWS_PER_BLOCK, ROW_CHUNKS, LANES)
    out = _combine_call(_tile_major(dest, TT_COMB), ys, gates.T, base, mod6, final_norm_w[None, :])
    return out.reshape(NB_BATCH, SEQ, D)
```

```python
import functools

import jax
import jax.numpy as jnp
from jax import lax
from jax.experimental import pallas as pl
from jax.experimental.pallas import tpu as pltpu

f32 = jnp.float32
bf16 = jnp.bfloat16
i32 = jnp.int32

D = 2048
NB_BATCH = 2
SEQ = 4096
CTX = 256
N_MOD = 6
SSD_W = 1024
SSD_H = 16
SSD_P = 64
SSD_G = 2
D_STATE = 128
CHUNK = 128
D_CONV = 4
BC_W = 2 * SSD_G * D_STATE
CM_W = 1024
CM_H = 8
CM_HD = 128
OFF_Z = SSD_W
OFF_XBC = OFF_Z + SSD_W + BC_W
OFF_DTF = OFF_XBC + SSD_H
OFF_DTB = OFF_DTF + SSD_H
OFF_U = OFF_DTB + CM_W
IN_PROJ_DIM = OFF_U + CM_W
N_EXP = 256
TOP_K = 8
N_GRP = 8
TOP_GRP = 4
GRP_SZ = N_EXP // N_GRP
D_EXP = 512
ROUTED_SCALE = 2.5
EPS = 1e-6

P_MAIN = 4 * 1024 + BC_W
DT_W = 128

LANES = 128
SUBLANES = 8
VMEM_LIMIT = 56 * 1024 * 1024

TM_INPROJ = 1024
TN_INPROJ = 512
TM_OUT = 256
TM_SHARED = 512
TT_ROUTE = 512
ROWS_PER_BLOCK = 128
N_TOK = NB_BATCH * SEQ
N_BLOCKS = N_TOK * TOP_K // ROWS_PER_BLOCK + N_EXP
TT_COMB = 64


def _cparams(sem):
    return pltpu.CompilerParams(dimension_semantics=sem, vmem_limit_bytes=VMEM_LIMIT)


def _sigmoid(x):
    return 0.5 * jnp.tanh(0.5 * x) + 0.5


def _silu(x):
    return x * _sigmoid(x)


def _dot(a, b):
    return jnp.dot(a, b, preferred_element_type=f32)


def _mod_kernel(c_ref, w_ref, b_ref, o_ref):
    a = _silu(c_ref[...]).astype(bf16)
    o_ref[...] = _dot(a, w_ref[...].astype(bf16)) + b_ref[...]


def _mod_call(cs, w_mod, b_mod):
    tn = 1024
    n = w_mod.shape[1]
    return pl.pallas_call(
        _mod_kernel,
        grid=(n // tn,),
        in_specs=[
            pl.BlockSpec((SUBLANES, D), lambda j: (0, 0)),
            pl.BlockSpec((D, tn), lambda j: (0, j)),
            pl.BlockSpec((1, tn), lambda j: (0, j)),
        ],
        out_specs=pl.BlockSpec((SUBLANES, tn), lambda j: (0, j)),
        out_shape=jax.ShapeDtypeStruct((SUBLANES, n), f32),
        compiler_params=_cparams(("arbitrary",)),
        name="mod",
    )(cs, w_mod, b_mod)


def _inproj_kernel(x_ref, sh_ref, sc_ref, nw_ref, w_ref, wdt_ref, p_ref, dt_ref, h_scr):
    @pl.when(pl.program_id(1) == 0)
    def _():
        x = x_ref[...]
        y = x * lax.rsqrt(jnp.mean(x * x, axis=-1, keepdims=True) + EPS) * nw_ref[...]
        h = (y * (1.0 + sc_ref[0, 0]) + sh_ref[0, 0]).astype(bf16)
        h_scr[...] = h
        dt_ref[...] = _dot(h, wdt_ref[...])

    p_ref[...] = _dot(h_scr[...], w_ref[...])


def _inproj_call(x2d, mod6, norm_w, w_main, w_dt, mod_row_fn):
    m = x2d.shape[0]
    tm, tn = min(TM_INPROJ, m), TN_INPROJ
    return pl.pallas_call(
        _inproj_kernel,
        grid=(m // tm, P_MAIN // tn),
        in_specs=[
            pl.BlockSpec((tm, D), lambda i, j: (i, 0)),
            pl.BlockSpec((1, 1, 1, D), lambda i, j: (mod_row_fn(i), 0, 0, 0)),
            pl.BlockSpec((1, 1, 1, D), lambda i, j: (mod_row_fn(i), 1, 0, 0)),
            pl.BlockSpec((1, D), lambda i, j: (0, 0)),
            pl.BlockSpec((D, tn), lambda i, j: (0, j)),
            pl.BlockSpec((D, DT_W), lambda i, j: (0, 0)),
        ],
        out_specs=[
            pl.BlockSpec((tm, tn), lambda i, j: (i, j)),
            pl.BlockSpec((tm, DT_W), lambda i, j: (i, 0)),
        ],
        out_shape=[
            jax.ShapeDtypeStruct((m, P_MAIN), f32),
            jax.ShapeDtypeStruct((m, DT_W), f32),
        ],
        scratch_shapes=[pltpu.VMEM((tm, D), bf16)],
        compiler_params=_cparams(("arbitrary", "arbitrary")),
        name="inproj",
    )(x2d, mod6, mod6, norm_w, w_main, w_dt)


def _split3(q):
    hi = q.astype(bf16)
    r1 = q - hi.astype(f32)
    mid = r1.astype(bf16)
    lo = (r1 - mid.astype(f32)).astype(bf16)
    return hi, mid, lo


def _split2_cat(q):
    hi = q.astype(bf16)
    lo = (q - hi.astype(f32)).astype(bf16)
    return jnp.concatenate([hi, lo], axis=1)


def _conv_silu(buf, w_ref, b_ref):
    w = w_ref[...]
    y = (w[0:1] * buf[7:135, :] + w[1:2] * buf[8:136, :] + w[2:3] * buf[9:137, :]
         + w[3:4] * buf[10:138, :] + b_ref[...])
    return _silu(y)


def _ssd_chunk(ph, first, last, xm_ref, xp_ref, xn_ref, bm_ref, bp_ref, bn_ref, dt_ref,
               cwx_ref, cbx_ref, cwb_ref, cbb_ref, dtb_ref, alog_ref, e2_ref,
               s_scr, xbuf, bcbuf, with_output):
    zero8x = jnp.zeros((SUBLANES, SSD_W), f32)
    zero8b = jnp.zeros((SUBLANES, BC_W), f32)
    xbuf[0:8, :] = jnp.where(first, zero8x, xp_ref[...])
    xbuf[8:136, :] = xm_ref[...]
    xbuf[136:144, :] = jnp.where(last, zero8x, xn_ref[...])
    bcbuf[0:8, :] = jnp.where(first, zero8b, bp_ref[...])
    bcbuf[8:136, :] = bm_ref[...]
    bcbuf[136:144, :] = jnp.where(last, zero8b, bn_ref[...])
    xs = _conv_silu(xbuf, cwx_ref, cbx_ref)
    bcv = _conv_silu(bcbuf, cwb_ref, cbb_ref)

    dtr = dt_ref[...]
    dtr = jnp.where(ph == 0, dtr, pltpu.roll(dtr, LANES - SSD_H, axis=1))
    pre = dtr + dtb_ref[0]
    dt = jnp.maximum(pre, 0.0) + jnp.log(1.0 + jnp.exp(-jnp.abs(pre)))
    d_a = dt * (-jnp.exp(alog_ref[0]))

    row = lax.broadcasted_iota(i32, (CHUNK, CHUNK), 0)
    col = lax.broadcasted_iota(i32, (CHUNK, CHUNK), 1)
    mask = jnp.where(ph == 0, row - col, col - row) >= 0
    tri = mask.astype(f32).astype(bf16)
    hi, mid, lo = _split3(d_a)
    acs = _dot(tri, hi) + _dot(tri, mid) + _dot(tri, lo)
    tot = jnp.sum(d_a, axis=0, keepdims=True)
    e_a = jnp.exp(acs)
    dte = jnp.exp(tot - acs)
    cd = jnp.broadcast_to(jnp.exp(tot), (SUBLANES, LANES))
    ex = _dot(jnp.concatenate([_split2_cat(dt), _split2_cat(e_a), _split2_cat(dte), _split2_cat(cd)], axis=0),
              e2_ref[...])
    dt_e = ex[0:128]
    ea_e = ex[128:256]
    dte_e = ex[256:384]
    cd_e = ex[384:385]
    xdt = xs * dt_e

    y_parts = []
    if with_output:
        acs_t = acs.T
        lane_lo = col < SSD_P
        g_mats = []
        for g in range(SSD_G):
            b_g = bcv[:, g * D_STATE:(g + 1) * D_STATE].astype(bf16)
            c_g = bcv[:, (SSD_G + g) * D_STATE:(SSD_G + g + 1) * D_STATE].astype(bf16)
            g_mats.append(lax.dot_general(c_g, b_g, (((1,), (1,)), ((), ())), preferred_element_type=f32))
        for p in range(SSD_H // 2):
            g = (2 * p) // (SSD_H // SSD_G)
            xp = xdt[:, p * LANES:(p + 1) * LANES]
            acc = None
            for k in range(2):
                h = 2 * p + k
                seg = jnp.broadcast_to(acs[:, h:h + 1], (CHUNK, CHUNK)) - jnp.broadcast_to(acs_t[h:h + 1, :], (CHUNK, CHUNK))
                dec = jnp.where(mask, jnp.exp(jnp.minimum(seg, 0.0)), 0.0)
                m_h = (g_mats[g] * dec).astype(bf16)
                x_h = jnp.where(lane_lo if k == 0 else jnp.logical_not(lane_lo), xp, 0.0).astype(bf16)
                t = _dot(m_h, x_h)
                acc = t if acc is None else acc + t
            y_parts.append(acc)

    gw = SSD_W // SSD_G
    y_off = []
    for g in range(SSD_G):
        sl = slice(g * gw, (g + 1) * gw)
        b_g = bcv[:, g * D_STATE:(g + 1) * D_STATE].astype(bf16)
        s_g = s_scr[g]
        if with_output:
            c_g = bcv[:, (SSD_G + g) * D_STATE:(SSD_G + g + 1) * D_STATE].astype(bf16)
            y_off.append(_dot(c_g, s_g.astype(bf16)) * ea_e[:, sl])
        upd = lax.dot_general(b_g, (xdt[:, sl] * dte_e[:, sl]).astype(bf16), (((0,), (0,)), ((), ())),
                              preferred_element_type=f32)
        s_scr[g] = s_g * cd_e[:, sl] + upd
    if not with_output:
        return xs, None
    y = jnp.concatenate(y_parts, axis=1) + jnp.concatenate(y_off, axis=1)
    return xs, y


def _ctx_state_kernel(xm_ref, xp_ref, xn_ref, bm_ref, bp_ref, bn_ref, dt_ref,
                      cwx_ref, cbx_ref, cwb_ref, cbb_ref, dtb_ref, alog_ref, e2_ref,
                      h_ref, s_scr, xbuf, bcbuf, *, nc):
    ph = pl.program_id(1)
    c = pl.program_id(2)
    ci = jnp.where(ph == 0, c, nc - 1 - c)

    @pl.when(c == 0)
    def _():
        s_scr[...] = jnp.zeros_like(s_scr)

    _ssd_chunk(ph, ci == 0, ci == nc - 1, xm_ref, xp_ref, xn_ref, bm_ref, bp_ref, bn_ref, dt_ref,
               cwx_ref, cbx_ref, cwb_ref, cbb_ref, dtb_ref, alog_ref, e2_ref, s_scr, xbuf, bcbuf, False)

    @pl.when(c == nc - 1)
    def _():
        h_ref[0, 0] = s_scr[...]


def _gelu_tanh(x):
    return 0.5 * x * (1.0 + jnp.tanh(0.7978845608028654 * (x + 0.044715 * x * x * x)))


def _ssd_main_kernel(xm_ref, xp_ref, xn_ref, bm_ref, bp_ref, bn_ref, dt_ref,
                     cwx_ref, cbx_ref, cwb_ref, cbb_ref, dtb_ref, alog_ref, e2_ref,
                     h0_ref, z_ref, u_ref, v_ref, dsk_ref, snw_ref, cnw_ref, ws_ref, bs_ref,
                     o_ref, s_scr, xbuf, bcbuf, yf_scr, *, nc):
    ph = pl.program_id(1)
    c = pl.program_id(2)
    ci = jnp.where(ph == 0, c, nc - 1 - c)

    @pl.when(c == 0)
    def _():
        s_scr[...] = h0_ref[0, 0]

    xs, y = _ssd_chunk(ph, ci == 0, ci == nc - 1, xm_ref, xp_ref, xn_ref, bm_ref, bp_ref, bn_ref, dt_ref,
                       cwx_ref, cbx_ref, cwb_ref, cbb_ref, dtb_ref, alog_ref, e2_ref, s_scr, xbuf, bcbuf, True)
    r0 = pl.multiple_of(ci * CHUNK, CHUNK)

    @pl.when(ph == 0)
    def _():
        yf_scr[pl.ds(r0, CHUNK), :] = y

    @pl.when(ph == 1)
    def _():
        yt = (yf_scr[pl.ds(r0, CHUNK), :] + y + dsk_ref[...] * xs) * _silu(z_ref[...])
        yt = yt * lax.rsqrt(jnp.mean(yt * yt, axis=-1, keepdims=True) + EPS) * snw_ref[...]
        o_ref[:, 0:SSD_W] = yt.astype(bf16)
        u = _gelu_tanh(u_ref[...])
        v = _gelu_tanh(v_ref[...])
        vn = (v * lax.rsqrt(jnp.mean(v * v, axis=-1, keepdims=True) + EPS) * cnw_ref[...]).astype(bf16)
        mixed = [_dot(ws_ref[h], vn[:, h * CM_HD:(h + 1) * CM_HD]) for h in range(CM_H)]
        o_ref[:, SSD_W:SSD_W + CM_W] = (u * (jnp.concatenate(mixed, axis=1) + bs_ref[...])).astype(bf16)


def _ssd_common_specs(nc, row_blocks_total):
    cpb = nc
    sub = CHUNK // SUBLANES

    def ci_of(ph, c):
        return jnp.where(ph == 0, c, nc - 1 - c)

    def main_i(b, ph, c):
        return b * cpb + ci_of(ph, c)

    def prev_i(b, ph, c):
        return jnp.maximum(main_i(b, ph, c) * sub - 1, 0)

    def next_i(b, ph, c):
        return jnp.minimum(main_i(b, ph, c) * sub + sub, row_blocks_total * sub - 1)

    bc_col = (4 * 1024) // BC_W
    specs = [
        pl.BlockSpec((CHUNK, SSD_W), lambda b, ph, c: (main_i(b, ph, c), 0)),
        pl.BlockSpec((SUBLANES, SSD_W), lambda b, ph, c: (prev_i(b, ph, c), 0)),
        pl.BlockSpec((SUBLANES, SSD_W), lambda b, ph, c: (next_i(b, ph, c), 0)),
        pl.BlockSpec((CHUNK, BC_W), lambda b, ph, c: (main_i(b, ph, c), bc_col)),
        pl.BlockSpec((SUBLANES, BC_W), lambda b, ph, c: (prev_i(b, ph, c), bc_col)),
        pl.BlockSpec((SUBLANES, BC_W), lambda b, ph, c: (next_i(b, ph, c), bc_col)),
        pl.BlockSpec((CHUNK, DT_W), lambda b, ph, c: (main_i(b, ph, c), 0)),
        pl.BlockSpec((D_CONV, SSD_W), lambda b, ph, c: (0, 0)),
        pl.BlockSpec((1, SSD_W), lambda b, ph, c: (0, 0)),
        pl.BlockSpec((D_CONV, BC_W), lambda b, ph, c: (0, 0)),
        pl.BlockSpec((1, BC_W), lambda b, ph, c: (0, 0)),
        pl.BlockSpec((1, 1, DT_W), lambda b, ph, c: (ph, 0, 0)),
        pl.BlockSpec((1, 1, DT_W), lambda b, ph, c: (ph, 0, 0)),
        pl.BlockSpec((2 * LANES, SSD_W), lambda b, ph, c: (0, 0)),
    ]
    return specs


def _ssd_scratch():
    return [
        pltpu.VMEM((SSD_G, D_STATE, SSD_W // SSD_G), f32),
        pltpu.VMEM((CHUNK + 2 * SUBLANES, SSD_W), f32),
        pltpu.VMEM((CHUNK + 2 * SUBLANES, BC_W), f32),
    ]


def _ctx_state_call(pm, dtm, ssd_consts):
    nc = CTX // CHUNK
    specs = _ssd_common_specs(nc, NB_BATCH * nc)
    return pl.pallas_call(
        functools.partial(_ctx_state_kernel, nc=nc),
        grid=(NB_BATCH, 2, nc),
        in_specs=specs,
        out_specs=pl.BlockSpec((1, 1, SSD_G, D_STATE, SSD_W // SSD_G), lambda b, ph, c: (b, ph, 0, 0, 0)),
        out_shape=jax.ShapeDtypeStruct((NB_BATCH, 2, SSD_G, D_STATE, SSD_W // SSD_G), f32),
        scratch_shapes=_ssd_scratch(),
        compiler_params=_cparams(("arbitrary", "arbitrary", "arbitrary")),
        name="ssd_ctx",
    )(pm, pm, pm, pm, pm, pm, dtm, *ssd_consts)


def _ssd_main_call(pm, dtm, ssd_consts, h0, dsk, snw, cnw, ws, bs_e):
    nc = SEQ // CHUNK
    specs = _ssd_common_specs(nc, NB_BATCH * nc)

    def zrow(b, ph, c):
        return b * nc + jnp.where(ph == 0, nc - 1, nc - 1 - c)

    specs += [
        pl.BlockSpec((1, 1, SSD_G, D_STATE, SSD_W // SSD_G), lambda b, ph, c: (b, ph, 0, 0, 0)),
        pl.BlockSpec((CHUNK, SSD_W), lambda b, ph, c: (zrow(b, ph, c), 1)),
        pl.BlockSpec((CHUNK, CM_W), lambda b, ph, c: (zrow(b, ph, c), 2)),
        pl.BlockSpec((CHUNK, CM_W), lambda b, ph, c: (zrow(b, ph, c), 3)),
        pl.BlockSpec((1, SSD_W), lambda b, ph, c: (0, 0)),
        pl.BlockSpec((1, SSD_W), lambda b, ph, c: (0, 0)),
        pl.BlockSpec((1, CM_W), lambda b, ph, c: (0, 0)),
        pl.BlockSpec((CM_H, CHUNK, CHUNK), lambda b, ph, c: (0, 0, 0)),
        pl.BlockSpec((CHUNK, CM_W), lambda b, ph, c: (0, 0)),
    ]
    return pl.pallas_call(
        functools.partial(_ssd_main_kernel, nc=nc),
        grid=(NB_BATCH, 2, nc),
        in_specs=specs,
        out_specs=pl.BlockSpec((CHUNK, SSD_W + CM_W), lambda b, ph, c: (zrow(b, ph, c), 0)),
        out_shape=jax.ShapeDtypeStruct((N_TOK, SSD_W + CM_W), bf16),
        scratch_shapes=_ssd_scratch() + [pltpu.VMEM((SEQ, SSD_W), f32)],
        compiler_params=_cparams(("arbitrary", "arbitrary", "arbitrary")),
        name="ssd_main",
    )(pm, pm, pm, pm, pm, pm, dtm, *ssd_consts, h0, pm, pm, pm, dsk, snw, cnw, ws, bs_e)


ROW_CHUNKS = D // LANES


def _store_token_major(ref, val, n_rows, row0=0):
    for c in range(ROW_CHUNKS):
        ref[pl.ds(row0 + c, n_rows, stride=ROW_CHUNKS), :] = val[:, c * LANES:(c + 1) * LANES]


def _load_token_major_chunk(ref, lead, c, n_rows, row0=0):
    idx = pl.ds(row0 + c, n_rows, stride=ROW_CHUNKS)
    return ref[idx, :] if lead is None else ref[lead, idx, :]


def _outproj_kernel(y_ref, x_ref, g_ref, sh_ref, sc_ref, g5_ref, nw_ref, wo_ref, wr_ref, w1_ref, w2_ref,
                    base_ref, h_ref, lg_ref):
    x1 = x_ref[...] + g_ref[0, 0] * _dot(y_ref[...], wo_ref[...])
    y = x1 * lax.rsqrt(jnp.mean(x1 * x1, axis=-1, keepdims=True) + EPS) * nw_ref[...]
    h = y * (1.0 + sc_ref[0, 0]) + sh_ref[0, 0]
    _store_token_major(h_ref, h, h.shape[0])
    hb = h.astype(bf16)
    lg_ref[...] = lax.dot_general(wr_ref[...], hb, (((1,), (1,)), ((), ())), preferred_element_type=f32)
    gu = _dot(hb, w1_ref[...])
    act = (_silu(gu[:, :D_EXP]) * gu[:, D_EXP:]).astype(bf16)
    base_ref[...] = x1 + g5_ref[0, 0] * _dot(act, w2_ref[...])


def _outproj_call(ycat, x2d, mod6, norm_w, w_out, w_router_t, w1, w2):
    tm = TM_OUT
    rb = SEQ // tm
    const = lambda i: (0, 0)
    return pl.pallas_call(
        _outproj_kernel,
        grid=(N_TOK // tm,),
        in_specs=[
            pl.BlockSpec((tm, D), lambda i: (i, 0)),
            pl.BlockSpec((tm, D), lambda i: (i, 0)),
            pl.BlockSpec((1, 1, 1, D), lambda i: (i // rb, 2, 0, 0)),
            pl.BlockSpec((1, 1, 1, D), lambda i: (i // rb, 3, 0, 0)),
            pl.BlockSpec((1, 1, 1, D), lambda i: (i // rb, 4, 0, 0)),
            pl.BlockSpec((1, 1, 1, D), lambda i: (i // rb, 5, 0, 0)),
            pl.BlockSpec((1, D), const),
            pl.BlockSpec((D, D), const),
            pl.BlockSpec((N_EXP, D), const),
            pl.BlockSpec((D, 2 * D_EXP), const),
            pl.BlockSpec((D_EXP, D), const),
        ],
        out_specs=[
            pl.BlockSpec((tm, D), lambda i: (i, 0)),
            pl.BlockSpec((tm * ROW_CHUNKS, LANES), lambda i: (i, 0)),
            pl.BlockSpec((N_EXP, tm), lambda i: (0, i)),
        ],
        out_shape=[
            jax.ShapeDtypeStruct((N_TOK, D), f32),
            jax.ShapeDtypeStruct((N_TOK * ROW_CHUNKS, LANES), f32),
            jax.ShapeDtypeStruct((N_EXP, N_TOK), f32),
        ],
        compiler_params=_cparams(("arbitrary",)),
        name="outproj",
    )(ycat, x2d, mod6, mod6, mod6, mod6, norm_w, w_out, w_router_t, w1, w2)


def _route_kernel(lg_ref, bias_ref, dest_ref, gate_ref, bexp_ref, bend_ref,
                  idx_scr, pos_scr, cnt_scr, *, n_tiles):
    j = pl.program_id(0)
    tt = TT_ROUTE
    neg = -jnp.inf

    @pl.when(j == 0)
    def _():
        cnt_scr[...] = jnp.zeros_like(cnt_scr)

    s = _sigmoid(lg_ref[...])
    biased = s + bias_ref[...]
    v3 = biased.reshape(N_GRP, GRP_SZ, tt)
    io3 = lax.broadcasted_iota(i32, (N_GRP, GRP_SZ, tt), 1)
    m1 = jnp.max(v3, axis=1, keepdims=True)
    i1 = jnp.min(jnp.where(v3 == m1, io3, GRP_SZ), axis=1, keepdims=True)
    m2 = jnp.max(jnp.where(io3 == i1, neg, v3), axis=1, keepdims=True)
    gs = m1 + m2
    gio = lax.broadcasted_iota(i32, (N_GRP, 1, tt), 0)
    sel = jnp.zeros((N_GRP, 1, tt), dtype=jnp.bool_)
    cur_g = gs
    for _ in range(TOP_GRP):
        m = jnp.max(cur_g, axis=0, keepdims=True)
        gi = jnp.min(jnp.where(cur_g == m, gio, N_GRP), axis=0, keepdims=True)
        hit = gio == gi
        sel = jnp.logical_or(sel, hit)
        cur_g = jnp.where(hit, neg, cur_g)
    cur = jnp.where(sel, v3, neg).reshape(N_EXP, tt)

    eio = lax.broadcasted_iota(i32, (N_EXP, tt), 0)
    onehot = jnp.zeros((N_EXP, tt), f32)
    idxs, gts = [], []
    for _ in range(TOP_K):
        m = jnp.max(cur, axis=0, keepdims=True)
        ei = jnp.min(jnp.where(cur == m, eio, N_EXP), axis=0, keepdims=True)
        hit = eio == ei
        gts.append(jnp.sum(jnp.where(hit, s, 0.0), axis=0, keepdims=True))
        idxs.append(ei)
        onehot = jnp.where(hit, 1.0, onehot)
        cur = jnp.where(hit, neg, cur)
    gsum = gts[0]
    for k in range(1, TOP_K):
        gsum = gsum + gts[k]

    r_io = lax.broadcasted_iota(i32, (tt, tt), 0)
    c_io = lax.broadcasted_iota(i32, (tt, tt), 1)
    upper = (r_io <= c_io).astype(f32).astype(bf16)
    oh = onehot.astype(bf16)
    incl = _dot(oh, upper)
    base = cnt_scr[...]
    pos = incl - 1.0 + jnp.concatenate([base] * (tt // LANES), axis=1)
    cnt_scr[...] = base + _dot(oh, jnp.ones((tt, LANES), bf16))
    c0 = pl.multiple_of(j * tt, tt)
    for k in range(TOP_K):
        hit = eio == idxs[k]
        idx_scr[k:k + 1, pl.ds(c0, tt)] = idxs[k]
        pos_scr[k:k + 1, pl.ds(c0, tt)] = jnp.sum(jnp.where(hit, pos, 0.0), axis=0, keepdims=True)
        gate_ref[k:k + 1, pl.ds(c0, tt)] = gts[k] / gsum * ROUTED_SCALE

    @pl.when(j == n_tiles - 1)
    def _():
        cnt = cnt_scr[...]
        r = float(ROWS_PER_BLOCK)
        nblk = jnp.floor((cnt + (r - 1.0)) * (1.0 / r))
        er = lax.broadcasted_iota(i32, (N_EXP, N_EXP), 0)
        ec = lax.broadcasted_iota(i32, (N_EXP, N_EXP), 1)
        lower = (er >= ec).astype(f32).astype(bf16)
        bend = _dot(lower, nblk.astype(bf16))
        pstart = (bend - nblk) * r
        pstart_t = jnp.concatenate([pstart] * (tt // LANES), axis=1)
        def slots(t, carry):
            c1 = pl.multiple_of(t * tt, tt)
            for k in range(TOP_K):
                hit = eio == idx_scr[k:k + 1, pl.ds(c1, tt)]
                d = jnp.sum(jnp.where(hit, pstart_t, 0.0), axis=0, keepdims=True) + pos_scr[k:k + 1, pl.ds(c1, tt)]
                dest_ref[k:k + 1, pl.ds(c1, tt)] = d.astype(i32)
            return carry

        lax.fori_loop(0, n_tiles, slots, 0)
        bio = lax.broadcasted_iota(i32, (N_EXP, N_BLOCKS), 1).astype(f32)
        bend_t = jnp.concatenate([bend] * (N_BLOCKS // LANES), axis=1)
        be = jnp.sum(jnp.where(bend_t <= bio, 1.0, 0.0), axis=0, keepdims=True)
        bexp_ref[...] = jnp.minimum(be, float(N_EXP - 1)).astype(i32)
        bend_ref[...] = bend.astype(i32)


def _route_call(logits_t, bias_col):
    tt = TT_ROUTE
    n_tiles = N_TOK // tt
    return pl.pallas_call(
        functools.partial(_route_kernel, n_tiles=n_tiles),
        grid=(n_tiles,),
        in_specs=[
            pl.BlockSpec((N_EXP, tt), lambda j: (0, j)),
            pl.BlockSpec((N_EXP, 1), lambda j: (0, 0)),
        ],
        out_specs=[
            pl.BlockSpec((TOP_K, N_TOK), lambda j: (0, 0)),
            pl.BlockSpec((TOP_K, N_TOK), lambda j: (0, 0)),
            pl.BlockSpec((1, N_BLOCKS), lambda j: (0, 0)),
            pl.BlockSpec((N_EXP, LANES), lambda j: (0, 0)),
        ],
        out_shape=[
            jax.ShapeDtypeStruct((TOP_K, N_TOK), i32),
            jax.ShapeDtypeStruct((TOP_K, N_TOK), f32),
            jax.ShapeDtypeStruct((1, N_BLOCKS), i32),
            jax.ShapeDtypeStruct((N_EXP, LANES), i32),
        ],
        scratch_shapes=[
            pltpu.VMEM((TOP_K, N_TOK), i32),
            pltpu.VMEM((TOP_K, N_TOK), f32),
            pltpu.VMEM((N_EXP, LANES), f32),
        ],
        compiler_params=_cparams(("arbitrary",)),
        name="route",
    )(logits_t, bias_col)


def _gather_rows(idx_ref, src_hbm, dst, sem, n_rows, priorities=(0, 1)):
    for r in range(n_rows):
        t = idx_ref[0, 0, r]
        pltpu.make_async_copy(src_hbm.at[t], dst.at[pl.ds(r * ROW_CHUNKS, ROW_CHUNKS), :], sem).start(
            priority=priorities[r % len(priorities)])


def _gather_rows_wait(src_hbm, dst, sem, n_rows):
    def body(r, carry):
        pltpu.make_async_copy(src_hbm.at[0], dst.at[pl.ds(0, ROW_CHUNKS), :], sem).wait()
        return carry
    lax.fori_loop(0, n_rows, body, 0, unroll=8)


W_CHUNKS = 4


def _experts_kernel(bexp_ref, bend_ref, nact_ref, tok0_ref, tokn_ref, h_hbm, wgu_hbm, wd_hbm, o_ref,
                    xbuf, xsem, wgu_buf, wd_buf, wsem, ord_scr):
    g = pl.program_id(0)
    na = nact_ref[0]
    e = bexp_ref[g]
    r = ROWS_PER_BLOCK
    xslot = g % 2

    def fetch(ex, slot):
        cps = []
        for q in range(W_CHUNKS):
            a = D // W_CHUNKS
            cps.append(pltpu.make_async_copy(wgu_hbm.at[ex, pl.ds(q * a, a), :],
                                             wgu_buf.at[slot, pl.ds(q * a, a), :], wsem.at[slot]))
            a = D_EXP // W_CHUNKS
            cps.append(pltpu.make_async_copy(wd_hbm.at[ex, pl.ds(q * a, a), :],
                                             wd_buf.at[slot, pl.ds(q * a, a), :], wsem.at[slot]))
        return cps

    def start_fetch(ex, slot):
        for cp in fetch(ex, slot):
            cp.start(priority=1)

    @pl.when(g == 0)
    def _():
        ord_scr[0] = 0
        start_fetch(e, 0)
        _gather_rows(tok0_ref, h_hbm, xbuf.at[0], xsem.at[0], r, priorities=(0,))

    @pl.when(g + 1 < na)
    def _():
        _gather_rows(tokn_ref, h_hbm, xbuf.at[1 - xslot], xsem.at[1 - xslot], r, priorities=(0,))

    first = jnp.logical_or(g == 0, e != bexp_ref[jnp.maximum(g - 1, 0)])

    @pl.when(jnp.logical_and(g < na, first))
    def _():
        k = ord_scr[0]
        slot = k % 2
        for cp in fetch(e, slot):
            cp.wait()
        nxt = bend_ref[e]

        @pl.when(nxt < na)
        def _():
            start_fetch(bexp_ref[jnp.minimum(nxt, N_BLOCKS - 1)], 1 - slot)

        ord_scr[0] = k + 1

    @pl.when(g < na)
    def _():
        _gather_rows_wait(h_hbm, xbuf.at[xslot], xsem.at[xslot], r)
        ws = (ord_scr[0] + 1) % 2
        dn = (((1,), (0,)), ((), ()))
        gu = None
        for p in range(ROW_CHUNKS // 2):
            xp = jnp.concatenate([_load_token_major_chunk(xbuf, xslot, 2 * p, r),
                                  _load_token_major_chunk(xbuf, xslot, 2 * p + 1, r)], axis=1).astype(bf16)
            t = lax.dot_general(xp, wgu_buf[ws, 2 * p * LANES:(2 * p + 2) * LANES, :], dn, preferred_element_type=f32)
            gu = t if gu is None else gu + t
        act = (_silu(gu[:, :D_EXP]) * gu[:, D_EXP:]).astype(bf16)
        _store_token_major(o_ref, lax.dot_general(act, wd_buf[ws], dn, preferred_element_type=f32), r)

    @pl.when(g >= na)
    def _():
        o_ref[...] = jnp.zeros_like(o_ref)


def _experts_call(bexp, bend, nact, slot_tok3, h2, w_gate_up, w_down):
    r = ROWS_PER_BLOCK
    nb = N_BLOCKS
    grid_spec = pltpu.PrefetchScalarGridSpec(
        num_scalar_prefetch=3,
        grid=(nb,),
        in_specs=[
            pl.BlockSpec((1, 1, r), lambda g, *_: (0, 0, 0), memory_space=pltpu.SMEM),
            pl.BlockSpec((1, 1, r), lambda g, *_: (jnp.minimum(g + 1, nb - 1), 0, 0), memory_space=pltpu.SMEM),
            pl.BlockSpec(memory_space=pl.ANY),
            pl.BlockSpec(memory_space=pl.ANY),
            pl.BlockSpec(memory_space=pl.ANY),
        ],
        out_specs=pl.BlockSpec((r * ROW_CHUNKS, LANES), lambda g, *_: (g, 0)),
        scratch_shapes=[
            pltpu.VMEM((2, r * ROW_CHUNKS, LANES), f32),
            pltpu.SemaphoreType.DMA((2,)),
            pltpu.VMEM((2, D, 2 * D_EXP), f32),
            pltpu.VMEM((2, D_EXP, D), f32),
            pltpu.SemaphoreType.DMA((2,)),
            pltpu.SMEM((1,), i32),
        ],
    )
    return pl.pallas_call(
        _experts_kernel,
        grid_spec=grid_spec,
        out_shape=jax.ShapeDtypeStruct((nb * r * ROW_CHUNKS, LANES), f32),
        compiler_params=_cparams(("arbitrary",)),
        name="experts",
    )(bexp, bend, nact, slot_tok3, slot_tok3, h2, w_gate_up, w_down)


def _combine_kernel(d0_ref, dn_ref, ys_hbm, gate_ref, base_ref, g5_ref, nw_ref, o_ref, buf, sem, *, n_tiles):
    i = pl.program_id(0)
    tt = TT_COMB
    n_rows = TOP_K * tt
    slot = i % 2

    @pl.when(i == 0)
    def _():
        _gather_rows(d0_ref, ys_hbm, buf.at[0], sem.at[0], n_rows)

    @pl.when(i + 1 < n_tiles)
    def _():
        _gather_rows(dn_ref, ys_hbm, buf.at[1 - slot], sem.at[1 - slot], n_rows)

    _gather_rows_wait(ys_hbm, buf.at[slot], sem.at[slot], n_rows)
    g = gate_ref[...]
    gcols = [jnp.broadcast_to(g[:, k:k + 1], (tt, LANES)) for k in range(TOP_K)]
    parts = []
    for c in range(ROW_CHUNKS):
        a = gcols[0] * _load_token_major_chunk(buf, slot, c, tt)
        for k in range(1, TOP_K):
            a = a + gcols[k] * _load_token_major_chunk(buf, slot, c, tt, row0=k * tt * ROW_CHUNKS)
        parts.append(a)
    x2 = base_ref[...] + g5_ref[0, 0] * jnp.concatenate(parts, axis=1)
    o_ref[...] = x2 * lax.rsqrt(jnp.mean(x2 * x2, axis=-1, keepdims=True) + EPS) * nw_ref[...]


def _combine_call(dest3, ys, gates_tk, base, mod6, final_w):
    tt = TT_COMB
    n_tiles = N_TOK // tt
    rb = SEQ // tt
    return pl.pallas_call(
        functools.partial(_combine_kernel, n_tiles=n_tiles),
        grid=(n_tiles,),
        in_specs=[
            pl.BlockSpec((1, 1, TOP_K * tt), lambda i: (0, 0, 0), memory_space=pltpu.SMEM),
            pl.BlockSpec((1, 1, TOP_K * tt), lambda i: (jnp.minimum(i + 1, n_tiles - 1), 0, 0),
                         memory_space=pltpu.SMEM),
            pl.BlockSpec(memory_space=pl.ANY),
            pl.BlockSpec((tt, TOP_K), lambda i: (i, 0)),
            pl.BlockSpec((tt, D), lambda i: (i, 0)),
            pl.BlockSpec((1, 1, 1, D), lambda i: (i // rb, 5, 0, 0)),
            pl.BlockSpec((1, D), lambda i: (0, 0)),
        ],
        out_specs=pl.BlockSpec((tt, D), lambda i: (i, 0)),
        out_shape=jax.ShapeDtypeStruct((N_TOK, D), f32),
        scratch_shapes=[
            pltpu.VMEM((2, TOP_K * tt * ROW_CHUNKS, LANES), f32),
            pltpu.SemaphoreType.DMA((2,)),
        ],
        compiler_params=_cparams(("arbitrary",)),
        name="combine",
    )(dest3, dest3, ys, gates_tk, base, mod6, final_w)


def _tile_major(a, tt):
    n = a.shape[1] // tt
    return a.reshape(TOP_K, n, tt).transpose(1, 0, 2).reshape(n, 1, TOP_K * tt)


def kernel(x, c, ctx, c_ctx, w_mod, b_mod, norm1_w, w_in, conv_w, conv_b, dt_bias_f, dt_bias_b, a_log_f, a_log_b, d_skip, ssd_norm_w, cmlp_norm_w, w_spatial, b_spatial, w_out, norm2_w, w_router, router_bias, w_gate_up, w_down, w_shared_gate_up, w_shared_down, final_norm_w):
    assert x.shape == (NB_BATCH, SEQ, D) and ctx.shape == (NB_BATCH, CTX, D) and w_mod.shape[0] == 1

    cs = jnp.concatenate([c, c_ctx[None, :], jnp.zeros((SUBLANES - NB_BATCH - 1, D), f32)], axis=0)
    mod6 = _mod_call(cs, w_mod[0], b_mod).reshape(SUBLANES, N_MOD, 1, D)

    wi = w_in[0]
    w_main = jnp.concatenate([wi[:, OFF_Z:OFF_Z + SSD_W], wi[:, 0:OFF_Z], wi[:, OFF_DTB:OFF_U], wi[:, OFF_U:],
                              wi[:, OFF_Z + SSD_W:OFF_XBC]], axis=1).astype(bf16)
    w_dt = jnp.pad(wi[:, OFF_XBC:OFF_DTB], ((0, 0), (0, DT_W - 2 * SSD_H))).astype(bf16)
    pad_h = DT_W - SSD_H
    dtb = jnp.stack([jnp.pad(dt_bias_f[0], (0, pad_h)), jnp.pad(dt_bias_b[0], (0, pad_h))])[:, None, :]
    alog = jnp.stack([jnp.pad(a_log_f[0], (0, pad_h)), jnp.pad(a_log_b[0], (0, pad_h))])[:, None, :]
    e_rows = lax.broadcasted_iota(i32, (2 * LANES, SSD_W), 0) % LANES
    e_cols = lax.broadcasted_iota(i32, (2 * LANES, SSD_W), 1) // SSD_P
    e2 = (e_rows == e_cols).astype(bf16)
    ssd_consts = (conv_w[0][:, :SSD_W], conv_b[0][None, :SSD_W], conv_w[0][:, SSD_W:], conv_b[0][None, SSD_W:],
                  dtb, alog, e2)

    n1 = norm1_w[0][None, :]
    pm_c, dt_c = _inproj_call(ctx.reshape(NB_BATCH * CTX, D), mod6, n1, w_main, w_dt, lambda i: 2)
    x2d = x.reshape(N_TOK, D)
    pm_l, dt_l = _inproj_call(x2d, mod6, n1, w_main, w_dt, lambda i: i // (SEQ // TM_INPROJ))

    h0 = _ctx_state_call(pm_c, dt_c, ssd_consts)
    dsk = jnp.repeat(d_skip[0], SSD_P)[None, :]
    bs_e = jnp.repeat(jnp.swapaxes(b_spatial[0], 0, 1), CM_HD, axis=1)
    ycat = _ssd_main_call(pm_l, dt_l, ssd_consts, h0, dsk, ssd_norm_w[0][None, :], cmlp_norm_w[0][None, :],
                          w_spatial[0].astype(bf16), bs_e)

    base, h2, logits_t = _outproj_call(ycat, x2d, mod6, norm2_w[0][None, :], w_out[0].astype(bf16),
                                       jnp.swapaxes(w_router[0], 0, 1).astype(bf16),
                                       w_shared_gate_up[0].astype(bf16), w_shared_down[0].astype(bf16))
    h2 = h2.reshape(N_TOK, ROW_CHUNKS, LANES)

    dest, gates, bexp, bend = _route_call(logits_t, router_bias[0][:, None])
    bend = bend[:, 0]
    nact = bend[N_EXP - 1:]
    tok = jnp.broadcast_to(jnp.arange(N_TOK, dtype=i32)[None, :], (TOP_K, N_TOK))
    slot_tok = jnp.zeros((N_BLOCKS * ROWS_PER_BLOCK,), i32).at[dest.reshape(-1)].set(
        tok.reshape(-1), unique_indices=True, mode="promise_in_bounds")
    ys = _experts_call(bexp.reshape(N_BLOCKS), bend, nact, slot_tok.reshape(N_BLOCKS, 1, ROWS_PER_BLOCK),
                       h2, w_gate_up[0], w_down[0]).reshape(N_BLOCKS * ROWS_PER_BLOCK, ROW_CHUNKS, LANES)
    out = _combine_call(_tile_major(dest, TT_COMB), ys, gates.T, base, mod6, final_norm_w[None, :])
    return out.reshape(NB_BATCH, SEQ, D)
```

```python
import functools

import jax
import jax.numpy as jnp
from jax import lax
from jax.experimental import pallas as pl
from jax.experimental.pallas import tpu as pltpu

f32 = jnp.float32
bf16 = jnp.bfloat16
i32 = jnp.int32

D = 2048
NB_BATCH = 2
SEQ = 4096
CTX = 256
N_MOD = 6
SSD_W = 1024
SSD_H = 16
SSD_P = 64
SSD_G = 2
D_STATE = 128
CHUNK = 128
D_CONV = 4
BC_W = 2 * SSD_G * D_STATE
CM_W = 1024
CM_H = 8
CM_HD = 128
OFF_Z = SSD_W
OFF_XBC = OFF_Z + SSD_W + BC_W
OFF_DTF = OFF_XBC + SSD_H
OFF_DTB = OFF_DTF + SSD_H
OFF_U = OFF_DTB + CM_W
IN_PROJ_DIM = OFF_U + CM_W
N_EXP = 256
TOP_K = 8
N_GRP = 8
TOP_GRP = 4
GRP_SZ = N_EXP // N_GRP
D_EXP = 512
ROUTED_SCALE = 2.5
EPS = 1e-6

P_MAIN = 4 * 1024 + BC_W
DT_W = 128

LANES = 128
SUBLANES = 8
VMEM_LIMIT = 56 * 1024 * 1024

TM_INPROJ = 1024
TN_INPROJ = 512
TM_OUT = 256
TM_SHARED = 512
TT_ROUTE = 512
ROWS_PER_BLOCK = 128
N_TOK = NB_BATCH * SEQ
N_BLOCKS = N_TOK * TOP_K // ROWS_PER_BLOCK + N_EXP
TT_COMB = 64


def _cparams(sem):
    return pltpu.CompilerParams(dimension_semantics=sem, vmem_limit_bytes=VMEM_LIMIT)


def _sigmoid(x):
    return 0.5 * jnp.tanh(0.5 * x) + 0.5


def _silu(x):
    return x * _sigmoid(x)


def _dot(a, b):
    return jnp.dot(a, b, preferred_element_type=f32)


def _mod_kernel(c_ref, w_ref, b_ref, o_ref):
    a = _silu(c_ref[...]).astype(bf16)
    o_ref[...] = _dot(a, w_ref[...].astype(bf16)) + b_ref[...]


def _mod_call(cs, w_mod, b_mod):
    tn = 1024
    n = w_mod.shape[1]
    return pl.pallas_call(
        _mod_kernel,
        grid=(n // tn,),
        in_specs=[
            pl.BlockSpec((SUBLANES, D), lambda j: (0, 0)),
            pl.BlockSpec((D, tn), lambda j: (0, j)),
            pl.BlockSpec((1, tn), lambda j: (0, j)),
        ],
        out_specs=pl.BlockSpec((SUBLANES, tn), lambda j: (0, j)),
        out_shape=jax.ShapeDtypeStruct((SUBLANES, n), f32),
        compiler_params=_cparams(("arbitrary",)),
        name="mod",
    )(cs, w_mod, b_mod)


def _inproj_kernel(x_ref, sh_ref, sc_ref, nw_ref, w_ref, wdt_ref, p_ref, dt_ref, h_scr):
    @pl.when(pl.program_id(1) == 0)
    def _():
        x = x_ref[...]
        y = x * lax.rsqrt(jnp.mean(x * x, axis=-1, keepdims=True) + EPS) * nw_ref[...]
        h = (y * (1.0 + sc_ref[0, 0]) + sh_ref[0, 0]).astype(bf16)
        h_scr[...] = h
        dt_ref[...] = _dot(h, wdt_ref[...])

    p_ref[...] = _dot(h_scr[...], w_ref[...])


def _inproj_call(x2d, mod6, norm_w, w_main, w_dt, mod_row_fn):
    m = x2d.shape[0]
    tm, tn = min(TM_INPROJ, m), TN_INPROJ
    return pl.pallas_call(
        _inproj_kernel,
        grid=(m // tm, P_MAIN // tn),
        in_specs=[
            pl.BlockSpec((tm, D), lambda i, j: (i, 0)),
            pl.BlockSpec((1, 1, 1, D), lambda i, j: (mod_row_fn(i), 0, 0, 0)),
            pl.BlockSpec((1, 1, 1, D), lambda i, j: (mod_row_fn(i), 1, 0, 0)),
            pl.BlockSpec((1, D), lambda i, j: (0, 0)),
            pl.BlockSpec((D, tn), lambda i, j: (0, j)),
            pl.BlockSpec((D, DT_W), lambda i, j: (0, 0)),
        ],
        out_specs=[
            pl.BlockSpec((tm, tn), lambda i, j: (i, j)),
            pl.BlockSpec((tm, DT_W), lambda i, j: (i, 0)),
        ],
        out_shape=[
            jax.ShapeDtypeStruct((m, P_MAIN), f32),
            jax.ShapeDtypeStruct((m, DT_W), f32),
        ],
        scratch_shapes=[pltpu.VMEM((tm, D), bf16)],
        compiler_params=_cparams(("arbitrary", "arbitrary")),
        name="inproj",
    )(x2d, mod6, mod6, norm_w, w_main, w_dt)


def _split3(q):
    hi = q.astype(bf16)
    r1 = q - hi.astype(f32)
    mid = r1.astype(bf16)
    lo = (r1 - mid.astype(f32)).astype(bf16)
    return hi, mid, lo


def _split2_cat(q):
    hi = q.astype(bf16)
    lo = (q - hi.astype(f32)).astype(bf16)
    return jnp.concatenate([hi, lo], axis=1)


def _conv_silu(buf, w_ref, b_ref):
    w = w_ref[...]
    y = (w[0:1] * buf[7:135, :] + w[1:2] * buf[8:136, :] + w[2:3] * buf[9:137, :]
         + w[3:4] * buf[10:138, :] + b_ref[...])
    return _silu(y)


def _ssd_chunk(ph, first, last, xm_ref, xp_ref, xn_ref, bm_ref, bp_ref, bn_ref, dt_ref,
               cwx_ref, cbx_ref, cwb_ref, cbb_ref, dtb_ref, alog_ref, e2_ref,
               s_scr, xbuf, bcbuf, with_output):
    zero8x = jnp.zeros((SUBLANES, SSD_W), f32)
    zero8b = jnp.zeros((SUBLANES, BC_W), f32)
    xbuf[0:8, :] = jnp.where(first, zero8x, xp_ref[...])
    xbuf[8:136, :] = xm_ref[...]
    xbuf[136:144, :] = jnp.where(last, zero8x, xn_ref[...])
    bcbuf[0:8, :] = jnp.where(first, zero8b, bp_ref[...])
    bcbuf[8:136, :] = bm_ref[...]
    bcbuf[136:144, :] = jnp.where(last, zero8b, bn_ref[...])
    xs = _conv_silu(xbuf, cwx_ref, cbx_ref)
    bcv = _conv_silu(bcbuf, cwb_ref, cbb_ref)

    dtr = dt_ref[...]
    dtr = jnp.where(ph == 0, dtr, pltpu.roll(dtr, LANES - SSD_H, axis=1))
    pre = dtr + dtb_ref[0]
    dt = jnp.maximum(pre, 0.0) + jnp.log(1.0 + jnp.exp(-jnp.abs(pre)))
    d_a = dt * (-jnp.exp(alog_ref[0]))

    row = lax.broadcasted_iota(i32, (CHUNK, CHUNK), 0)
    col = lax.broadcasted_iota(i32, (CHUNK, CHUNK), 1)
    mask = jnp.where(ph == 0, row - col, col - row) >= 0
    tri = mask.astype(f32).astype(bf16)
    hi, mid, lo = _split3(d_a)
    acs = _dot(tri, hi) + _dot(tri, mid) + _dot(tri, lo)
    tot = jnp.sum(d_a, axis=0, keepdims=True)
    e_a = jnp.exp(acs)
    dte = jnp.exp(tot - acs)
    cd = jnp.broadcast_to(jnp.exp(tot), (SUBLANES, LANES))
    ex = _dot(jnp.concatenate([_split2_cat(dt), _split2_cat(e_a), _split2_cat(dte), _split2_cat(cd)], axis=0),
              e2_ref[...])
    dt_e = ex[0:128]
    ea_e = ex[128:256]
    dte_e = ex[256:384]
    cd_e = ex[384:385]
    xdt = xs * dt_e

    y_parts = []
    if with_output:
        acs_t = acs.T
        lane_lo = col < SSD_P
        g_mats = []
        for g in range(SSD_G):
            b_g = bcv[:, g * D_STATE:(g + 1) * D_STATE].astype(bf16)
            c_g = bcv[:, (SSD_G + g) * D_STATE:(SSD_G + g + 1) * D_STATE].astype(bf16)
            g_mats.append(lax.dot_general(c_g, b_g, (((1,), (1,)), ((), ())), preferred_element_type=f32))
        for p in range(SSD_H // 2):
            g = (2 * p) // (SSD_H // SSD_G)
            xp = xdt[:, p * LANES:(p + 1) * LANES]
            acc = None
            for k in range(2):
                h = 2 * p + k
                seg = jnp.broadcast_to(acs[:, h:h + 1], (CHUNK, CHUNK)) - jnp.broadcast_to(acs_t[h:h + 1, :], (CHUNK, CHUNK))
                dec = jnp.where(mask, jnp.exp(jnp.minimum(seg, 0.0)), 0.0)
                m_h = (g_mats[g] * dec).astype(bf16)
                x_h = jnp.where(lane_lo if k == 0 else jnp.logical_not(lane_lo), xp, 0.0).astype(bf16)
                t = _dot(m_h, x_h)
                acc = t if acc is None else acc + t
            y_parts.append(acc)

    gw = SSD_W // SSD_G
    y_off = []
    for g in range(SSD_G):
        sl = slice(g * gw, (g + 1) * gw)
        b_g = bcv[:, g * D_STATE:(g + 1) * D_STATE].astype(bf16)
        s_g = s_scr[g]
        if with_output:
            c_g = bcv[:, (SSD_G + g) * D_STATE:(SSD_G + g + 1) * D_STATE].astype(bf16)
            y_off.append(_dot(c_g, s_g.astype(bf16)) * ea_e[:, sl])
        upd = lax.dot_general(b_g, (xdt[:, sl] * dte_e[:, sl]).astype(bf16), (((0,), (0,)), ((), ())),
                              preferred_element_type=f32)
        s_scr[g] = s_g * cd_e[:, sl] + upd
    if not with_output:
        return xs, None
    y = jnp.concatenate(y_parts, axis=1) + jnp.concatenate(y_off, axis=1)
    return xs, y


def _ctx_state_kernel(xm_ref, xp_ref, xn_ref, bm_ref, bp_ref, bn_ref, dt_ref,
                      cwx_ref, cbx_ref, cwb_ref, cbb_ref, dtb_ref, alog_ref, e2_ref,
                      h_ref, s_scr, xbuf, bcbuf, *, nc):
    ph = pl.program_id(1)
    c = pl.program_id(2)
    ci = jnp.where(ph == 0, c, nc - 1 - c)

    @pl.when(c == 0)
    def _():
        s_scr[...] = jnp.zeros_like(s_scr)

    _ssd_chunk(ph, ci == 0, ci == nc - 1, xm_ref, xp_ref, xn_ref, bm_ref, bp_ref, bn_ref, dt_ref,
               cwx_ref, cbx_ref, cwb_ref, cbb_ref, dtb_ref, alog_ref, e2_ref, s_scr, xbuf, bcbuf, False)

    @pl.when(c == nc - 1)
    def _():
        h_ref[0, 0] = s_scr[...]


def _gelu_tanh(x):
    return 0.5 * x * (1.0 + jnp.tanh(0.7978845608028654 * (x + 0.044715 * x * x * x)))


def _ssd_main_kernel(xm_ref, xp_ref, xn_ref, bm_ref, bp_ref, bn_ref, dt_ref,
                     cwx_ref, cbx_ref, cwb_ref, cbb_ref, dtb_ref, alog_ref, e2_ref,
                     h0_ref, z_ref, u_ref, v_ref, dsk_ref, snw_ref, cnw_ref, ws_ref, bs_ref,
                     o_ref, s_scr, xbuf, bcbuf, yf_scr, *, nc):
    ph = pl.program_id(1)
    c = pl.program_id(2)
    ci = jnp.where(ph == 0, c, nc - 1 - c)

    @pl.when(c == 0)
    def _():
        s_scr[...] = h0_ref[0, 0]

    xs, y = _ssd_chunk(ph, ci == 0, ci == nc - 1, xm_ref, xp_ref, xn_ref, bm_ref, bp_ref, bn_ref, dt_ref,
                       cwx_ref, cbx_ref, cwb_ref, cbb_ref, dtb_ref, alog_ref, e2_ref, s_scr, xbuf, bcbuf, True)
    r0 = pl.multiple_of(ci * CHUNK, CHUNK)

    @pl.when(ph == 0)
    def _():
        yf_scr[pl.ds(r0, CHUNK), :] = y

    @pl.when(ph == 1)
    def _():
        yt = (yf_scr[pl.ds(r0, CHUNK), :] + y + dsk_ref[...] * xs) * _silu(z_ref[...])
        yt = yt * lax.rsqrt(jnp.mean(yt * yt, axis=-1, keepdims=True) + EPS) * snw_ref[...]
        o_ref[:, 0:SSD_W] = yt.astype(bf16)
        u = _gelu_tanh(u_ref[...])
        v = _gelu_tanh(v_ref[...])
        vn = (v * lax.rsqrt(jnp.mean(v * v, axis=-1, keepdims=True) + EPS) * cnw_ref[...]).astype(bf16)
        mixed = [_dot(ws_ref[h], vn[:, h * CM_HD:(h + 1) * CM_HD]) for h in range(CM_H)]
        o_ref[:, SSD_W:SSD_W + CM_W] = (u * (jnp.concatenate(mixed, axis=1) + bs_ref[...])).astype(bf16)


def _ssd_common_specs(nc, row_blocks_total):
    cpb = nc
    sub = CHUNK // SUBLANES

    def ci_of(ph, c):
        return jnp.where(ph == 0, c, nc - 1 - c)

    def main_i(b, ph, c):
        return b * cpb + ci_of(ph, c)

    def prev_i(b, ph, c):
        return jnp.maximum(main_i(b, ph, c) * sub - 1, 0)

    def next_i(b, ph, c):
        return jnp.minimum(main_i(b, ph, c) * sub + sub, row_blocks_total * sub - 1)

    bc_col = (4 * 1024) // BC_W
    specs = [
        pl.BlockSpec((CHUNK, SSD_W), lambda b, ph, c: (main_i(b, ph, c), 0)),
        pl.BlockSpec((SUBLANES, SSD_W), lambda b, ph, c: (prev_i(b, ph, c), 0)),
        pl.BlockSpec((SUBLANES, SSD_W), lambda b, ph, c: (next_i(b, ph, c), 0)),
        pl.BlockSpec((CHUNK, BC_W), lambda b, ph, c: (main_i(b, ph, c), bc_col)),
        pl.BlockSpec((SUBLANES, BC_W), lambda b, ph, c: (prev_i(b, ph, c), bc_col)),
        pl.BlockSpec((SUBLANES, BC_W), lambda b, ph, c: (next_i(b, ph, c), bc_col)),
        pl.BlockSpec((CHUNK, DT_W), lambda b, ph, c: (main_i(b, ph, c), 0)),
        pl.BlockSpec((D_CONV, SSD_W), lambda b, ph, c: (0, 0)),
        pl.BlockSpec((1, SSD_W), lambda b, ph, c: (0, 0)),
        pl.BlockSpec((D_CONV, BC_W), lambda b, ph, c: (0, 0)),
        pl.BlockSpec((1, BC_W), lambda b, ph, c: (0, 0)),
        pl.BlockSpec((1, 1, DT_W), lambda b, ph, c: (ph, 0, 0)),
        pl.BlockSpec((1, 1, DT_W), lambda b, ph, c: (ph, 0, 0)),
        pl.BlockSpec((2 * LANES, SSD_W), lambda b, ph, c: (0, 0)),
    ]
    return specs


def _ssd_scratch():
    return [
        pltpu.VMEM((SSD_G, D_STATE, SSD_W // SSD_G), f32),
        pltpu.VMEM((CHUNK + 2 * SUBLANES, SSD_W), f32),
        pltpu.VMEM((CHUNK + 2 * SUBLANES, BC_W), f32),
    ]


def _ctx_state_call(pm, dtm, ssd_consts):
    nc = CTX // CHUNK
    specs = _ssd_common_specs(nc, NB_BATCH * nc)
    return pl.pallas_call(
        functools.partial(_ctx_state_kernel, nc=nc),
        grid=(NB_BATCH, 2, nc),
        in_specs=specs,
        out_specs=pl.BlockSpec((1, 1, SSD_G, D_STATE, SSD_W // SSD_G), lambda b, ph, c: (b, ph, 0, 0, 0)),
        out_shape=jax.ShapeDtypeStruct((NB_BATCH, 2, SSD_G, D_STATE, SSD_W // SSD_G), f32),
        scratch_shapes=_ssd_scratch(),
        compiler_params=_cparams(("arbitrary", "arbitrary", "arbitrary")),
        name="ssd_ctx",
    )(pm, pm, pm, pm, pm, pm, dtm, *ssd_consts)


def _ssd_main_call(pm, dtm, ssd_consts, h0, dsk, snw, cnw, ws, bs_e):
    nc = SEQ // CHUNK
    specs = _ssd_common_specs(nc, NB_BATCH * nc)

    def zrow(b, ph, c):
        return b * nc + jnp.where(ph == 0, nc - 1, nc - 1 - c)

    specs += [
        pl.BlockSpec((1, 1, SSD_G, D_STATE, SSD_W // SSD_G), lambda b, ph, c: (b, ph, 0, 0, 0)),
        pl.BlockSpec((CHUNK, SSD_W), lambda b, ph, c: (zrow(b, ph, c), 1)),
        pl.BlockSpec((CHUNK, CM_W), lambda b, ph, c: (zrow(b, ph, c), 2)),
        pl.BlockSpec((CHUNK, CM_W), lambda b, ph, c: (zrow(b, ph, c), 3)),
        pl.BlockSpec((1, SSD_W), lambda b, ph, c: (0, 0)),
        pl.BlockSpec((1, SSD_W), lambda b, ph, c: (0, 0)),
        pl.BlockSpec((1, CM_W), lambda b, ph, c: (0, 0)),
        pl.BlockSpec((CM_H, CHUNK, CHUNK), lambda b, ph, c: (0, 0, 0)),
        pl.BlockSpec((CHUNK, CM_W), lambda b, ph, c: (0, 0)),
    ]
    return pl.pallas_call(
        functools.partial(_ssd_main_kernel, nc=nc),
        grid=(NB_BATCH, 2, nc),
        in_specs=specs,
        out_specs=pl.BlockSpec((CHUNK, SSD_W + CM_W), lambda b, ph, c: (zrow(b, ph, c), 0)),
        out_shape=jax.ShapeDtypeStruct((N_TOK, SSD_W + CM_W), bf16),
        scratch_shapes=_ssd_scratch() + [pltpu.VMEM((SEQ, SSD_W), f32)],
        compiler_params=_cparams(("arbitrary", "arbitrary", "arbitrary")),
        name="ssd_main",
    )(pm, pm, pm, pm, pm, pm, dtm, *ssd_consts, h0, pm, pm, pm, dsk, snw, cnw, ws, bs_e)


ROW_CHUNKS = D // LANES


def _store_token_major(ref, val, n_rows, row0=0):
    for c in range(ROW_CHUNKS):
        ref[pl.ds(row0 + c, n_rows, stride=ROW_CHUNKS), :] = val[:, c * LANES:(c + 1) * LANES]


def _load_token_major_chunk(ref, lead, c, n_rows, row0=0):
    idx = pl.ds(row0 + c, n_rows, stride=ROW_CHUNKS)
    return ref[idx, :] if lead is None else ref[lead, idx, :]


def _outproj_kernel(y_ref, x_ref, g_ref, sh_ref, sc_ref, g5_ref, nw_ref, wo_ref, wr_ref, w1_ref, w2_ref,
                    base_ref, h_ref, lg_ref):
    x1 = x_ref[...] + g_ref[0, 0] * _dot(y_ref[...], wo_ref[...])
    y = x1 * lax.rsqrt(jnp.mean(x1 * x1, axis=-1, keepdims=True) + EPS) * nw_ref[...]
    h = y * (1.0 + sc_ref[0, 0]) + sh_ref[0, 0]
    _store_token_major(h_ref, h, h.shape[0])
    hb = h.astype(bf16)
    lg_ref[...] = lax.dot_general(wr_ref[...], hb, (((1,), (1,)), ((), ())), preferred_element_type=f32)
    gu = _dot(hb, w1_ref[...])
    act = (_silu(gu[:, :D_EXP]) * gu[:, D_EXP:]).astype(bf16)
    base_ref[...] = x1 + g5_ref[0, 0] * _dot(act, w2_ref[...])


def _outproj_call(ycat, x2d, mod6, norm_w, w_out, w_router_t, w1, w2):
    tm = TM_OUT
    rb = SEQ // tm
    const = lambda i: (0, 0)
    return pl.pallas_call(
        _outproj_kernel,
        grid=(N_TOK // tm,),
        in_specs=[
            pl.BlockSpec((tm, D), lambda i: (i, 0)),
            pl.BlockSpec((tm, D), lambda i: (i, 0)),
            pl.BlockSpec((1, 1, 1, D), lambda i: (i // rb, 2, 0, 0)),
            pl.BlockSpec((1, 1, 1, D), lambda i: (i // rb, 3, 0, 0)),
            pl.BlockSpec((1, 1, 1, D), lambda i: (i // rb, 4, 0, 0)),
            pl.BlockSpec((1, 1, 1, D), lambda i: (i // rb, 5, 0, 0)),
            pl.BlockSpec((1, D), const),
            pl.BlockSpec((D, D), const),
            pl.BlockSpec((N_EXP, D), const),
            pl.BlockSpec((D, 2 * D_EXP), const),
            pl.BlockSpec((D_EXP, D), const),
        ],
        out_specs=[
            pl.BlockSpec((tm, D), lambda i: (i, 0)),
            pl.BlockSpec((tm * ROW_CHUNKS, LANES), lambda i: (i, 0)),
            pl.BlockSpec((N_EXP, tm), lambda i: (0, i)),
        ],
        out_shape=[
            jax.ShapeDtypeStruct((N_TOK, D), f32),
            jax.ShapeDtypeStruct((N_TOK * ROW_CHUNKS, LANES), f32),
            jax.ShapeDtypeStruct((N_EXP, N_TOK), f32),
        ],
        compiler_params=_cparams(("arbitrary",)),
        name="outproj",
    )(ycat, x2d, mod6, mod6, mod6, mod6, norm_w, w_out, w_router_t, w1, w2)


def _route_kernel(lg_ref, bias_ref, dest_ref, gate_ref, bexp_ref, bend_ref, nval_ref,
                  idx_scr, pos_scr, cnt_scr, *, n_tiles):
    j = pl.program_id(0)
    tt = TT_ROUTE
    neg = -jnp.inf

    @pl.when(j == 0)
    def _():
        cnt_scr[...] = jnp.zeros_like(cnt_scr)

    s = _sigmoid(lg_ref[...])
    biased = s + bias_ref[...]
    v3 = biased.reshape(N_GRP, GRP_SZ, tt)
    io3 = lax.broadcasted_iota(i32, (N_GRP, GRP_SZ, tt), 1)
    m1 = jnp.max(v3, axis=1, keepdims=True)
    i1 = jnp.min(jnp.where(v3 == m1, io3, GRP_SZ), axis=1, keepdims=True)
    m2 = jnp.max(jnp.where(io3 == i1, neg, v3), axis=1, keepdims=True)
    gs = m1 + m2
    gio = lax.broadcasted_iota(i32, (N_GRP, 1, tt), 0)
    sel = jnp.zeros((N_GRP, 1, tt), dtype=jnp.bool_)
    cur_g = gs
    for _ in range(TOP_GRP):
        m = jnp.max(cur_g, axis=0, keepdims=True)
        gi = jnp.min(jnp.where(cur_g == m, gio, N_GRP), axis=0, keepdims=True)
        hit = gio == gi
        sel = jnp.logical_or(sel, hit)
        cur_g = jnp.where(hit, neg, cur_g)
    cur = jnp.where(sel, v3, neg).reshape(N_EXP, tt)

    eio = lax.broadcasted_iota(i32, (N_EXP, tt), 0)
    onehot = jnp.zeros((N_EXP, tt), f32)
    idxs, gts = [], []
    for _ in range(TOP_K):
        m = jnp.max(cur, axis=0, keepdims=True)
        ei = jnp.min(jnp.where(cur == m, eio, N_EXP), axis=0, keepdims=True)
        hit = eio == ei
        gts.append(jnp.sum(jnp.where(hit, s, 0.0), axis=0, keepdims=True))
        idxs.append(ei)
        onehot = jnp.where(hit, 1.0, onehot)
        cur = jnp.where(hit, neg, cur)
    gsum = gts[0]
    for k in range(1, TOP_K):
        gsum = gsum + gts[k]

    r_io = lax.broadcasted_iota(i32, (tt, tt), 0)
    c_io = lax.broadcasted_iota(i32, (tt, tt), 1)
    upper = (r_io <= c_io).astype(f32).astype(bf16)
    oh = onehot.astype(bf16)
    incl = _dot(oh, upper)
    base = cnt_scr[...]
    pos = incl - 1.0 + jnp.concatenate([base] * (tt // LANES), axis=1)
    cnt_scr[...] = base + _dot(oh, jnp.ones((tt, LANES), bf16))
    c0 = pl.multiple_of(j * tt, tt)
    for k in range(TOP_K):
        hit = eio == idxs[k]
        idx_scr[k:k + 1, pl.ds(c0, tt)] = idxs[k]
        pos_scr[k:k + 1, pl.ds(c0, tt)] = jnp.sum(jnp.where(hit, pos, 0.0), axis=0, keepdims=True)
        gate_ref[k:k + 1, pl.ds(c0, tt)] = gts[k] / gsum * ROUTED_SCALE

    @pl.when(j == n_tiles - 1)
    def _():
        cnt = cnt_scr[...]
        r = float(ROWS_PER_BLOCK)
        nblk = jnp.floor((cnt + (r - 1.0)) * (1.0 / r))
        er = lax.broadcasted_iota(i32, (N_EXP, N_EXP), 0)
        ec = lax.broadcasted_iota(i32, (N_EXP, N_EXP), 1)
        lower = (er >= ec).astype(f32).astype(bf16)
        bend = _dot(lower, nblk.astype(bf16))
        pstart = (bend - nblk) * r
        pstart_t = jnp.concatenate([pstart] * (tt // LANES), axis=1)
        def slots(t, carry):
            c1 = pl.multiple_of(t * tt, tt)
            for k in range(TOP_K):
                hit = eio == idx_scr[k:k + 1, pl.ds(c1, tt)]
                d = jnp.sum(jnp.where(hit, pstart_t, 0.0), axis=0, keepdims=True) + pos_scr[k:k + 1, pl.ds(c1, tt)]
                dest_ref[k:k + 1, pl.ds(c1, tt)] = d.astype(i32)
            return carry

        lax.fori_loop(0, n_tiles, slots, 0)
        bio = lax.broadcasted_iota(i32, (N_EXP, N_BLOCKS), 1).astype(f32)
        bend_t = jnp.concatenate([bend] * (N_BLOCKS // LANES), axis=1)
        be = jnp.sum(jnp.where(bend_t <= bio, 1.0, 0.0), axis=0, keepdims=True)
        bexp_ref[...] = jnp.minimum(be, float(N_EXP - 1)).astype(i32)
        bend_ref[...] = bend.astype(i32)
        cnt_t = jnp.concatenate([cnt] * (N_BLOCKS // LANES), axis=1)
        pst_b = jnp.concatenate([pstart] * (N_BLOCKS // LANES), axis=1)
        lo = jnp.maximum(pst_b, bio * r)
        hi = jnp.minimum(pst_b + cnt_t, (bio + 1.0) * r)
        nval_ref[...] = jnp.sum(jnp.maximum(hi - lo, 0.0), axis=0, keepdims=True).astype(i32)


def _route_call(logits_t, bias_col):
    tt = TT_ROUTE
    n_tiles = N_TOK // tt
    return pl.pallas_call(
        functools.partial(_route_kernel, n_tiles=n_tiles),
        grid=(n_tiles,),
        in_specs=[
            pl.BlockSpec((N_EXP, tt), lambda j: (0, j)),
            pl.BlockSpec((N_EXP, 1), lambda j: (0, 0)),
        ],
        out_specs=[
            pl.BlockSpec((TOP_K, N_TOK), lambda j: (0, 0)),
            pl.BlockSpec((TOP_K, N_TOK), lambda j: (0, 0)),
            pl.BlockSpec((1, N_BLOCKS), lambda j: (0, 0)),
            pl.BlockSpec((N_EXP, LANES), lambda j: (0, 0)),
            pl.BlockSpec((1, N_BLOCKS), lambda j: (0, 0)),
        ],
        out_shape=[
            jax.ShapeDtypeStruct((TOP_K, N_TOK), i32),
            jax.ShapeDtypeStruct((TOP_K, N_TOK), f32),
            jax.ShapeDtypeStruct((1, N_BLOCKS), i32),
            jax.ShapeDtypeStruct((N_EXP, LANES), i32),
            jax.ShapeDtypeStruct((1, N_BLOCKS), i32),
        ],
        scratch_shapes=[
            pltpu.VMEM((TOP_K, N_TOK), i32),
            pltpu.VMEM((TOP_K, N_TOK), f32),
            pltpu.VMEM((N_EXP, LANES), f32),
        ],
        compiler_params=_cparams(("arbitrary",)),
        name="route",
    )(logits_t, bias_col)


def _slots_kernel(dest_ref, tab_ref):
    i = pl.program_id(0)
    tt = TT_COMB

    @pl.when(i == 0)
    def _():
        def zero_row(b, c):
            b0 = b * ROWS_PER_BLOCK
            for l in range(ROWS_PER_BLOCK):
                tab_ref[b0 + l] = 0
            return c
        lax.fori_loop(0, N_BLOCKS, zero_row, 0)

    base = i * tt
    for rr in range(tt):
        for k in range(TOP_K):
            tab_ref[dest_ref[0, 0, k * tt + rr]] = base + rr


def _slots_call(dest3):
    tt = TT_COMB
    return pl.pallas_call(
        _slots_kernel,
        grid=(N_TOK // tt,),
        in_specs=[pl.BlockSpec((1, 1, TOP_K * tt), lambda i: (i, 0, 0), memory_space=pltpu.SMEM)],
        out_specs=pl.BlockSpec(memory_space=pltpu.SMEM),
        out_shape=jax.ShapeDtypeStruct((N_BLOCKS * ROWS_PER_BLOCK,), i32),
        compiler_params=_cparams(("arbitrary",)),
        name="slots",
    )(dest3)


GATHER_GROUP = 16


def _gather_rows(idx_ref, src_hbm, dst, sem, n_rows, n_valid=None):
    def issue(lo, hi):
        for r in range(lo, hi):
            t = idx_ref[0, 0, r]
            pltpu.make_async_copy(src_hbm.at[t], dst.at[pl.ds(r * ROW_CHUNKS, ROW_CHUNKS), :], sem).start(
                priority=r % 2)

    if n_valid is None:
        issue(0, n_rows)
        return
    for lo in range(0, n_rows, GATHER_GROUP):
        pl.when(lo < n_valid)(functools.partial(issue, lo, lo + GATHER_GROUP))


def _gather_rows_wait(src_hbm, dst, sem, n_rows, n_valid=None):
    def body(r, carry):
        pltpu.make_async_copy(src_hbm.at[0], dst.at[pl.ds(0, ROW_CHUNKS), :], sem).wait()
        return carry
    if n_valid is not None:
        n_rows = ((n_valid + (GATHER_GROUP - 1)) // GATHER_GROUP) * GATHER_GROUP
    lax.fori_loop(0, n_rows, body, 0, unroll=8 if n_valid is None else 1)


W_CHUNKS = 4


def _experts_kernel(bexp_ref, bend_ref, nval_ref, nact_ref, tok0_ref, tokn_ref, h_hbm, wgu_hbm, wd_hbm, o_ref,
                    xbuf, xsem, wgu_buf, wd_buf, wsem, ord_scr):
    g = pl.program_id(0)
    na = nact_ref[0]
    e = bexp_ref[g]
    r = ROWS_PER_BLOCK
    xslot = g % 2

    def fetch(ex, slot):
        cps = []
        for q in range(W_CHUNKS):
            a = D // W_CHUNKS
            cps.append(pltpu.make_async_copy(wgu_hbm.at[ex, pl.ds(q * a, a), :],
                                             wgu_buf.at[slot, pl.ds(q * a, a), :], wsem.at[slot]))
            a = D_EXP // W_CHUNKS
            cps.append(pltpu.make_async_copy(wd_hbm.at[ex, pl.ds(q * a, a), :],
                                             wd_buf.at[slot, pl.ds(q * a, a), :], wsem.at[slot]))
        return cps

    def start_fetch(ex, slot):
        for cp in fetch(ex, slot):
            cp.start(priority=1)

    @pl.when(g == 0)
    def _():
        ord_scr[0] = 0
        start_fetch(e, 0)
        xbuf[...] = jnp.zeros_like(xbuf)
        _gather_rows(tok0_ref, h_hbm, xbuf.at[0], xsem.at[0], r, n_valid=nval_ref[0])

    @pl.when(g + 1 < na)
    def _():
        _gather_rows(tokn_ref, h_hbm, xbuf.at[1 - xslot], xsem.at[1 - xslot], r,
                     n_valid=nval_ref[jnp.minimum(g + 1, N_BLOCKS - 1)])

    first = jnp.logical_or(g == 0, e != bexp_ref[jnp.maximum(g - 1, 0)])

    @pl.when(jnp.logical_and(g < na, first))
    def _():
        k = ord_scr[0]
        slot = k % 2
        for cp in fetch(e, slot):
            cp.wait()
        nxt = bend_ref[e]

        @pl.when(nxt < na)
        def _():
            start_fetch(bexp_ref[jnp.minimum(nxt, N_BLOCKS - 1)], 1 - slot)

        ord_scr[0] = k + 1

    @pl.when(g < na)
    def _():
        _gather_rows_wait(h_hbm, xbuf.at[xslot], xsem.at[xslot], r, n_valid=nval_ref[g])
        ws = (ord_scr[0] + 1) % 2
        dn = (((1,), (0,)), ((), ()))
        gu = None
        for p in range(ROW_CHUNKS // 2):
            xp = jnp.concatenate([_load_token_major_chunk(xbuf, xslot, 2 * p, r),
                                  _load_token_major_chunk(xbuf, xslot, 2 * p + 1, r)], axis=1).astype(bf16)
            t = lax.dot_general(xp, wgu_buf[ws, 2 * p * LANES:(2 * p + 2) * LANES, :], dn, preferred_element_type=f32)
            gu = t if gu is None else gu + t
        act = (_silu(gu[:, :D_EXP]) * gu[:, D_EXP:]).astype(bf16)
        _store_token_major(o_ref, lax.dot_general(act, wd_buf[ws], dn, preferred_element_type=f32), r)

    @pl.when(g >= na)
    def _():
        o_ref[...] = jnp.zeros_like(o_ref)


def _experts_call(bexp, bend, nval, nact, slot_tok3, h2, w_gate_up, w_down):
    r = ROWS_PER_BLOCK
    nb = N_BLOCKS
    grid_spec = pltpu.PrefetchScalarGridSpec(
        num_scalar_prefetch=4,
        grid=(nb,),
        in_specs=[
            pl.BlockSpec((1, 1, r), lambda g, *_: (0, 0, 0), memory_space=pltpu.SMEM),
            pl.BlockSpec((1, 1, r), lambda g, *_: (jnp.minimum(g + 1, nb - 1), 0, 0), memory_space=pltpu.SMEM),
            pl.BlockSpec(memory_space=pl.ANY),
            pl.BlockSpec(memory_space=pl.ANY),
            pl.BlockSpec(memory_space=pl.ANY),
        ],
        out_specs=pl.BlockSpec((r * ROW_CHUNKS, LANES), lambda g, *_: (g, 0)),
        scratch_shapes=[
            pltpu.VMEM((2, r * ROW_CHUNKS, LANES), f32),
            pltpu.SemaphoreType.DMA((2,)),
            pltpu.VMEM((2, D, 2 * D_EXP), f32),
            pltpu.VMEM((2, D_EXP, D), f32),
            pltpu.SemaphoreType.DMA((2,)),
            pltpu.SMEM((1,), i32),
        ],
    )
    return pl.pallas_call(
        _experts_kernel,
        grid_spec=grid_spec,
        out_shape=jax.ShapeDtypeStruct((nb * r * ROW_CHUNKS, LANES), f32),
        compiler_params=_cparams(("arbitrary",)),
        name="experts",
    )(bexp, bend, nval, nact, slot_tok3, slot_tok3, h2, w_gate_up, w_down)


def _combine_kernel(d0_ref, dn_ref, ys_hbm, gate_ref, base_ref, g5_ref, nw_ref, o_ref, buf, sem, *, n_tiles):
    i = pl.program_id(0)
    tt = TT_COMB
    n_rows = TOP_K * tt
    slot = i % 2

    @pl.when(i == 0)
    def _():
        _gather_rows(d0_ref, ys_hbm, buf.at[0], sem.at[0], n_rows)

    @pl.when(i + 1 < n_tiles)
    def _():
        _gather_rows(dn_ref, ys_hbm, buf.at[1 - slot], sem.at[1 - slot], n_rows)

    _gather_rows_wait(ys_hbm, buf.at[slot], sem.at[slot], n_rows)
    g = gate_ref[...]
    gcols = [jnp.broadcast_to(g[:, k:k + 1], (tt, LANES)) for k in range(TOP_K)]
    parts = []
    for c in range(ROW_CHUNKS):
        a = gcols[0] * _load_token_major_chunk(buf, slot, c, tt)
        for k in range(1, TOP_K):
            a = a + gcols[k] * _load_token_major_chunk(buf, slot, c, tt, row0=k * tt * ROW_CHUNKS)
        parts.append(a)
    x2 = base_ref[...] + g5_ref[0, 0] * jnp.concatenate(parts, axis=1)
    o_ref[...] = x2 * lax.rsqrt(jnp.mean(x2 * x2, axis=-1, keepdims=True) + EPS) * nw_ref[...]


def _combine_call(dest3, ys, gates_tk, base, mod6, final_w):
    tt = TT_COMB
    n_tiles = N_TOK // tt
    rb = SEQ // tt
    return pl.pallas_call(
        functools.partial(_combine_kernel, n_tiles=n_tiles),
        grid=(n_tiles,),
        in_specs=[
            pl.BlockSpec((1, 1, TOP_K * tt), lambda i: (0, 0, 0), memory_space=pltpu.SMEM),
            pl.BlockSpec((1, 1, TOP_K * tt), lambda i: (jnp.minimum(i + 1, n_tiles - 1), 0, 0),
                         memory_space=pltpu.SMEM),
            pl.BlockSpec(memory_space=pl.ANY),
            pl.BlockSpec((tt, TOP_K), lambda i: (i, 0)),
            pl.BlockSpec((tt, D), lambda i: (i, 0)),
            pl.BlockSpec((1, 1, 1, D), lambda i: (i // rb, 5, 0, 0)),
            pl.BlockSpec((1, D), lambda i: (0, 0)),
        ],
        out_specs=pl.BlockSpec((tt, D), lambda i: (i, 0)),
        out_shape=jax.ShapeDtypeStruct((N_TOK, D), f32),
        scratch_shapes=[
            pltpu.VMEM((2, TOP_K * tt * ROW_CHUNKS, LANES), f32),
            pltpu.SemaphoreType.DMA((2,)),
        ],
        compiler_params=_cparams(("arbitrary",)),
        name="combine",
    )(dest3, dest3, ys, gates_tk, base, mod6, final_w)


def _tile_major(a, tt):
    n = a.shape[1] // tt
    return a.reshape(TOP_K, n, tt).transpose(1, 0, 2).reshape(n, 1, TOP_K * tt)


def kernel(x, c, ctx, c_ctx, w_mod, b_mod, norm1_w, w_in, conv_w, conv_b, dt_bias_f, dt_bias_b, a_log_f, a_log_b, d_skip, ssd_norm_w, cmlp_norm_w, w_spatial, b_spatial, w_out, norm2_w, w_router, router_bias, w_gate_up, w_down, w_shared_gate_up, w_shared_down, final_norm_w):
    assert x.shape == (NB_BATCH, SEQ, D) and ctx.shape == (NB_BATCH, CTX, D) and w_mod.shape[0] == 1

    cs = jnp.concatenate([c, c_ctx[None, :], jnp.zeros((SUBLANES - NB_BATCH - 1, D), f32)], axis=0)
    mod6 = _mod_call(cs, w_mod[0], b_mod).reshape(SUBLANES, N_MOD, 1, D)

    wi = w_in[0]
    w_main = jnp.concatenate([wi[:, OFF_Z:OFF_Z + SSD_W], wi[:, 0:OFF_Z], wi[:, OFF_DTB:OFF_U], wi[:, OFF_U:],
                              wi[:, OFF_Z + SSD_W:OFF_XBC]], axis=1).astype(bf16)
    w_dt = jnp.pad(wi[:, OFF_XBC:OFF_DTB], ((0, 0), (0, DT_W - 2 * SSD_H))).astype(bf16)
    pad_h = DT_W - SSD_H
    dtb = jnp.stack([jnp.pad(dt_bias_f[0], (0, pad_h)), jnp.pad(dt_bias_b[0], (0, pad_h))])[:, None, :]
    alog = jnp.stack([jnp.pad(a_log_f[0], (0, pad_h)), jnp.pad(a_log_b[0], (0, pad_h))])[:, None, :]
    e_rows = lax.broadcasted_iota(i32, (2 * LANES, SSD_W), 0) % LANES
    e_cols = lax.broadcasted_iota(i32, (2 * LANES, SSD_W), 1) // SSD_P
    e2 = (e_rows == e_cols).astype(bf16)
    ssd_consts = (conv_w[0][:, :SSD_W], conv_b[0][None, :SSD_W], conv_w[0][:, SSD_W:], conv_b[0][None, SSD_W:],
                  dtb, alog, e2)

    n1 = norm1_w[0][None, :]
    pm_c, dt_c = _inproj_call(ctx.reshape(NB_BATCH * CTX, D), mod6, n1, w_main, w_dt, lambda i: 2)
    x2d = x.reshape(N_TOK, D)
    pm_l, dt_l = _inproj_call(x2d, mod6, n1, w_main, w_dt, lambda i: i // (SEQ // TM_INPROJ))

    h0 = _ctx_state_call(pm_c, dt_c, ssd_consts)
    dsk = jnp.repeat(d_skip[0], SSD_P)[None, :]
    bs_e = jnp.repeat(jnp.swapaxes(b_spatial[0], 0, 1), CM_HD, axis=1)
    ycat = _ssd_main_call(pm_l, dt_l, ssd_consts, h0, dsk, ssd_norm_w[0][None, :], cmlp_norm_w[0][None, :],
                          w_spatial[0].astype(bf16), bs_e)

    base, h2, logits_t = _outproj_call(ycat, x2d, mod6, norm2_w[0][None, :], w_out[0].astype(bf16),
                                       jnp.swapaxes(w_router[0], 0, 1).astype(bf16),
                                       w_shared_gate_up[0].astype(bf16), w_shared_down[0].astype(bf16))
    h2 = h2.reshape(N_TOK, ROW_CHUNKS, LANES)

    dest, gates, bexp, bend, nval = _route_call(logits_t, router_bias[0][:, None])
    bend = bend[:, 0]
    nact = bend[N_EXP - 1:]
    dest3 = _tile_major(dest, TT_COMB)
    slot_tok = _slots_call(dest3)
    ys = _experts_call(bexp.reshape(N_BLOCKS), bend, nval.reshape(N_BLOCKS), nact,
                       slot_tok.reshape(N_BLOCKS, 1, ROWS_PER_BLOCK),
                       h2, w_gate_up[0], w_down[0]).reshape(N_BLOCKS * ROWS_PER_BLOCK, ROW_CHUNKS, LANES)
    out = _combine_call(dest3, ys, gates.T, base, mod6, final_norm_w[None, :])
    return out.reshape(NB_BATCH, SEQ, D)
```

```python
import functools

import jax
import jax.numpy as jnp
from jax import lax
from jax.experimental import pallas as pl
from jax.experimental.pallas import tpu as pltpu

f32 = jnp.float32
bf16 = jnp.bfloat16
i32 = jnp.int32

D = 2048
NB_BATCH = 2
SEQ = 4096
CTX = 256
N_MOD = 6
SSD_W = 1024
SSD_H = 16
SSD_P = 64
SSD_G = 2
D_STATE = 128
CHUNK = 128
D_CONV = 4
BC_W = 2 * SSD_G * D_STATE
CM_W = 1024
CM_H = 8
CM_HD = 128
OFF_Z = SSD_W
OFF_XBC = OFF_Z + SSD_W + BC_W
OFF_DTF = OFF_XBC + SSD_H
OFF_DTB = OFF_DTF + SSD_H
OFF_U = OFF_DTB + CM_W
IN_PROJ_DIM = OFF_U + CM_W
N_EXP = 256
TOP_K = 8
N_GRP = 8
TOP_GRP = 4
GRP_SZ = N_EXP // N_GRP
D_EXP = 512
ROUTED_SCALE = 2.5
EPS = 1e-6

P_MAIN = 4 * 1024 + BC_W
DT_W = 128

LANES = 128
SUBLANES = 8
VMEM_LIMIT = 56 * 1024 * 1024

TM_INPROJ = 1024
TN_INPROJ = 512
TM_OUT = 256
TM_SHARED = 512
TT_ROUTE = 512
ROWS_PER_BLOCK = 128
N_TOK = NB_BATCH * SEQ
N_BLOCKS = N_TOK * TOP_K // ROWS_PER_BLOCK + N_EXP
TT_COMB = 64


def _cparams(sem):
    return pltpu.CompilerParams(dimension_semantics=sem, vmem_limit_bytes=VMEM_LIMIT)


def _sigmoid(x):
    return 0.5 * jnp.tanh(0.5 * x) + 0.5


def _silu(x):
    return x * _sigmoid(x)


def _dot(a, b):
    return jnp.dot(a, b, preferred_element_type=f32)


def _mod_kernel(c_ref, w_ref, b_ref, o_ref):
    a = _silu(c_ref[...]).astype(bf16)
    o_ref[...] = _dot(a, w_ref[...].astype(bf16)) + b_ref[...]


def _mod_call(cs, w_mod, b_mod):
    tn = 1024
    n = w_mod.shape[1]
    return pl.pallas_call(
        _mod_kernel,
        grid=(n // tn,),
        in_specs=[
            pl.BlockSpec((SUBLANES, D), lambda j: (0, 0)),
            pl.BlockSpec((D, tn), lambda j: (0, j)),
            pl.BlockSpec((1, tn), lambda j: (0, j)),
        ],
        out_specs=pl.BlockSpec((SUBLANES, tn), lambda j: (0, j)),
        out_shape=jax.ShapeDtypeStruct((SUBLANES, n), f32),
        compiler_params=_cparams(("arbitrary",)),
        name="mod",
    )(cs, w_mod, b_mod)


def _inproj_kernel(x_ref, sh_ref, sc_ref, nw_ref, w_ref, wdt_ref, p_ref, dt_ref, h_scr):
    @pl.when(pl.program_id(1) == 0)
    def _():
        x = x_ref[...]
        y = x * lax.rsqrt(jnp.mean(x * x, axis=-1, keepdims=True) + EPS) * nw_ref[...]
        h = (y * (1.0 + sc_ref[0, 0]) + sh_ref[0, 0]).astype(bf16)
        h_scr[...] = h
        dt_ref[...] = _dot(h, wdt_ref[...])

    p_ref[...] = _dot(h_scr[...], w_ref[...])


def _inproj_call(x2d, mod6, norm_w, w_main, w_dt, mod_row_fn):
    m = x2d.shape[0]
    tm, tn = min(TM_INPROJ, m), TN_INPROJ
    return pl.pallas_call(
        _inproj_kernel,
        grid=(m // tm, P_MAIN // tn),
        in_specs=[
            pl.BlockSpec((tm, D), lambda i, j: (i, 0)),
            pl.BlockSpec((1, 1, 1, D), lambda i, j: (mod_row_fn(i), 0, 0, 0)),
            pl.BlockSpec((1, 1, 1, D), lambda i, j: (mod_row_fn(i), 1, 0, 0)),
            pl.BlockSpec((1, D), lambda i, j: (0, 0)),
            pl.BlockSpec((D, tn), lambda i, j: (0, j)),
            pl.BlockSpec((D, DT_W), lambda i, j: (0, 0)),
        ],
        out_specs=[
            pl.BlockSpec((tm, tn), lambda i, j: (i, j)),
            pl.BlockSpec((tm, DT_W), lambda i, j: (i, 0)),
        ],
        out_shape=[
            jax.ShapeDtypeStruct((m, P_MAIN), f32),
            jax.ShapeDtypeStruct((m, DT_W), f32),
        ],
        scratch_shapes=[pltpu.VMEM((tm, D), bf16)],
        compiler_params=_cparams(("arbitrary", "arbitrary")),
        name="inproj",
    )(x2d, mod6, mod6, norm_w, w_main, w_dt)


def _split3(q):
    hi = q.astype(bf16)
    r1 = q - hi.astype(f32)
    mid = r1.astype(bf16)
    lo = (r1 - mid.astype(f32)).astype(bf16)
    return hi, mid, lo


def _split2_cat(q):
    hi = q.astype(bf16)
    lo = (q - hi.astype(f32)).astype(bf16)
    return jnp.concatenate([hi, lo], axis=1)


def _conv_silu(buf, w_ref, b_ref):
    w = w_ref[...]
    y = (w[0:1] * buf[7:135, :] + w[1:2] * buf[8:136, :] + w[2:3] * buf[9:137, :]
         + w[3:4] * buf[10:138, :] + b_ref[...])
    return _silu(y)


def _ssd_chunk(ph, first, last, xm_ref, xp_ref, xn_ref, bm_ref, bp_ref, bn_ref, dt_ref,
               cwx_ref, cbx_ref, cwb_ref, cbb_ref, dtb_ref, alog_ref, e2_ref,
               s_scr, xbuf, bcbuf, with_output):
    zero8x = jnp.zeros((SUBLANES, SSD_W), f32)
    zero8b = jnp.zeros((SUBLANES, BC_W), f32)
    xbuf[0:8, :] = jnp.where(first, zero8x, xp_ref[...])
    xbuf[8:136, :] = xm_ref[...]
    xbuf[136:144, :] = jnp.where(last, zero8x, xn_ref[...])
    bcbuf[0:8, :] = jnp.where(first, zero8b, bp_ref[...])
    bcbuf[8:136, :] = bm_ref[...]
    bcbuf[136:144, :] = jnp.where(last, zero8b, bn_ref[...])
    xs = _conv_silu(xbuf, cwx_ref, cbx_ref)
    bcv = _conv_silu(bcbuf, cwb_ref, cbb_ref)

    dtr = dt_ref[...]
    dtr = jnp.where(ph == 0, dtr, pltpu.roll(dtr, LANES - SSD_H, axis=1))
    pre = dtr + dtb_ref[0]
    dt = jnp.maximum(pre, 0.0) + jnp.log(1.0 + jnp.exp(-jnp.abs(pre)))
    d_a = dt * (-jnp.exp(alog_ref[0]))

    row = lax.broadcasted_iota(i32, (CHUNK, CHUNK), 0)
    col = lax.broadcasted_iota(i32, (CHUNK, CHUNK), 1)
    mask = jnp.where(ph == 0, row - col, col - row) >= 0
    tri = mask.astype(f32).astype(bf16)
    hi, mid, lo = _split3(d_a)
    acs = _dot(tri, hi) + _dot(tri, mid) + _dot(tri, lo)
    tot = jnp.sum(d_a, axis=0, keepdims=True)
    e_a = jnp.exp(acs)
    dte = jnp.exp(tot - acs)
    cd = jnp.broadcast_to(jnp.exp(tot), (SUBLANES, LANES))
    ex = _dot(jnp.concatenate([_split2_cat(dt), _split2_cat(e_a), _split2_cat(dte), _split2_cat(cd)], axis=0),
              e2_ref[...])
    dt_e = ex[0:128]
    ea_e = ex[128:256]
    dte_e = ex[256:384]
    cd_e = ex[384:385]
    xdt = xs * dt_e

    y_parts = []
    if with_output:
        acs_t = acs.T
        lane_lo = col < SSD_P
        g_mats = []
        for g in range(SSD_G):
            b_g = bcv[:, g * D_STATE:(g + 1) * D_STATE].astype(bf16)
            c_g = bcv[:, (SSD_G + g) * D_STATE:(SSD_G + g + 1) * D_STATE].astype(bf16)
            g_mats.append(lax.dot_general(c_g, b_g, (((1,), (1,)), ((), ())), preferred_element_type=f32))
        for p in range(SSD_H // 2):
            g = (2 * p) // (SSD_H // SSD_G)
            xp = xdt[:, p * LANES:(p + 1) * LANES]
            acc = None
            for k in range(2):
                h = 2 * p + k
                seg = jnp.broadcast_to(acs[:, h:h + 1], (CHUNK, CHUNK)) - jnp.broadcast_to(acs_t[h:h + 1, :], (CHUNK, CHUNK))
                dec = jnp.where(mask, jnp.exp(jnp.minimum(seg, 0.0)), 0.0)
                m_h = (g_mats[g] * dec).astype(bf16)
                x_h = jnp.where(lane_lo if k == 0 else jnp.logical_not(lane_lo), xp, 0.0).astype(bf16)
                t = _dot(m_h, x_h)
                acc = t if acc is None else acc + t
            y_parts.append(acc)

    gw = SSD_W // SSD_G
    y_off = []
    for g in range(SSD_G):
        sl = slice(g * gw, (g + 1) * gw)
        b_g = bcv[:, g * D_STATE:(g + 1) * D_STATE].astype(bf16)
        s_g = s_scr[g]
        if with_output:
            c_g = bcv[:, (SSD_G + g) * D_STATE:(SSD_G + g + 1) * D_STATE].astype(bf16)
            y_off.append(_dot(c_g, s_g.astype(bf16)) * ea_e[:, sl])
        upd = lax.dot_general(b_g, (xdt[:, sl] * dte_e[:, sl]).astype(bf16), (((0,), (0,)), ((), ())),
                              preferred_element_type=f32)
        s_scr[g] = s_g * cd_e[:, sl] + upd
    if not with_output:
        return xs, None
    y = jnp.concatenate(y_parts, axis=1) + jnp.concatenate(y_off, axis=1)
    return xs, y


def _ctx_state_kernel(xm_ref, xp_ref, xn_ref, bm_ref, bp_ref, bn_ref, dt_ref,
                      cwx_ref, cbx_ref, cwb_ref, cbb_ref, dtb_ref, alog_ref, e2_ref,
                      h_ref, s_scr, xbuf, bcbuf, *, nc):
    ph = pl.program_id(1)
    c = pl.program_id(2)
    ci = jnp.where(ph == 0, c, nc - 1 - c)

    @pl.when(c == 0)
    def _():
        s_scr[...] = jnp.zeros_like(s_scr)

    _ssd_chunk(ph, ci == 0, ci == nc - 1, xm_ref, xp_ref, xn_ref, bm_ref, bp_ref, bn_ref, dt_ref,
               cwx_ref, cbx_ref, cwb_ref, cbb_ref, dtb_ref, alog_ref, e2_ref, s_scr, xbuf, bcbuf, False)

    @pl.when(c == nc - 1)
    def _():
        h_ref[0, 0] = s_scr[...]


def _gelu_tanh(x):
    return 0.5 * x * (1.0 + jnp.tanh(0.7978845608028654 * (x + 0.044715 * x * x * x)))


def _ssd_main_kernel(xm_ref, xp_ref, xn_ref, bm_ref, bp_ref, bn_ref, dt_ref,
                     cwx_ref, cbx_ref, cwb_ref, cbb_ref, dtb_ref, alog_ref, e2_ref,
                     h0_ref, z_ref, u_ref, v_ref, dsk_ref, snw_ref, cnw_ref, ws_ref, bs_ref,
                     o_ref, s_scr, xbuf, bcbuf, yf_scr, *, nc):
    ph = pl.program_id(1)
    c = pl.program_id(2)
    ci = jnp.where(ph == 0, c, nc - 1 - c)

    @pl.when(c == 0)
    def _():
        s_scr[...] = h0_ref[0, 0]

    xs, y = _ssd_chunk(ph, ci == 0, ci == nc - 1, xm_ref, xp_ref, xn_ref, bm_ref, bp_ref, bn_ref, dt_ref,
                       cwx_ref, cbx_ref, cwb_ref, cbb_ref, dtb_ref, alog_ref, e2_ref, s_scr, xbuf, bcbuf, True)
    r0 = pl.multiple_of(ci * CHUNK, CHUNK)

    @pl.when(ph == 0)
    def _():
        yf_scr[pl.ds(r0, CHUNK), :] = y

    @pl.when(ph == 1)
    def _():
        yt = (yf_scr[pl.ds(r0, CHUNK), :] + y + dsk_ref[...] * xs) * _silu(z_ref[...])
        yt = yt * lax.rsqrt(jnp.mean(yt * yt, axis=-1, keepdims=True) + EPS) * snw_ref[...]
        o_ref[:, 0:SSD_W] = yt.astype(bf16)
        u = _gelu_tanh(u_ref[...])
        v = _gelu_tanh(v_ref[...])
        vn = (v * lax.rsqrt(jnp.mean(v * v, axis=-1, keepdims=True) + EPS) * cnw_ref[...]).astype(bf16)
        mixed = [_dot(ws_ref[h], vn[:, h * CM_HD:(h + 1) * CM_HD]) for h in range(CM_H)]
        o_ref[:, SSD_W:SSD_W + CM_W] = (u * (jnp.concatenate(mixed, axis=1) + bs_ref[...])).astype(bf16)


def _ssd_common_specs(nc, row_blocks_total):
    cpb = nc
    sub = CHUNK // SUBLANES

    def ci_of(ph, c):
        return jnp.where(ph == 0, c, nc - 1 - c)

    def main_i(b, ph, c):
        return b * cpb + ci_of(ph, c)

    def prev_i(b, ph, c):
        return jnp.maximum(main_i(b, ph, c) * sub - 1, 0)

    def next_i(b, ph, c):
        return jnp.minimum(main_i(b, ph, c) * sub + sub, row_blocks_total * sub - 1)

    bc_col = (4 * 1024) // BC_W
    specs = [
        pl.BlockSpec((CHUNK, SSD_W), lambda b, ph, c: (main_i(b, ph, c), 0)),
        pl.BlockSpec((SUBLANES, SSD_W), lambda b, ph, c: (prev_i(b, ph, c), 0)),
        pl.BlockSpec((SUBLANES, SSD_W), lambda b, ph, c: (next_i(b, ph, c), 0)),
        pl.BlockSpec((CHUNK, BC_W), lambda b, ph, c: (main_i(b, ph, c), bc_col)),
        pl.BlockSpec((SUBLANES, BC_W), lambda b, ph, c: (prev_i(b, ph, c), bc_col)),
        pl.BlockSpec((SUBLANES, BC_W), lambda b, ph, c: (next_i(b, ph, c), bc_col)),
        pl.BlockSpec((CHUNK, DT_W), lambda b, ph, c: (main_i(b, ph, c), 0)),
        pl.BlockSpec((D_CONV, SSD_W), lambda b, ph, c: (0, 0)),
        pl.BlockSpec((1, SSD_W), lambda b, ph, c: (0, 0)),
        pl.BlockSpec((D_CONV, BC_W), lambda b, ph, c: (0, 0)),
        pl.BlockSpec((1, BC_W), lambda b, ph, c: (0, 0)),
        pl.BlockSpec((1, 1, DT_W), lambda b, ph, c: (ph, 0, 0)),
        pl.BlockSpec((1, 1, DT_W), lambda b, ph, c: (ph, 0, 0)),
        pl.BlockSpec((2 * LANES, SSD_W), lambda b, ph, c: (0, 0)),
    ]
    return specs


def _ssd_scratch():
    return [
        pltpu.VMEM((SSD_G, D_STATE, SSD_W // SSD_G), f32),
        pltpu.VMEM((CHUNK + 2 * SUBLANES, SSD_W), f32),
        pltpu.VMEM((CHUNK + 2 * SUBLANES, BC_W), f32),
    ]


def _ctx_state_call(pm, dtm, ssd_consts):
    nc = CTX // CHUNK
    specs = _ssd_common_specs(nc, NB_BATCH * nc)
    return pl.pallas_call(
        functools.partial(_ctx_state_kernel, nc=nc),
        grid=(NB_BATCH, 2, nc),
        in_specs=specs,
        out_specs=pl.BlockSpec((1, 1, SSD_G, D_STATE, SSD_W // SSD_G), lambda b, ph, c: (b, ph, 0, 0, 0)),
        out_shape=jax.ShapeDtypeStruct((NB_BATCH, 2, SSD_G, D_STATE, SSD_W // SSD_G), f32),
        scratch_shapes=_ssd_scratch(),
        compiler_params=_cparams(("arbitrary", "arbitrary", "arbitrary")),
        name="ssd_ctx",
    )(pm, pm, pm, pm, pm, pm, dtm, *ssd_consts)


def _ssd_main_call(pm, dtm, ssd_consts, h0, dsk, snw, cnw, ws, bs_e):
    nc = SEQ // CHUNK
    specs = _ssd_common_specs(nc, NB_BATCH * nc)

    def zrow(b, ph, c):
        return b * nc + jnp.where(ph == 0, nc - 1, nc - 1 - c)

    specs += [
        pl.BlockSpec((1, 1, SSD_G, D_STATE, SSD_W // SSD_G), lambda b, ph, c: (b, ph, 0, 0, 0)),
        pl.BlockSpec((CHUNK, SSD_W), lambda b, ph, c: (zrow(b, ph, c), 1)),
        pl.BlockSpec((CHUNK, CM_W), lambda b, ph, c: (zrow(b, ph, c), 2)),
        pl.BlockSpec((CHUNK, CM_W), lambda b, ph, c: (zrow(b, ph, c), 3)),
        pl.BlockSpec((1, SSD_W), lambda b, ph, c: (0, 0)),
        pl.BlockSpec((1, SSD_W), lambda b, ph, c: (0, 0)),
        pl.BlockSpec((1, CM_W), lambda b, ph, c: (0, 0)),
        pl.BlockSpec((CM_H, CHUNK, CHUNK), lambda b, ph, c: (0, 0, 0)),
        pl.BlockSpec((CHUNK, CM_W), lambda b, ph, c: (0, 0)),
    ]
    return pl.pallas_call(
        functools.partial(_ssd_main_kernel, nc=nc),
        grid=(NB_BATCH, 2, nc),
        in_specs=specs,
        out_specs=pl.BlockSpec((CHUNK, SSD_W + CM_W), lambda b, ph, c: (zrow(b, ph, c), 0)),
        out_shape=jax.ShapeDtypeStruct((N_TOK, SSD_W + CM_W), bf16),
        scratch_shapes=_ssd_scratch() + [pltpu.VMEM((SEQ, SSD_W), f32)],
        compiler_params=_cparams(("arbitrary", "arbitrary", "arbitrary")),
        name="ssd_main",
    )(pm, pm, pm, pm, pm, pm, dtm, *ssd_consts, h0, pm, pm, pm, dsk, snw, cnw, ws, bs_e)


ROW_CHUNKS = D // LANES
ROW_DT = f32


def _store_token_major(ref, val, n_rows, row0=0):
    for c in range(ROW_CHUNKS):
        ref[pl.ds(row0 + c, n_rows, stride=ROW_CHUNKS), :] = val[:, c * LANES:(c + 1) * LANES].astype(ref.dtype)


def _load_token_major_chunk(ref, lead, c, n_rows, row0=0):
    idx = pl.ds(row0 + c, n_rows, stride=ROW_CHUNKS)
    return ref[idx, :] if lead is None else ref[lead, idx, :]


def _outproj_kernel(y_ref, x_ref, g_ref, sh_ref, sc_ref, g5_ref, nw_ref, wo_ref, wr_ref, w1_ref, w2_ref,
                    base_ref, h_ref, lg_ref):
    x1 = x_ref[...] + g_ref[0, 0] * _dot(y_ref[...], wo_ref[...])
    y = x1 * lax.rsqrt(jnp.mean(x1 * x1, axis=-1, keepdims=True) + EPS) * nw_ref[...]
    h = y * (1.0 + sc_ref[0, 0]) + sh_ref[0, 0]
    _store_token_major(h_ref, h, h.shape[0])
    hb = h.astype(bf16)
    lg_ref[...] = lax.dot_general(wr_ref[...], hb, (((1,), (1,)), ((), ())), preferred_element_type=f32)
    gu = _dot(hb, w1_ref[...])
    act = (_silu(gu[:, :D_EXP]) * gu[:, D_EXP:]).astype(bf16)
    base_ref[...] = x1 + g5_ref[0, 0] * _dot(act, w2_ref[...])


def _outproj_call(ycat, x2d, mod6, norm_w, w_out, w_router_t, w1, w2):
    tm = TM_OUT
    rb = SEQ // tm
    const = lambda i: (0, 0)
    return pl.pallas_call(
        _outproj_kernel,
        grid=(N_TOK // tm,),
        in_specs=[
            pl.BlockSpec((tm, D), lambda i: (i, 0)),
            pl.BlockSpec((tm, D), lambda i: (i, 0)),
            pl.BlockSpec((1, 1, 1, D), lambda i: (i // rb, 2, 0, 0)),
            pl.BlockSpec((1, 1, 1, D), lambda i: (i // rb, 3, 0, 0)),
            pl.BlockSpec((1, 1, 1, D), lambda i: (i // rb, 4, 0, 0)),
            pl.BlockSpec((1, 1, 1, D), lambda i: (i // rb, 5, 0, 0)),
            pl.BlockSpec((1, D), const),
            pl.BlockSpec((D, D), const),
            pl.BlockSpec((N_EXP, D), const),
            pl.BlockSpec((D, 2 * D_EXP), const),
            pl.BlockSpec((D_EXP, D), const),
        ],
        out_specs=[
            pl.BlockSpec((tm, D), lambda i: (i, 0)),
            pl.BlockSpec((tm * ROW_CHUNKS, LANES), lambda i: (i, 0)),
            pl.BlockSpec((N_EXP, tm), lambda i: (0, i)),
        ],
        out_shape=[
            jax.ShapeDtypeStruct((N_TOK, D), f32),
            jax.ShapeDtypeStruct((N_TOK * ROW_CHUNKS, LANES), ROW_DT),
            jax.ShapeDtypeStruct((N_EXP, N_TOK), f32),
        ],
        compiler_params=_cparams(("arbitrary",)),
        name="outproj",
    )(ycat, x2d, mod6, mod6, mod6, mod6, norm_w, w_out, w_router_t, w1, w2)


def _route_kernel(lg_ref, bias_ref, dest_ref, gate_ref, bexp_ref, bend_ref, nval_ref,
                  idx_scr, pos_scr, cnt_scr, *, n_tiles):
    j = pl.program_id(0)
    tt = TT_ROUTE
    neg = -jnp.inf

    @pl.when(j == 0)
    def _():
        cnt_scr[...] = jnp.zeros_like(cnt_scr)

    s = _sigmoid(lg_ref[...])
    biased = s + bias_ref[...]
    v3 = biased.reshape(N_GRP, GRP_SZ, tt)
    io3 = lax.broadcasted_iota(i32, (N_GRP, GRP_SZ, tt), 1)
    m1 = jnp.max(v3, axis=1, keepdims=True)
    i1 = jnp.min(jnp.where(v3 == m1, io3, GRP_SZ), axis=1, keepdims=True)
    m2 = jnp.max(jnp.where(io3 == i1, neg, v3), axis=1, keepdims=True)
    gs = m1 + m2
    gio = lax.broadcasted_iota(i32, (N_GRP, 1, tt), 0)
    sel = jnp.zeros((N_GRP, 1, tt), dtype=jnp.bool_)
    cur_g = gs
    for _ in range(TOP_GRP):
        m = jnp.max(cur_g, axis=0, keepdims=True)
        gi = jnp.min(jnp.where(cur_g == m, gio, N_GRP), axis=0, keepdims=True)
        hit = gio == gi
        sel = jnp.logical_or(sel, hit)
        cur_g = jnp.where(hit, neg, cur_g)
    cur = jnp.where(sel, v3, neg).reshape(N_EXP, tt)

    eio = lax.broadcasted_iota(i32, (N_EXP, tt), 0)
    onehot = jnp.zeros((N_EXP, tt), f32)
    idxs, gts = [], []
    for _ in range(TOP_K):
        m = jnp.max(cur, axis=0, keepdims=True)
        ei = jnp.min(jnp.where(cur == m, eio, N_EXP), axis=0, keepdims=True)
        hit = eio == ei
        gts.append(jnp.sum(jnp.where(hit, s, 0.0), axis=0, keepdims=True))
        idxs.append(ei)
        onehot = jnp.where(hit, 1.0, onehot)
        cur = jnp.where(hit, neg, cur)
    gsum = gts[0]
    for k in range(1, TOP_K):
        gsum = gsum + gts[k]

    r_io = lax.broadcasted_iota(i32, (tt, tt), 0)
    c_io = lax.broadcasted_iota(i32, (tt, tt), 1)
    upper = (r_io <= c_io).astype(f32).astype(bf16)
    oh = onehot.astype(bf16)
    incl = _dot(oh, upper)
    base = cnt_scr[...]
    pos = incl - 1.0 + jnp.concatenate([base] * (tt // LANES), axis=1)
    cnt_scr[...] = base + _dot(oh, jnp.ones((tt, LANES), bf16))
    c0 = pl.multiple_of(j * tt, tt)
    for k in range(TOP_K):
        hit = eio == idxs[k]
        idx_scr[k:k + 1, pl.ds(c0, tt)] = idxs[k]
        pos_scr[k:k + 1, pl.ds(c0, tt)] = jnp.sum(jnp.where(hit, pos, 0.0), axis=0, keepdims=True)
        gate_ref[k:k + 1, pl.ds(c0, tt)] = gts[k] / gsum * ROUTED_SCALE

    @pl.when(j == n_tiles - 1)
    def _():
        cnt = cnt_scr[...]
        r = float(ROWS_PER_BLOCK)
        nblk = jnp.floor((cnt + (r - 1.0)) * (1.0 / r))
        er = lax.broadcasted_iota(i32, (N_EXP, N_EXP), 0)
        ec = lax.broadcasted_iota(i32, (N_EXP, N_EXP), 1)
        lower = (er >= ec).astype(f32).astype(bf16)
        bend = _dot(lower, nblk.astype(bf16))
        pstart = (bend - nblk) * r
        pstart_t = jnp.concatenate([pstart] * (tt // LANES), axis=1)
        def slots(t, carry):
            c1 = pl.multiple_of(t * tt, tt)
            for k in range(TOP_K):
                hit = eio == idx_scr[k:k + 1, pl.ds(c1, tt)]
                d = jnp.sum(jnp.where(hit, pstart_t, 0.0), axis=0, keepdims=True) + pos_scr[k:k + 1, pl.ds(c1, tt)]
                dest_ref[k:k + 1, pl.ds(c1, tt)] = d.astype(i32)
            return carry

        lax.fori_loop(0, n_tiles, slots, 0)
        bio = lax.broadcasted_iota(i32, (N_EXP, N_BLOCKS), 1).astype(f32)
        bend_t = jnp.concatenate([bend] * (N_BLOCKS // LANES), axis=1)
        be = jnp.sum(jnp.where(bend_t <= bio, 1.0, 0.0), axis=0, keepdims=True)
        bexp_ref[...] = jnp.minimum(be, float(N_EXP - 1)).astype(i32)
        bend_ref[...] = bend.astype(i32)
        cnt_t = jnp.concatenate([cnt] * (N_BLOCKS // LANES), axis=1)
        pst_b = jnp.concatenate([pstart] * (N_BLOCKS // LANES), axis=1)
        lo = jnp.maximum(pst_b, bio * r)
        hi = jnp.minimum(pst_b + cnt_t, (bio + 1.0) * r)
        nval_ref[...] = jnp.sum(jnp.maximum(hi - lo, 0.0), axis=0, keepdims=True).astype(i32)


def _route_call(logits_t, bias_col):
    tt = TT_ROUTE
    n_tiles = N_TOK // tt
    return pl.pallas_call(
        functools.partial(_route_kernel, n_tiles=n_tiles),
        grid=(n_tiles,),
        in_specs=[
            pl.BlockSpec((N_EXP, tt), lambda j: (0, j)),
            pl.BlockSpec((N_EXP, 1), lambda j: (0, 0)),
        ],
        out_specs=[
            pl.BlockSpec((TOP_K, N_TOK), lambda j: (0, 0)),
            pl.BlockSpec((TOP_K, N_TOK), lambda j: (0, 0)),
            pl.BlockSpec((1, N_BLOCKS), lambda j: (0, 0)),
            pl.BlockSpec((N_EXP, LANES), lambda j: (0, 0)),
            pl.BlockSpec((1, N_BLOCKS), lambda j: (0, 0)),
        ],
        out_shape=[
            jax.ShapeDtypeStruct((TOP_K, N_TOK), i32),
            jax.ShapeDtypeStruct((TOP_K, N_TOK), f32),
            jax.ShapeDtypeStruct((1, N_BLOCKS), i32),
            jax.ShapeDtypeStruct((N_EXP, LANES), i32),
            jax.ShapeDtypeStruct((1, N_BLOCKS), i32),
        ],
        scratch_shapes=[
            pltpu.VMEM((TOP_K, N_TOK), i32),
            pltpu.VMEM((TOP_K, N_TOK), f32),
            pltpu.VMEM((N_EXP, LANES), f32),
        ],
        compiler_params=_cparams(("arbitrary",)),
        name="route",
    )(logits_t, bias_col)


def _slots_kernel(dest_ref, tab_ref):
    i = pl.program_id(0)
    tt = TT_COMB

    @pl.when(i == 0)
    def _():
        def zero_row(b, c):
            b0 = b * ROWS_PER_BLOCK
            for l in range(ROWS_PER_BLOCK):
                tab_ref[b0 + l] = 0
            return c
        lax.fori_loop(0, N_BLOCKS, zero_row, 0)

    base = i * tt
    for rr in range(tt):
        for k in range(TOP_K):
            tab_ref[dest_ref[0, 0, k * tt + rr]] = base + rr


def _slots_call(dest3):
    tt = TT_COMB
    return pl.pallas_call(
        _slots_kernel,
        grid=(N_TOK // tt,),
        in_specs=[pl.BlockSpec((1, 1, TOP_K * tt), lambda i: (i, 0, 0), memory_space=pltpu.SMEM)],
        out_specs=pl.BlockSpec(memory_space=pltpu.SMEM),
        out_shape=jax.ShapeDtypeStruct((N_BLOCKS * ROWS_PER_BLOCK,), i32),
        compiler_params=_cparams(("arbitrary",)),
        name="slots",
    )(dest3)


GATHER_GROUP = 16


def _gather_rows(idx_ref, src_hbm, dst, sem, n_rows, n_valid=None):
    def issue(lo, hi):
        for r in range(lo, hi):
            t = idx_ref[0, 0, r]
            pltpu.make_async_copy(src_hbm.at[t], dst.at[pl.ds(r * ROW_CHUNKS, ROW_CHUNKS), :], sem).start(
                priority=r % 2)

    if n_valid is None:
        issue(0, n_rows)
        return
    for lo in range(0, n_rows, GATHER_GROUP):
        pl.when(lo < n_valid)(functools.partial(issue, lo, lo + GATHER_GROUP))


def _gather_rows_wait(src_hbm, dst, sem, n_rows, n_valid=None):
    def body(r, carry):
        pltpu.make_async_copy(src_hbm.at[0], dst.at[pl.ds(0, ROW_CHUNKS), :], sem).wait()
        return carry
    if n_valid is not None:
        n_rows = ((n_valid + (GATHER_GROUP - 1)) // GATHER_GROUP) * GATHER_GROUP
    lax.fori_loop(0, n_rows, body, 0, unroll=8 if n_valid is None else 1)


W_CHUNKS = 4
W_SLOTS = 3


def _experts_kernel(bexp_ref, bend_ref, nval_ref, nact_ref, tok0_ref, tokn_ref, h_hbm, wgu_hbm, wd_hbm, o_ref,
                    xbuf, xsem, wgu_buf, wd_buf, wsem, ord_scr):
    g = pl.program_id(0)
    na = nact_ref[0]
    e = bexp_ref[g]
    r = ROWS_PER_BLOCK
    xslot = g % 2

    def fetch(ex, slot):
        cps = []
        for q in range(W_CHUNKS):
            a = D // W_CHUNKS
            cps.append(pltpu.make_async_copy(wgu_hbm.at[ex, pl.ds(q * a, a), :],
                                             wgu_buf.at[slot, pl.ds(q * a, a), :], wsem.at[slot]))
            a = D_EXP // W_CHUNKS
            cps.append(pltpu.make_async_copy(wd_hbm.at[ex, pl.ds(q * a, a), :],
                                             wd_buf.at[slot, pl.ds(q * a, a), :], wsem.at[slot]))
        return cps

    def start_fetch(ex, slot):
        for n, cp in enumerate(fetch(ex, slot)):
            cp.start(priority=(n // 2) % 2)

    def next_expert_block(ex):
        return bend_ref[ex]

    def expert_of(blk):
        return bexp_ref[jnp.minimum(blk, N_BLOCKS - 1)]

    @pl.when(g == 0)
    def _():
        ord_scr[0] = 0
        start_fetch(e, 0)
        n1 = next_expert_block(e)

        @pl.when(n1 < na)
        def _():
            start_fetch(expert_of(n1), 1)

        xbuf[...] = jnp.zeros_like(xbuf)
        _gather_rows(tok0_ref, h_hbm, xbuf.at[0], xsem.at[0], r, n_valid=nval_ref[0])

    @pl.when(g + 1 < na)
    def _():
        _gather_rows(tokn_ref, h_hbm, xbuf.at[1 - xslot], xsem.at[1 - xslot], r,
                     n_valid=nval_ref[jnp.minimum(g + 1, N_BLOCKS - 1)])

    first = jnp.logical_or(g == 0, e != bexp_ref[jnp.maximum(g - 1, 0)])

    @pl.when(jnp.logical_and(g < na, first))
    def _():
        k = ord_scr[0]
        for cp in fetch(e, k % W_SLOTS):
            cp.wait()
        n1 = next_expert_block(e)

        @pl.when(n1 < na)
        def _():
            n2 = next_expert_block(expert_of(n1))

            @pl.when(n2 < na)
            def _():
                start_fetch(expert_of(n2), (k + 2) % W_SLOTS)

        ord_scr[0] = k + 1

    @pl.when(g < na)
    def _():
        _gather_rows_wait(h_hbm, xbuf.at[xslot], xsem.at[xslot], r, n_valid=nval_ref[g])
        ws = (ord_scr[0] + W_SLOTS - 1) % W_SLOTS
        dn = (((1,), (0,)), ((), ()))
        gu = None
        for p in range(ROW_CHUNKS // 2):
            xp = jnp.concatenate([_load_token_major_chunk(xbuf, xslot, 2 * p, r),
                                  _load_token_major_chunk(xbuf, xslot, 2 * p + 1, r)], axis=1).astype(bf16)
            t = lax.dot_general(xp, wgu_buf[ws, 2 * p * LANES:(2 * p + 2) * LANES, :], dn, preferred_element_type=f32)
            gu = t if gu is None else gu + t
        act = (_silu(gu[:, :D_EXP]) * gu[:, D_EXP:]).astype(bf16)
        _store_token_major(o_ref, lax.dot_general(act, wd_buf[ws], dn, preferred_element_type=f32), r)

    @pl.when(g >= na)
    def _():
        o_ref[...] = jnp.zeros_like(o_ref)


def _experts_call(bexp, bend, nval, nact, slot_tok3, h2, w_gate_up, w_down):
    r = ROWS_PER_BLOCK
    nb = N_BLOCKS
    grid_spec = pltpu.PrefetchScalarGridSpec(
        num_scalar_prefetch=4,
        grid=(nb,),
        in_specs=[
            pl.BlockSpec((1, 1, r), lambda g, *_: (0, 0, 0), memory_space=pltpu.SMEM),
            pl.BlockSpec((1, 1, r), lambda g, *_: (jnp.minimum(g + 1, nb - 1), 0, 0), memory_space=pltpu.SMEM),
            pl.BlockSpec(memory_space=pl.ANY),
            pl.BlockSpec(memory_space=pl.ANY),
            pl.BlockSpec(memory_space=pl.ANY),
        ],
        out_specs=pl.BlockSpec((r * ROW_CHUNKS, LANES), lambda g, *_: (g, 0)),
        scratch_shapes=[
            pltpu.VMEM((2, r * ROW_CHUNKS, LANES), ROW_DT),
            pltpu.SemaphoreType.DMA((2,)),
            pltpu.VMEM((W_SLOTS, D, 2 * D_EXP), f32),
            pltpu.VMEM((W_SLOTS, D_EXP, D), f32),
            pltpu.SemaphoreType.DMA((W_SLOTS,)),
            pltpu.SMEM((1,), i32),
        ],
    )
    return pl.pallas_call(
        _experts_kernel,
        grid_spec=grid_spec,
        out_shape=jax.ShapeDtypeStruct((nb * r * ROW_CHUNKS, LANES), ROW_DT),
        compiler_params=_cparams(("arbitrary",)),
        name="experts",
    )(bexp, bend, nval, nact, slot_tok3, slot_tok3, h2, w_gate_up, w_down)


def _combine_kernel(d0_ref, dn_ref, ys_hbm, gate_ref, base_ref, g5_ref, nw_ref, o_ref, buf, sem, *, n_tiles):
    i = pl.program_id(0)
    tt = TT_COMB
    n_rows = TOP_K * tt
    slot = i % 2

    @pl.when(i == 0)
    def _():
        _gather_rows(d0_ref, ys_hbm, buf.at[0], sem.at[0], n_rows)

    @pl.when(i + 1 < n_tiles)
    def _():
        _gather_rows(dn_ref, ys_hbm, buf.at[1 - slot], sem.at[1 - slot], n_rows)

    _gather_rows_wait(ys_hbm, buf.at[slot], sem.at[slot], n_rows)
    g = gate_ref[...]
    gcols = [jnp.broadcast_to(g[:, k:k + 1], (tt, LANES)) for k in range(TOP_K)]
    parts = []
    for c in range(ROW_CHUNKS):
        a = gcols[0] * _load_token_major_chunk(buf, slot, c, tt).astype(f32)
        for k in range(1, TOP_K):
            a = a + gcols[k] * _load_token_major_chunk(buf, slot, c, tt, row0=k * tt * ROW_CHUNKS).astype(f32)
        parts.append(a)
    x2 = base_ref[...] + g5_ref[0, 0] * jnp.concatenate(parts, axis=1)
    o_ref[...] = x2 * lax.rsqrt(jnp.mean(x2 * x2, axis=-1, keepdims=True) + EPS) * nw_ref[...]


def _combine_call(dest3, ys, gates_tk, base, mod6, final_w):
    tt = TT_COMB
    n_tiles = N_TOK // tt
    rb = SEQ // tt
    return pl.pallas_call(
        functools.partial(_combine_kernel, n_tiles=n_tiles),
        grid=(n_tiles,),
        in_specs=[
            pl.BlockSpec((1, 1, TOP_K * tt), lambda i: (0, 0, 0), memory_space=pltpu.SMEM),
            pl.BlockSpec((1, 1, TOP_K * tt), lambda i: (jnp.minimum(i + 1, n_tiles - 1), 0, 0),
                         memory_space=pltpu.SMEM),
            pl.BlockSpec(memory_space=pl.ANY),
            pl.BlockSpec((tt, TOP_K), lambda i: (i, 0)),
            pl.BlockSpec((tt, D), lambda i: (i, 0)),
            pl.BlockSpec((1, 1, 1, D), lambda i: (i // rb, 5, 0, 0)),
            pl.BlockSpec((1, D), lambda i: (0, 0)),
        ],
        out_specs=pl.BlockSpec((tt, D), lambda i: (i, 0)),
        out_shape=jax.ShapeDtypeStruct((N_TOK, D), f32),
        scratch_shapes=[
            pltpu.VMEM((2, TOP_K * tt * ROW_CHUNKS, LANES), ROW_DT),
            pltpu.SemaphoreType.DMA((2,)),
        ],
        compiler_params=_cparams(("arbitrary",)),
        name="combine",
    )(dest3, dest3, ys, gates_tk, base, mod6, final_w)


def _tile_major(a, tt):
    n = a.shape[1] // tt
    return a.reshape(TOP_K, n, tt).transpose(1, 0, 2).reshape(n, 1, TOP_K * tt)


def kernel(x, c, ctx, c_ctx, w_mod, b_mod, norm1_w, w_in, conv_w, conv_b, dt_bias_f, dt_bias_b, a_log_f, a_log_b, d_skip, ssd_norm_w, cmlp_norm_w, w_spatial, b_spatial, w_out, norm2_w, w_router, router_bias, w_gate_up, w_down, w_shared_gate_up, w_shared_down, final_norm_w):
    assert x.shape == (NB_BATCH, SEQ, D) and ctx.shape == (NB_BATCH, CTX, D) and w_mod.shape[0] == 1

    cs = jnp.concatenate([c, c_ctx[None, :], jnp.zeros((SUBLANES - NB_BATCH - 1, D), f32)], axis=0)
    mod6 = _mod_call(cs, w_mod[0], b_mod).reshape(SUBLANES, N_MOD, 1, D)

    wi = w_in[0]
    w_main = jnp.concatenate([wi[:, OFF_Z:OFF_Z + SSD_W], wi[:, 0:OFF_Z], wi[:, OFF_DTB:OFF_U], wi[:, OFF_U:],
                              wi[:, OFF_Z + SSD_W:OFF_XBC]], axis=1).astype(bf16)
    w_dt = jnp.pad(wi[:, OFF_XBC:OFF_DTB], ((0, 0), (0, DT_W - 2 * SSD_H))).astype(bf16)
    pad_h = DT_W - SSD_H
    dtb = jnp.stack([jnp.pad(dt_bias_f[0], (0, pad_h)), jnp.pad(dt_bias_b[0], (0, pad_h))])[:, None, :]
    alog = jnp.stack([jnp.pad(a_log_f[0], (0, pad_h)), jnp.pad(a_log_b[0], (0, pad_h))])[:, None, :]
    e_rows = lax.broadcasted_iota(i32, (2 * LANES, SSD_W), 0) % LANES
    e_cols = lax.broadcasted_iota(i32, (2 * LANES, SSD_W), 1) // SSD_P
    e2 = (e_rows == e_cols).astype(bf16)
    ssd_consts = (conv_w[0][:, :SSD_W], conv_b[0][None, :SSD_W], conv_w[0][:, SSD_W:], conv_b[0][None, SSD_W:],
                  dtb, alog, e2)

    n1 = norm1_w[0][None, :]
    pm_c, dt_c = _inproj_call(ctx.reshape(NB_BATCH * CTX, D), mod6, n1, w_main, w_dt, lambda i: 2)
    x2d = x.reshape(N_TOK, D)
    pm_l, dt_l = _inproj_call(x2d, mod6, n1, w_main, w_dt, lambda i: i // (SEQ // TM_INPROJ))

    h0 = _ctx_state_call(pm_c, dt_c, ssd_consts)
    dsk = jnp.repeat(d_skip[0], SSD_P)[None, :]
    bs_e = jnp.repeat(jnp.swapaxes(b_spatial[0], 0, 1), CM_HD, axis=1)
    ycat = _ssd_main_call(pm_l, dt_l, ssd_consts, h0, dsk, ssd_norm_w[0][None, :], cmlp_norm_w[0][None, :],
                          w_spatial[0].astype(bf16), bs_e)

    base, h2, logits_t = _outproj_call(ycat, x2d, mod6, norm2_w[0][None, :], w_out[0].astype(bf16),
                                       jnp.swapaxes(w_router[0], 0, 1).astype(bf16),
                                       w_shared_gate_up[0].astype(bf16), w_shared_down[0].astype(bf16))
    h2 = h2.reshape(N_TOK, ROW_CHUNKS, LANES)

    dest, gates, bexp, bend, nval = _route_call(logits_t, router_bias[0][:, None])
    bend = bend[:, 0]
    nact = bend[N_EXP - 1:]
    dest3 = _tile_major(dest, TT_COMB)
    slot_tok = _slots_call(dest3)
    ys = _experts_call(bexp.reshape(N_BLOCKS), bend, nval.reshape(N_BLOCKS), nact,
                       slot_tok.reshape(N_BLOCKS, 1, ROWS_PER_BLOCK),
                       h2, w_gate_up[0], w_down[0]).reshape(N_BLOCKS * ROWS_PER_BLOCK, ROW_CHUNKS, LANES)
    out = _combine_call(dest3, ys, gates.T, base, mod6, final_norm_w[None, :])
    return out.reshape(NB_BATCH, SEQ, D)
```

```python
import functools

import jax
import jax.numpy as jnp
from jax import lax
from jax.experimental import pallas as pl
from jax.experimental.pallas import tpu as pltpu

f32 = jnp.float32
bf16 = jnp.bfloat16
i32 = jnp.int32

D = 2048
NB_BATCH = 2
SEQ = 4096
CTX = 256
N_MOD = 6
SSD_W = 1024
SSD_H = 16
SSD_P = 64
SSD_G = 2
D_STATE = 128
CHUNK = 128
D_CONV = 4
BC_W = 2 * SSD_G * D_STATE
CM_W = 1024
CM_H = 8
CM_HD = 128
OFF_Z = SSD_W
OFF_XBC = OFF_Z + SSD_W + BC_W
OFF_DTF = OFF_XBC + SSD_H
OFF_DTB = OFF_DTF + SSD_H
OFF_U = OFF_DTB + CM_W
IN_PROJ_DIM = OFF_U + CM_W
N_EXP = 256
TOP_K = 8
N_GRP = 8
TOP_GRP = 4
GRP_SZ = N_EXP // N_GRP
D_EXP = 512
ROUTED_SCALE = 2.5
EPS = 1e-6

P_MAIN = 4 * 1024 + BC_W
DT_W = 128

LANES = 128
SUBLANES = 8
VMEM_LIMIT = 56 * 1024 * 1024

TM_INPROJ = 1024
TN_INPROJ = 512
TM_OUT = 256
TM_SHARED = 512
TT_ROUTE = 512
ROWS_PER_BLOCK = 128
N_TOK = NB_BATCH * SEQ
N_BLOCKS = N_TOK * TOP_K // ROWS_PER_BLOCK + N_EXP
TT_COMB = 64


def _cparams(sem):
    return pltpu.CompilerParams(dimension_semantics=sem, vmem_limit_bytes=VMEM_LIMIT)


def _sigmoid(x):
    return 0.5 * jnp.tanh(0.5 * x) + 0.5


def _silu(x):
    return x * _sigmoid(x)


def _dot(a, b):
    return jnp.dot(a, b, preferred_element_type=f32)


def _mod_kernel(c_ref, w_ref, b_ref, o_ref):
    a = _silu(c_ref[...]).astype(bf16)
    o_ref[...] = _dot(a, w_ref[...].astype(bf16)) + b_ref[...]


def _mod_call(cs, w_mod, b_mod):
    tn = 1024
    n = w_mod.shape[1]
    return pl.pallas_call(
        _mod_kernel,
        grid=(n // tn,),
        in_specs=[
            pl.BlockSpec((SUBLANES, D), lambda j: (0, 0)),
            pl.BlockSpec((D, tn), lambda j: (0, j)),
            pl.BlockSpec((1, tn), lambda j: (0, j)),
        ],
        out_specs=pl.BlockSpec((SUBLANES, tn), lambda j: (0, j)),
        out_shape=jax.ShapeDtypeStruct((SUBLANES, n), f32),
        compiler_params=_cparams(("arbitrary",)),
        name="mod",
    )(cs, w_mod, b_mod)


def _inproj_kernel(x_ref, sh_ref, sc_ref, nw_ref, w_ref, wdt_ref, p_ref, dt_ref, h_scr):
    @pl.when(pl.program_id(1) == 0)
    def _():
        x = x_ref[...]
        y = x * lax.rsqrt(jnp.mean(x * x, axis=-1, keepdims=True) + EPS) * nw_ref[...]
        h = (y * (1.0 + sc_ref[0, 0]) + sh_ref[0, 0]).astype(bf16)
        h_scr[...] = h
        dt_ref[...] = _dot(h, wdt_ref[...])

    p_ref[...] = _dot(h_scr[...], w_ref[...])


def _inproj_call(x2d, mod6, norm_w, w_main, w_dt, mod_row_fn):
    m = x2d.shape[0]
    tm, tn = min(TM_INPROJ, m), TN_INPROJ
    return pl.pallas_call(
        _inproj_kernel,
        grid=(m // tm, P_MAIN // tn),
        in_specs=[
            pl.BlockSpec((tm, D), lambda i, j: (i, 0)),
            pl.BlockSpec((1, 1, 1, D), lambda i, j: (mod_row_fn(i), 0, 0, 0)),
            pl.BlockSpec((1, 1, 1, D), lambda i, j: (mod_row_fn(i), 1, 0, 0)),
            pl.BlockSpec((1, D), lambda i, j: (0, 0)),
            pl.BlockSpec((D, tn), lambda i, j: (0, j)),
            pl.BlockSpec((D, DT_W), lambda i, j: (0, 0)),
        ],
        out_specs=[
            pl.BlockSpec((tm, tn), lambda i, j: (i, j)),
            pl.BlockSpec((tm, DT_W), lambda i, j: (i, 0)),
        ],
        out_shape=[
            jax.ShapeDtypeStruct((m, P_MAIN), f32),
            jax.ShapeDtypeStruct((m, DT_W), f32),
        ],
        scratch_shapes=[pltpu.VMEM((tm, D), bf16)],
        compiler_params=_cparams(("arbitrary", "arbitrary")),
        name="inproj",
    )(x2d, mod6, mod6, norm_w, w_main, w_dt)


def _split3(q):
    hi = q.astype(bf16)
    r1 = q - hi.astype(f32)
    mid = r1.astype(bf16)
    lo = (r1 - mid.astype(f32)).astype(bf16)
    return hi, mid, lo


def _split2_cat(q):
    hi = q.astype(bf16)
    lo = (q - hi.astype(f32)).astype(bf16)
    return jnp.concatenate([hi, lo], axis=1)


def _conv_silu(buf, w_ref, b_ref):
    w = w_ref[...]
    y = (w[0:1] * buf[7:135, :] + w[1:2] * buf[8:136, :] + w[2:3] * buf[9:137, :]
         + w[3:4] * buf[10:138, :] + b_ref[...])
    return _silu(y)


def _ssd_chunk(ph, first, last, r0, xm_ref, xp_ref, xn_ref, bm_ref, bp_ref, bn_ref, dt_ref,
               cwx_ref, cbx_ref, cwb_ref, cbb_ref, dtb_ref, alog_ref, e2_ref, tri_ref,
               s_scr, xbuf, bcbuf, xs_cache, bc_cache, with_output):
    @pl.when(ph == 0)
    def _():
        zero8x = jnp.zeros((SUBLANES, SSD_W), f32)
        zero8b = jnp.zeros((SUBLANES, BC_W), f32)
        xbuf[0:8, :] = jnp.where(first, zero8x, xp_ref[...])
        xbuf[8:136, :] = xm_ref[...]
        xbuf[136:144, :] = jnp.where(last, zero8x, xn_ref[...])
        bcbuf[0:8, :] = jnp.where(first, zero8b, bp_ref[...])
        bcbuf[8:136, :] = bm_ref[...]
        bcbuf[136:144, :] = jnp.where(last, zero8b, bn_ref[...])
        xs_cache[pl.ds(r0, CHUNK), :] = _conv_silu(xbuf, cwx_ref, cbx_ref).astype(bf16)
        bc_cache[pl.ds(r0, CHUNK), :] = _conv_silu(bcbuf, cwb_ref, cbb_ref).astype(bf16)

    xs = xs_cache[pl.ds(r0, CHUNK), :].astype(f32)
    bcv = bc_cache[pl.ds(r0, CHUNK), :]

    dtr = dt_ref[...]
    dtr = jnp.where(ph == 0, dtr, pltpu.roll(dtr, LANES - SSD_H, axis=1))
    pre = dtr + dtb_ref[0]
    dt = jnp.maximum(pre, 0.0) + jnp.log(1.0 + jnp.exp(-jnp.abs(pre)))
    d_a = dt * (-jnp.exp(alog_ref[0]))

    tri = tri_ref[0]
    mask = tri > 0
    col = lax.broadcasted_iota(i32, (CHUNK, CHUNK), 1)
    hi, mid, lo = _split3(d_a)
    acs = _dot(tri, hi) + _dot(tri, mid) + _dot(tri, lo)
    tot = jnp.sum(d_a, axis=0, keepdims=True)
    e_a = jnp.exp(acs)
    dte = jnp.exp(tot - acs)
    cd = jnp.broadcast_to(jnp.exp(tot), (SUBLANES, LANES))
    ex = _dot(jnp.concatenate([_split2_cat(dt), _split2_cat(e_a), _split2_cat(dte), _split2_cat(cd)], axis=0),
              e2_ref[...])
    dt_e = ex[0:128]
    ea_e = ex[128:256]
    dte_e = ex[256:384]
    cd_e = ex[384:385]
    xdt = xs * dt_e

    y_parts = []
    if with_output:
        acs_t = acs.T
        lane_lo = col < SSD_P
        g_mats = []
        for g in range(SSD_G):
            b_g = bcv[:, g * D_STATE:(g + 1) * D_STATE]
            c_g = bcv[:, (SSD_G + g) * D_STATE:(SSD_G + g + 1) * D_STATE]
            g_mats.append(lax.dot_general(c_g, b_g, (((1,), (1,)), ((), ())), preferred_element_type=f32))
        for p in range(SSD_H // 2):
            g = (2 * p) // (SSD_H // SSD_G)
            xp = xdt[:, p * LANES:(p + 1) * LANES]
            acc = None
            for k in range(2):
                h = 2 * p + k
                seg = jnp.broadcast_to(acs[:, h:h + 1], (CHUNK, CHUNK)) - jnp.broadcast_to(acs_t[h:h + 1, :], (CHUNK, CHUNK))
                dec = jnp.where(mask, jnp.exp(jnp.minimum(seg, 0.0)), 0.0)
                m_h = (g_mats[g] * dec).astype(bf16)
                x_h = jnp.where(lane_lo if k == 0 else jnp.logical_not(lane_lo), xp, 0.0).astype(bf16)
                t = _dot(m_h, x_h)
                acc = t if acc is None else acc + t
            y_parts.append(acc)

    gw = SSD_W // SSD_G
    y_off = []
    for g in range(SSD_G):
        sl = slice(g * gw, (g + 1) * gw)
        b_g = bcv[:, g * D_STATE:(g + 1) * D_STATE]
        s_g = s_scr[g]
        if with_output:
            c_g = bcv[:, (SSD_G + g) * D_STATE:(SSD_G + g + 1) * D_STATE]
            y_off.append(_dot(c_g, s_g.astype(bf16)) * ea_e[:, sl])
        upd = lax.dot_general(b_g, (xdt[:, sl] * dte_e[:, sl]).astype(bf16), (((0,), (0,)), ((), ())),
                              preferred_element_type=f32)
        s_scr[g] = s_g * cd_e[:, sl] + upd
    if not with_output:
        return xs, None
    y = jnp.concatenate(y_parts, axis=1) + jnp.concatenate(y_off, axis=1)
    return xs, y


def _ctx_state_kernel(xm_ref, xp_ref, xn_ref, bm_ref, bp_ref, bn_ref, dt_ref,
                      cwx_ref, cbx_ref, cwb_ref, cbb_ref, dtb_ref, alog_ref, e2_ref, tri_ref,
                      h_ref, s_scr, xbuf, bcbuf, xs_cache, bc_cache, *, nc):
    ph = pl.program_id(1)
    c = pl.program_id(2)
    ci = jnp.where(ph == 0, c, nc - 1 - c)

    @pl.when(c == 0)
    def _():
        s_scr[...] = jnp.zeros_like(s_scr)

    _ssd_chunk(ph, ci == 0, ci == nc - 1, pl.multiple_of(ci * CHUNK, CHUNK),
               xm_ref, xp_ref, xn_ref, bm_ref, bp_ref, bn_ref, dt_ref,
               cwx_ref, cbx_ref, cwb_ref, cbb_ref, dtb_ref, alog_ref, e2_ref, tri_ref,
               s_scr, xbuf, bcbuf, xs_cache, bc_cache, False)

    @pl.when(c == nc - 1)
    def _():
        h_ref[0, 0] = s_scr[...]


def _gelu_tanh(x):
    return 0.5 * x * (1.0 + jnp.tanh(0.7978845608028654 * (x + 0.044715 * x * x * x)))


def _ssd_main_kernel(xm_ref, xp_ref, xn_ref, bm_ref, bp_ref, bn_ref, dt_ref,
                     cwx_ref, cbx_ref, cwb_ref, cbb_ref, dtb_ref, alog_ref, e2_ref, tri_ref,
                     h0_ref, z_ref, u_ref, v_ref, dsk_ref, snw_ref, cnw_ref, ws_ref, bs_ref,
                     o_ref, s_scr, xbuf, bcbuf, xs_cache, bc_cache, yf_scr, *, nc):
    ph = pl.program_id(1)
    c = pl.program_id(2)
    ci = jnp.where(ph == 0, c, nc - 1 - c)
    r0 = pl.multiple_of(ci * CHUNK, CHUNK)

    @pl.when(c == 0)
    def _():
        s_scr[...] = h0_ref[0, 0]

    xs, y = _ssd_chunk(ph, ci == 0, ci == nc - 1, r0, xm_ref, xp_ref, xn_ref, bm_ref, bp_ref, bn_ref, dt_ref,
                       cwx_ref, cbx_ref, cwb_ref, cbb_ref, dtb_ref, alog_ref, e2_ref, tri_ref,
                       s_scr, xbuf, bcbuf, xs_cache, bc_cache, True)

    @pl.when(ph == 0)
    def _():
        yf_scr[pl.ds(r0, CHUNK), :] = y

    @pl.when(ph == 1)
    def _():
        yt = (yf_scr[pl.ds(r0, CHUNK), :] + y + dsk_ref[...] * xs) * _silu(z_ref[...])
        yt = yt * lax.rsqrt(jnp.mean(yt * yt, axis=-1, keepdims=True) + EPS) * snw_ref[...]
        o_ref[:, 0:SSD_W] = yt.astype(bf16)
        u = _gelu_tanh(u_ref[...])
        v = _gelu_tanh(v_ref[...])
        vn = (v * lax.rsqrt(jnp.mean(v * v, axis=-1, keepdims=True) + EPS) * cnw_ref[...]).astype(bf16)
        mixed = [_dot(ws_ref[h], vn[:, h * CM_HD:(h + 1) * CM_HD]) for h in range(CM_H)]
        o_ref[:, SSD_W:SSD_W + CM_W] = (u * (jnp.concatenate(mixed, axis=1) + bs_ref[...])).astype(bf16)


def _ssd_common_specs(nc, row_blocks_total):
    cpb = nc
    sub = CHUNK // SUBLANES

    def dt_i(b, ph, c):
        return b * cpb + jnp.where(ph == 0, c, nc - 1 - c)

    def main_i(b, ph, c):
        return b * cpb + jnp.where(ph == 0, c, nc - 1)

    def prev_i(b, ph, c):
        return jnp.maximum(main_i(b, ph, c) * sub - 1, 0)

    def next_i(b, ph, c):
        return jnp.minimum(main_i(b, ph, c) * sub + sub, row_blocks_total * sub - 1)

    bc_col = (4 * 1024) // BC_W
    specs = [
        pl.BlockSpec((CHUNK, SSD_W), lambda b, ph, c: (main_i(b, ph, c), 0)),
        pl.BlockSpec((SUBLANES, SSD_W), lambda b, ph, c: (prev_i(b, ph, c), 0)),
        pl.BlockSpec((SUBLANES, SSD_W), lambda b, ph, c: (next_i(b, ph, c), 0)),
        pl.BlockSpec((CHUNK, BC_W), lambda b, ph, c: (main_i(b, ph, c), bc_col)),
        pl.BlockSpec((SUBLANES, BC_W), lambda b, ph, c: (prev_i(b, ph, c), bc_col)),
        pl.BlockSpec((SUBLANES, BC_W), lambda b, ph, c: (next_i(b, ph, c), bc_col)),
        pl.BlockSpec((CHUNK, DT_W), lambda b, ph, c: (dt_i(b, ph, c), 0)),
        pl.BlockSpec((D_CONV, SSD_W), lambda b, ph, c: (0, 0)),
        pl.BlockSpec((1, SSD_W), lambda b, ph, c: (0, 0)),
        pl.BlockSpec((D_CONV, BC_W), lambda b, ph, c: (0, 0)),
        pl.BlockSpec((1, BC_W), lambda b, ph, c: (0, 0)),
        pl.BlockSpec((1, 1, DT_W), lambda b, ph, c: (ph, 0, 0)),
        pl.BlockSpec((1, 1, DT_W), lambda b, ph, c: (ph, 0, 0)),
        pl.BlockSpec((2 * LANES, SSD_W), lambda b, ph, c: (0, 0)),
        pl.BlockSpec((1, CHUNK, CHUNK), lambda b, ph, c: (ph, 0, 0)),
    ]
    return specs


def _ssd_scratch(seq_rows):
    return [
        pltpu.VMEM((SSD_G, D_STATE, SSD_W // SSD_G), f32),
        pltpu.VMEM((CHUNK + 2 * SUBLANES, SSD_W), f32),
        pltpu.VMEM((CHUNK + 2 * SUBLANES, BC_W), f32),
        pltpu.VMEM((seq_rows, SSD_W), bf16),
        pltpu.VMEM((seq_rows, BC_W), bf16),
    ]


def _ctx_state_call(pm, dtm, ssd_consts):
    nc = CTX // CHUNK
    specs = _ssd_common_specs(nc, NB_BATCH * nc)
    return pl.pallas_call(
        functools.partial(_ctx_state_kernel, nc=nc),
        grid=(NB_BATCH, 2, nc),
        in_specs=specs,
        out_specs=pl.BlockSpec((1, 1, SSD_G, D_STATE, SSD_W // SSD_G), lambda b, ph, c: (b, ph, 0, 0, 0)),
        out_shape=jax.ShapeDtypeStruct((NB_BATCH, 2, SSD_G, D_STATE, SSD_W // SSD_G), f32),
        scratch_shapes=_ssd_scratch(CTX),
        compiler_params=_cparams(("arbitrary", "arbitrary", "arbitrary")),
        name="ssd_ctx",
    )(pm, pm, pm, pm, pm, pm, dtm, *ssd_consts)


def _ssd_main_call(pm, dtm, ssd_consts, h0, dsk, snw, cnw, ws, bs_e):
    nc = SEQ // CHUNK
    specs = _ssd_common_specs(nc, NB_BATCH * nc)

    def zrow(b, ph, c):
        return b * nc + jnp.where(ph == 0, nc - 1, nc - 1 - c)

    specs += [
        pl.BlockSpec((1, 1, SSD_G, D_STATE, SSD_W // SSD_G), lambda b, ph, c: (b, ph, 0, 0, 0)),
        pl.BlockSpec((CHUNK, SSD_W), lambda b, ph, c: (zrow(b, ph, c), 1)),
        pl.BlockSpec((CHUNK, CM_W), lambda b, ph, c: (zrow(b, ph, c), 2)),
        pl.BlockSpec((CHUNK, CM_W), lambda b, ph, c: (zrow(b, ph, c), 3)),
        pl.BlockSpec((1, SSD_W), lambda b, ph, c: (0, 0)),
        pl.BlockSpec((1, SSD_W), lambda b, ph, c: (0, 0)),
        pl.BlockSpec((1, CM_W), lambda b, ph, c: (0, 0)),
        pl.BlockSpec((CM_H, CHUNK, CHUNK), lambda b, ph, c: (0, 0, 0)),
        pl.BlockSpec((CHUNK, CM_W), lambda b, ph, c: (0, 0)),
    ]
    return pl.pallas_call(
        functools.partial(_ssd_main_kernel, nc=nc),
        grid=(NB_BATCH, 2, nc),
        in_specs=specs,
        out_specs=pl.BlockSpec((CHUNK, SSD_W + CM_W), lambda b, ph, c: (zrow(b, ph, c), 0)),
        out_shape=jax.ShapeDtypeStruct((N_TOK, SSD_W + CM_W), bf16),
        scratch_shapes=_ssd_scratch(SEQ) + [pltpu.VMEM((SEQ, SSD_W), f32)],
        compiler_params=_cparams(("arbitrary", "arbitrary", "arbitrary")),
        name="ssd_main",
    )(pm, pm, pm, pm, pm, pm, dtm, *ssd_consts, h0, pm, pm, pm, dsk, snw, cnw, ws, bs_e)


ROW_CHUNKS = D // LANES
ROW_DT = f32


def _store_token_major(ref, val, n_rows, row0=0):
    for c in range(ROW_CHUNKS):
        ref[pl.ds(row0 + c, n_rows, stride=ROW_CHUNKS), :] = val[:, c * LANES:(c + 1) * LANES].astype(ref.dtype)


def _load_token_major_chunk(ref, lead, c, n_rows, row0=0):
    idx = pl.ds(row0 + c, n_rows, stride=ROW_CHUNKS)
    return ref[idx, :] if lead is None else ref[lead, idx, :]


def _outproj_kernel(y_ref, x_ref, g_ref, sh_ref, sc_ref, g5_ref, nw_ref, wo_ref, wr_ref, w1_ref, w2_ref,
                    base_ref, h_ref, lg_ref):
    x1 = x_ref[...] + g_ref[0, 0] * _dot(y_ref[...], wo_ref[...])
    y = x1 * lax.rsqrt(jnp.mean(x1 * x1, axis=-1, keepdims=True) + EPS) * nw_ref[...]
    h = y * (1.0 + sc_ref[0, 0]) + sh_ref[0, 0]
    _store_token_major(h_ref, h, h.shape[0])
    hb = h.astype(bf16)
    lg_ref[...] = lax.dot_general(wr_ref[...], hb, (((1,), (1,)), ((), ())), preferred_element_type=f32)
    gu = _dot(hb, w1_ref[...])
    act = (_silu(gu[:, :D_EXP]) * gu[:, D_EXP:]).astype(bf16)
    base_ref[...] = x1 + g5_ref[0, 0] * _dot(act, w2_ref[...])


def _outproj_call(ycat, x2d, mod6, norm_w, w_out, w_router_t, w1, w2):
    tm = TM_OUT
    rb = SEQ // tm
    const = lambda i: (0, 0)
    return pl.pallas_call(
        _outproj_kernel,
        grid=(N_TOK // tm,),
        in_specs=[
            pl.BlockSpec((tm, D), lambda i: (i, 0)),
            pl.BlockSpec((tm, D), lambda i: (i, 0)),
            pl.BlockSpec((1, 1, 1, D), lambda i: (i // rb, 2, 0, 0)),
            pl.BlockSpec((1, 1, 1, D), lambda i: (i // rb, 3, 0, 0)),
            pl.BlockSpec((1, 1, 1, D), lambda i: (i // rb, 4, 0, 0)),
            pl.BlockSpec((1, 1, 1, D), lambda i: (i // rb, 5, 0, 0)),
            pl.BlockSpec((1, D), const),
            pl.BlockSpec((D, D), const),
            pl.BlockSpec((N_EXP, D), const),
            pl.BlockSpec((D, 2 * D_EXP), const),
            pl.BlockSpec((D_EXP, D), const),
        ],
        out_specs=[
            pl.BlockSpec((tm, D), lambda i: (i, 0)),
            pl.BlockSpec((tm * ROW_CHUNKS, LANES), lambda i: (i, 0)),
            pl.BlockSpec((N_EXP, tm), lambda i: (0, i)),
        ],
        out_shape=[
            jax.ShapeDtypeStruct((N_TOK, D), f32),
            jax.ShapeDtypeStruct((N_TOK * ROW_CHUNKS, LANES), ROW_DT),
            jax.ShapeDtypeStruct((N_EXP, N_TOK), f32),
        ],
        compiler_params=_cparams(("arbitrary",)),
        name="outproj",
    )(ycat, x2d, mod6, mod6, mod6, mod6, norm_w, w_out, w_router_t, w1, w2)


def _route_kernel(lg_ref, bias_ref, dest_ref, gate_ref, bexp_ref, bend_ref, nval_ref,
                  idx_scr, pos_scr, cnt_scr, *, n_tiles):
    j = pl.program_id(0)
    tt = TT_ROUTE
    neg = -jnp.inf

    @pl.when(j == 0)
    def _():
        cnt_scr[...] = jnp.zeros_like(cnt_scr)

    s = _sigmoid(lg_ref[...])
    biased = s + bias_ref[...]
    v3 = biased.reshape(N_GRP, GRP_SZ, tt)
    io3 = lax.broadcasted_iota(i32, (N_GRP, GRP_SZ, tt), 1)
    m1 = jnp.max(v3, axis=1, keepdims=True)
    i1 = jnp.min(jnp.where(v3 == m1, io3, GRP_SZ), axis=1, keepdims=True)
    m2 = jnp.max(jnp.where(io3 == i1, neg, v3), axis=1, keepdims=True)
    gs = m1 + m2
    gio = lax.broadcasted_iota(i32, (N_GRP, 1, tt), 0)
    sel = jnp.zeros((N_GRP, 1, tt), dtype=jnp.bool_)
    cur_g = gs
    for _ in range(TOP_GRP):
        m = jnp.max(cur_g, axis=0, keepdims=True)
        gi = jnp.min(jnp.where(cur_g == m, gio, N_GRP), axis=0, keepdims=True)
        hit = gio == gi
        sel = jnp.logical_or(sel, hit)
        cur_g = jnp.where(hit, neg, cur_g)
    cur = jnp.where(sel, v3, neg).reshape(N_EXP, tt)

    eio = lax.broadcasted_iota(i32, (N_EXP, tt), 0)
    onehot = jnp.zeros((N_EXP, tt), f32)
    idxs, gts = [], []
    for _ in range(TOP_K):
        m = jnp.max(cur, axis=0, keepdims=True)
        ei = jnp.min(jnp.where(cur == m, eio, N_EXP), axis=0, keepdims=True)
        hit = eio == ei
        gts.append(jnp.sum(jnp.where(hit, s, 0.0), axis=0, keepdims=True))
        idxs.append(ei)
        onehot = jnp.where(hit, 1.0, onehot)
        cur = jnp.where(hit, neg, cur)
    gsum = gts[0]
    for k in range(1, TOP_K):
        gsum = gsum + gts[k]

    r_io = lax.broadcasted_iota(i32, (tt, tt), 0)
    c_io = lax.broadcasted_iota(i32, (tt, tt), 1)
    upper = (r_io <= c_io).astype(f32).astype(bf16)
    oh = onehot.astype(bf16)
    incl = _dot(oh, upper)
    base = cnt_scr[...]
    pos = incl - 1.0 + jnp.concatenate([base] * (tt // LANES), axis=1)
    cnt_scr[...] = base + _dot(oh, jnp.ones((tt, LANES), bf16))
    c0 = pl.multiple_of(j * tt, tt)
    for k in range(TOP_K):
        hit = eio == idxs[k]
        idx_scr[k:k + 1, pl.ds(c0, tt)] = idxs[k]
        pos_scr[k:k + 1, pl.ds(c0, tt)] = jnp.sum(jnp.where(hit, pos, 0.0), axis=0, keepdims=True)
        gate_ref[k:k + 1, pl.ds(c0, tt)] = gts[k] / gsum * ROUTED_SCALE

    @pl.when(j == n_tiles - 1)
    def _():
        cnt = cnt_scr[...]
        r = float(ROWS_PER_BLOCK)
        nblk = jnp.floor((cnt + (r - 1.0)) * (1.0 / r))
        er = lax.broadcasted_iota(i32, (N_EXP, N_EXP), 0)
        ec = lax.broadcasted_iota(i32, (N_EXP, N_EXP), 1)
        lower = (er >= ec).astype(f32).astype(bf16)
        bend = _dot(lower, nblk.astype(bf16))
        pstart = (bend - nblk) * r
        pstart_t = jnp.concatenate([pstart] * (tt // LANES), axis=1)
        def slots(t, carry):
            c1 = pl.multiple_of(t * tt, tt)
            for k in range(TOP_K):
                hit = eio == idx_scr[k:k + 1, pl.ds(c1, tt)]
                d = jnp.sum(jnp.where(hit, pstart_t, 0.0), axis=0, keepdims=True) + pos_scr[k:k + 1, pl.ds(c1, tt)]
                dest_ref[k:k + 1, pl.ds(c1, tt)] = d.astype(i32)
            return carry

        lax.fori_loop(0, n_tiles, slots, 0)
        bio = lax.broadcasted_iota(i32, (N_EXP, N_BLOCKS), 1).astype(f32)
        bend_t = jnp.concatenate([bend] * (N_BLOCKS // LANES), axis=1)
        be = jnp.sum(jnp.where(bend_t <= bio, 1.0, 0.0), axis=0, keepdims=True)
        bexp_ref[...] = jnp.minimum(be, float(N_EXP - 1)).astype(i32)
        bend_ref[...] = bend.astype(i32)
        cnt_t = jnp.concatenate([cnt] * (N_BLOCKS // LANES), axis=1)
        pst_b = jnp.concatenate([pstart] * (N_BLOCKS // LANES), axis=1)
        lo = jnp.maximum(pst_b, bio * r)
        hi = jnp.minimum(pst_b + cnt_t, (bio + 1.0) * r)
        nval_ref[...] = jnp.sum(jnp.maximum(hi - lo, 0.0), axis=0, keepdims=True).astype(i32)


def _route_call(logits_t, bias_col):
    tt = TT_ROUTE
    n_tiles = N_TOK // tt
    return pl.pallas_call(
        functools.partial(_route_kernel, n_tiles=n_tiles),
        grid=(n_tiles,),
        in_specs=[
            pl.BlockSpec((N_EXP, tt), lambda j: (0, j)),
            pl.BlockSpec((N_EXP, 1), lambda j: (0, 0)),
        ],
        out_specs=[
            pl.BlockSpec((TOP_K, N_TOK), lambda j: (0, 0)),
            pl.BlockSpec((TOP_K, N_TOK), lambda j: (0, 0)),
            pl.BlockSpec((1, N_BLOCKS), lambda j: (0, 0)),
            pl.BlockSpec((N_EXP, LANES), lambda j: (0, 0)),
            pl.BlockSpec((1, N_BLOCKS), lambda j: (0, 0)),
        ],
        out_shape=[
            jax.ShapeDtypeStruct((TOP_K, N_TOK), i32),
            jax.ShapeDtypeStruct((TOP_K, N_TOK), f32),
            jax.ShapeDtypeStruct((1, N_BLOCKS), i32),
            jax.ShapeDtypeStruct((N_EXP, LANES), i32),
            jax.ShapeDtypeStruct((1, N_BLOCKS), i32),
        ],
        scratch_shapes=[
            pltpu.VMEM((TOP_K, N_TOK), i32),
            pltpu.VMEM((TOP_K, N_TOK), f32),
            pltpu.VMEM((N_EXP, LANES), f32),
        ],
        compiler_params=_cparams(("arbitrary",)),
        name="route",
    )(logits_t, bias_col)


def _slots_kernel(dest_ref, tab_ref):
    i = pl.program_id(0)
    tt = TT_COMB

    @pl.when(i == 0)
    def _():
        def zero_row(b, c):
            b0 = b * ROWS_PER_BLOCK
            for l in range(ROWS_PER_BLOCK):
                tab_ref[b0 + l] = 0
            return c
        lax.fori_loop(0, N_BLOCKS, zero_row, 0)

    base = i * tt
    for rr in range(tt):
        for k in range(TOP_K):
            tab_ref[dest_ref[0, 0, k * tt + rr]] = base + rr


def _slots_call(dest3):
    tt = TT_COMB
    return pl.pallas_call(
        _slots_kernel,
        grid=(N_TOK // tt,),
        in_specs=[pl.BlockSpec((1, 1, TOP_K * tt), lambda i: (i, 0, 0), memory_space=pltpu.SMEM)],
        out_specs=pl.BlockSpec(memory_space=pltpu.SMEM),
        out_shape=jax.ShapeDtypeStruct((N_BLOCKS * ROWS_PER_BLOCK,), i32),
        compiler_params=_cparams(("arbitrary",)),
        name="slots",
    )(dest3)


GATHER_GROUP = 16


def _gather_rows(idx_ref, src_hbm, dst, sem, n_rows, n_valid=None):
    def issue(lo, hi):
        for r in range(lo, hi):
            t = idx_ref[0, 0, r]
            pltpu.make_async_copy(src_hbm.at[t], dst.at[pl.ds(r * ROW_CHUNKS, ROW_CHUNKS), :], sem).start(
                priority=r % 2)

    if n_valid is None:
        issue(0, n_rows)
        return
    for lo in range(0, n_rows, GATHER_GROUP):
        pl.when(lo < n_valid)(functools.partial(issue, lo, lo + GATHER_GROUP))


def _gather_rows_wait(src_hbm, dst, sem, n_rows, n_valid=None):
    def body(r, carry):
        pltpu.make_async_copy(src_hbm.at[0], dst.at[pl.ds(0, ROW_CHUNKS), :], sem).wait()
        return carry
    if n_valid is not None:
        n_rows = ((n_valid + (GATHER_GROUP - 1)) // GATHER_GROUP) * GATHER_GROUP
    lax.fori_loop(0, n_rows, body, 0, unroll=8 if n_valid is None else 1)


W_CHUNKS = 4
W_SLOTS = 3


def _experts_kernel(bexp_ref, bend_ref, nval_ref, nact_ref, tok0_ref, tokn_ref, h_hbm, wgu_hbm, wd_hbm, o_ref,
                    xbuf, xsem, wgu_buf, wd_buf, wsem, ord_scr):
    g = pl.program_id(0)
    na = nact_ref[0]
    e = bexp_ref[g]
    r = ROWS_PER_BLOCK
    xslot = g % 2

    def fetch(ex, slot):
        cps = []
        for q in range(W_CHUNKS):
            a = D // W_CHUNKS
            cps.append(pltpu.make_async_copy(wgu_hbm.at[ex, pl.ds(q * a, a), :],
                                             wgu_buf.at[slot, pl.ds(q * a, a), :], wsem.at[slot]))
            a = D_EXP // W_CHUNKS
            cps.append(pltpu.make_async_copy(wd_hbm.at[ex, pl.ds(q * a, a), :],
                                             wd_buf.at[slot, pl.ds(q * a, a), :], wsem.at[slot]))
        return cps

    def start_fetch(ex, slot):
        for n, cp in enumerate(fetch(ex, slot)):
            cp.start(priority=(n // 2) % 2)

    def next_expert_block(ex):
        return bend_ref[ex]

    def expert_of(blk):
        return bexp_ref[jnp.minimum(blk, N_BLOCKS - 1)]

    @pl.when(g == 0)
    def _():
        ord_scr[0] = 0
        start_fetch(e, 0)
        n1 = next_expert_block(e)

        @pl.when(n1 < na)
        def _():
            start_fetch(expert_of(n1), 1)

        xbuf[...] = jnp.zeros_like(xbuf)
        _gather_rows(tok0_ref, h_hbm, xbuf.at[0], xsem.at[0], r, n_valid=nval_ref[0])

    @pl.when(g + 1 < na)
    def _():
        _gather_rows(tokn_ref, h_hbm, xbuf.at[1 - xslot], xsem.at[1 - xslot], r,
                     n_valid=nval_ref[jnp.minimum(g + 1, N_BLOCKS - 1)])

    first = jnp.logical_or(g == 0, e != bexp_ref[jnp.maximum(g - 1, 0)])

    @pl.when(jnp.logical_and(g < na, first))
    def _():
        k = ord_scr[0]
        for cp in fetch(e, k % W_SLOTS):
            cp.wait()
        n1 = next_expert_block(e)

        @pl.when(n1 < na)
        def _():
            n2 = next_expert_block(expert_of(n1))

            @pl.when(n2 < na)
            def _():
                start_fetch(expert_of(n2), (k + 2) % W_SLOTS)

        ord_scr[0] = k + 1

    @pl.when(g < na)
    def _():
        _gather_rows_wait(h_hbm, xbuf.at[xslot], xsem.at[xslot], r, n_valid=nval_ref[g])
        ws = (ord_scr[0] + W_SLOTS - 1) % W_SLOTS
        dn = (((1,), (0,)), ((), ()))
        gu = None
        for p in range(ROW_CHUNKS // 2):
            xp = jnp.concatenate([_load_token_major_chunk(xbuf, xslot, 2 * p, r),
                                  _load_token_major_chunk(xbuf, xslot, 2 * p + 1, r)], axis=1).astype(bf16)
            t = lax.dot_general(xp, wgu_buf[ws, 2 * p * LANES:(2 * p + 2) * LANES, :], dn, preferred_element_type=f32)
            gu = t if gu is None else gu + t
        act = (_silu(gu[:, :D_EXP]) * gu[:, D_EXP:]).astype(bf16)
        _store_token_major(o_ref, lax.dot_general(act, wd_buf[ws], dn, preferred_element_type=f32), r)

    @pl.when(g >= na)
    def _():
        o_ref[...] = jnp.zeros_like(o_ref)


def _experts_call(bexp, bend, nval, nact, slot_tok3, h2, w_gate_up, w_down):
    r = ROWS_PER_BLOCK
    nb = N_BLOCKS
    grid_spec = pltpu.PrefetchScalarGridSpec(
        num_scalar_prefetch=4,
        grid=(nb,),
        in_specs=[
            pl.BlockSpec((1, 1, r), lambda g, *_: (0, 0, 0), memory_space=pltpu.SMEM),
            pl.BlockSpec((1, 1, r), lambda g, *_: (jnp.minimum(g + 1, nb - 1), 0, 0), memory_space=pltpu.SMEM),
            pl.BlockSpec(memory_space=pl.ANY),
            pl.BlockSpec(memory_space=pl.ANY),
            pl.BlockSpec(memory_space=pl.ANY),
        ],
        out_specs=pl.BlockSpec((r * ROW_CHUNKS, LANES), lambda g, *_: (g, 0)),
        scratch_shapes=[
            pltpu.VMEM((2, r * ROW_CHUNKS, LANES), ROW_DT),
            pltpu.SemaphoreType.DMA((2,)),
            pltpu.VMEM((W_SLOTS, D, 2 * D_EXP), f32),
            pltpu.VMEM((W_SLOTS, D_EXP, D), f32),
            pltpu.SemaphoreType.DMA((W_SLOTS,)),
            pltpu.SMEM((1,), i32),
        ],
    )
    return pl.pallas_call(
        _experts_kernel,
        grid_spec=grid_spec,
        out_shape=jax.ShapeDtypeStruct((nb * r * ROW_CHUNKS, LANES), ROW_DT),
        compiler_params=_cparams(("arbitrary",)),
        name="experts",
    )(bexp, bend, nval, nact, slot_tok3, slot_tok3, h2, w_gate_up, w_down)


def _combine_kernel(d0_ref, dn_ref, ys_hbm, gate_ref, base_ref, g5_ref, nw_ref, o_ref, buf, sem, *, n_tiles):
    i = pl.program_id(0)
    tt = TT_COMB
    n_rows = TOP_K * tt
    slot = i % 2

    @pl.when(i == 0)
    def _():
        _gather_rows(d0_ref, ys_hbm, buf.at[0], sem.at[0], n_rows)

    @pl.when(i + 1 < n_tiles)
    def _():
        _gather_rows(dn_ref, ys_hbm, buf.at[1 - slot], sem.at[1 - slot], n_rows)

    _gather_rows_wait(ys_hbm, buf.at[slot], sem.at[slot], n_rows)
    g = gate_ref[...]
    gcols = [jnp.broadcast_to(g[:, k:k + 1], (tt, LANES)) for k in range(TOP_K)]
    parts = []
    for c in range(ROW_CHUNKS):
        a = gcols[0] * _load_token_major_chunk(buf, slot, c, tt).astype(f32)
        for k in range(1, TOP_K):
            a = a + gcols[k] * _load_token_major_chunk(buf, slot, c, tt, row0=k * tt * ROW_CHUNKS).astype(f32)
        parts.append(a)
    x2 = base_ref[...] + g5_ref[0, 0] * jnp.concatenate(parts, axis=1)
    o_ref[...] = x2 * lax.rsqrt(jnp.mean(x2 * x2, axis=-1, keepdims=True) + EPS) * nw_ref[...]


def _combine_call(dest3, ys, gates_tk, base, mod6, final_w):
    tt = TT_COMB
    n_tiles = N_TOK // tt
    rb = SEQ // tt
    return pl.pallas_call(
        functools.partial(_combine_kernel, n_tiles=n_tiles),
        grid=(n_tiles,),
        in_specs=[
            pl.BlockSpec((1, 1, TOP_K * tt), lambda i: (0, 0, 0), memory_space=pltpu.SMEM),
            pl.BlockSpec((1, 1, TOP_K * tt), lambda i: (jnp.minimum(i + 1, n_tiles - 1), 0, 0),
                         memory_space=pltpu.SMEM),
            pl.BlockSpec(memory_space=pl.ANY),
            pl.BlockSpec((tt, TOP_K), lambda i: (i, 0)),
            pl.BlockSpec((tt, D), lambda i: (i, 0)),
            pl.BlockSpec((1, 1, 1, D), lambda i: (i // rb, 5, 0, 0)),
            pl.BlockSpec((1, D), lambda i: (0, 0)),
        ],
        out_specs=pl.BlockSpec((tt, D), lambda i: (i, 0)),
        out_shape=jax.ShapeDtypeStruct((N_TOK, D), f32),
        scratch_shapes=[
            pltpu.VMEM((2, TOP_K * tt * ROW_CHUNKS, LANES), ROW_DT),
            pltpu.SemaphoreType.DMA((2,)),
        ],
        compiler_params=_cparams(("arbitrary",)),
        name="combine",
    )(dest3, dest3, ys, gates_tk, base, mod6, final_w)


def _tile_major(a, tt):
    n = a.shape[1] // tt
    return a.reshape(TOP_K, n, tt).transpose(1, 0, 2).reshape(n, 1, TOP_K * tt)


def kernel(x, c, ctx, c_ctx, w_mod, b_mod, norm1_w, w_in, conv_w, conv_b, dt_bias_f, dt_bias_b, a_log_f, a_log_b, d_skip, ssd_norm_w, cmlp_norm_w, w_spatial, b_spatial, w_out, norm2_w, w_router, router_bias, w_gate_up, w_down, w_shared_gate_up, w_shared_down, final_norm_w):
    assert x.shape == (NB_BATCH, SEQ, D) and ctx.shape == (NB_BATCH, CTX, D) and w_mod.shape[0] == 1

    cs = jnp.concatenate([c, c_ctx[None, :], jnp.zeros((SUBLANES - NB_BATCH - 1, D), f32)], axis=0)
    mod6 = _mod_call(cs, w_mod[0], b_mod).reshape(SUBLANES, N_MOD, 1, D)

    wi = w_in[0]
    w_main = jnp.concatenate([wi[:, OFF_Z:OFF_Z + SSD_W], wi[:, 0:OFF_Z], wi[:, OFF_DTB:OFF_U], wi[:, OFF_U:],
                              wi[:, OFF_Z + SSD_W:OFF_XBC]], axis=1).astype(bf16)
    w_dt = jnp.pad(wi[:, OFF_XBC:OFF_DTB], ((0, 0), (0, DT_W - 2 * SSD_H))).astype(bf16)
    pad_h = DT_W - SSD_H
    dtb = jnp.stack([jnp.pad(dt_bias_f[0], (0, pad_h)), jnp.pad(dt_bias_b[0], (0, pad_h))])[:, None, :]
    alog = jnp.stack([jnp.pad(a_log_f[0], (0, pad_h)), jnp.pad(a_log_b[0], (0, pad_h))])[:, None, :]
    e_rows = lax.broadcasted_iota(i32, (2 * LANES, SSD_W), 0) % LANES
    e_cols = lax.broadcasted_iota(i32, (2 * LANES, SSD_W), 1) // SSD_P
    e2 = (e_rows == e_cols).astype(bf16)
    t_rows = lax.broadcasted_iota(i32, (CHUNK, CHUNK), 0)
    t_cols = lax.broadcasted_iota(i32, (CHUNK, CHUNK), 1)
    tri2 = jnp.stack([t_rows >= t_cols, t_rows <= t_cols]).astype(bf16)
    ssd_consts = (conv_w[0][:, :SSD_W], conv_b[0][None, :SSD_W], conv_w[0][:, SSD_W:], conv_b[0][None, SSD_W:],
                  dtb, alog, e2, tri2)

    n1 = norm1_w[0][None, :]
    pm_c, dt_c = _inproj_call(ctx.reshape(NB_BATCH * CTX, D), mod6, n1, w_main, w_dt, lambda i: 2)
    x2d = x.reshape(N_TOK, D)
    pm_l, dt_l = _inproj_call(x2d, mod6, n1, w_main, w_dt, lambda i: i // (SEQ // TM_INPROJ))

    h0 = _ctx_state_call(pm_c, dt_c, ssd_consts)
    dsk = jnp.repeat(d_skip[0], SSD_P)[None, :]
    bs_e = jnp.repeat(jnp.swapaxes(b_spatial[0], 0, 1), CM_HD, axis=1)
    ycat = _ssd_main_call(pm_l, dt_l, ssd_consts, h0, dsk, ssd_norm_w[0][None, :], cmlp_norm_w[0][None, :],
                          w_spatial[0].astype(bf16), bs_e)

    base, h2, logits_t = _outproj_call(ycat, x2d, mod6, norm2_w[0][None, :], w_out[0].astype(bf16),
                                       jnp.swapaxes(w_router[0], 0, 1).astype(bf16),
                                       w_shared_gate_up[0].astype(bf16), w_shared_down[0].astype(bf16))
    h2 = h2.reshape(N_TOK, ROW_CHUNKS, LANES)

    dest, gates, bexp, bend, nval = _route_call(logits_t, router_bias[0][:, None])
    bend = bend[:, 0]
    nact = bend[N_EXP - 1:]
    dest3 = _tile_major(dest, TT_COMB)
    slot_tok = _slots_call(dest3)
    ys = _experts_call(bexp.reshape(N_BLOCKS), bend, nval.reshape(N_BLOCKS), nact,
                       slot_tok.reshape(N_BLOCKS, 1, ROWS_PER_BLOCK),
                       h2, w_gate_up[0], w_down[0]).reshape(N_BLOCKS * ROWS_PER_BLOCK, ROW_CHUNKS, LANES)
    out = _combine_call(dest3, ys, gates.T, base, mod6, final_norm_w[None, :])
    return out.reshape(NB_BATCH, SEQ, D)
```

```python
import functools

import jax
import jax.numpy as jnp
from jax import lax
from jax.experimental import pallas as pl
from jax.experimental.pallas import tpu as pltpu

f32 = jnp.float32
bf16 = jnp.bfloat16
i32 = jnp.int32

D = 2048
NB_BATCH = 2
SEQ = 4096
CTX = 256
N_MOD = 6
SSD_W = 1024
SSD_H = 16
SSD_P = 64
SSD_G = 2
D_STATE = 128
CHUNK = 128
D_CONV = 4
BC_W = 2 * SSD_G * D_STATE
CM_W = 1024
CM_H = 8
CM_HD = 128
OFF_Z = SSD_W
OFF_XBC = OFF_Z + SSD_W + BC_W
OFF_DTF = OFF_XBC + SSD_H
OFF_DTB = OFF_DTF + SSD_H
OFF_U = OFF_DTB + CM_W
IN_PROJ_DIM = OFF_U + CM_W
N_EXP = 256
TOP_K = 8
N_GRP = 8
TOP_GRP = 4
GRP_SZ = N_EXP // N_GRP
D_EXP = 512
ROUTED_SCALE = 2.5
EPS = 1e-6

P_MAIN = 4 * 1024 + BC_W
DT_W = 128

LANES = 128
SUBLANES = 8
VMEM_LIMIT = 56 * 1024 * 1024

TM_INPROJ = 1024
TN_INPROJ = 1536
TM_OUT = 256
TM_SHARED = 512
TT_ROUTE = 512
ROWS_PER_BLOCK = 128
N_TOK = NB_BATCH * SEQ
N_BLOCKS = N_TOK * TOP_K // ROWS_PER_BLOCK + N_EXP
TT_COMB = 128


def _cparams(sem):
    return pltpu.CompilerParams(dimension_semantics=sem, vmem_limit_bytes=VMEM_LIMIT)


def _sigmoid(x):
    return 0.5 * jnp.tanh(0.5 * x) + 0.5


def _silu(x):
    return x * _sigmoid(x)


def _dot(a, b):
    return jnp.dot(a, b, preferred_element_type=f32)


def _mod_kernel(c_ref, w_ref, b_ref, o_ref):
    a = _silu(c_ref[...]).astype(bf16)
    o_ref[...] = _dot(a, w_ref[...].astype(bf16)) + b_ref[...]


def _mod_call(cs, w_mod, b_mod):
    tn = 1024
    n = w_mod.shape[1]
    return pl.pallas_call(
        _mod_kernel,
        grid=(n // tn,),
        in_specs=[
            pl.BlockSpec((SUBLANES, D), lambda j: (0, 0)),
            pl.BlockSpec((D, tn), lambda j: (0, j)),
            pl.BlockSpec((1, tn), lambda j: (0, j)),
        ],
        out_specs=pl.BlockSpec((SUBLANES, tn), lambda j: (0, j)),
        out_shape=jax.ShapeDtypeStruct((SUBLANES, n), f32),
        compiler_params=_cparams(("arbitrary",)),
        name="mod",
    )(cs, w_mod, b_mod)


def _inproj_kernel(x_ref, sh_ref, sc_ref, nw_ref, w_ref, wdt_ref, p_ref, dt_ref, h_scr):
    @pl.when(pl.program_id(1) == 0)
    def _():
        x = x_ref[...]
        y = x * lax.rsqrt(jnp.mean(x * x, axis=-1, keepdims=True) + EPS) * nw_ref[...]
        h = (y * (1.0 + sc_ref[0, 0]) + sh_ref[0, 0]).astype(bf16)
        h_scr[...] = h
        dt_ref[...] = _dot(h, wdt_ref[...])

    p_ref[...] = _dot(h_scr[...], w_ref[...])


def _inproj_call(x2d, mod6, norm_w, w_main, w_dt, mod_row_fn):
    m = x2d.shape[0]
    tm, tn = min(TM_INPROJ, m), TN_INPROJ
    return pl.pallas_call(
        _inproj_kernel,
        grid=(m // tm, P_MAIN // tn),
        in_specs=[
            pl.BlockSpec((tm, D), lambda i, j: (i, 0)),
            pl.BlockSpec((1, 1, 1, D), lambda i, j: (mod_row_fn(i), 0, 0, 0)),
            pl.BlockSpec((1, 1, 1, D), lambda i, j: (mod_row_fn(i), 1, 0, 0)),
            pl.BlockSpec((1, D), lambda i, j: (0, 0)),
            pl.BlockSpec((D, tn), lambda i, j: (0, j)),
            pl.BlockSpec((D, DT_W), lambda i, j: (0, 0)),
        ],
        out_specs=[
            pl.BlockSpec((tm, tn), lambda i, j: (i, j)),
            pl.BlockSpec((tm, DT_W), lambda i, j: (i, 0)),
        ],
        out_shape=[
            jax.ShapeDtypeStruct((m, P_MAIN), f32),
            jax.ShapeDtypeStruct((m, DT_W), f32),
        ],
        scratch_shapes=[pltpu.VMEM((tm, D), bf16)],
        compiler_params=_cparams(("arbitrary", "arbitrary")),
        name="inproj",
    )(x2d, mod6, mod6, norm_w, w_main, w_dt)


def _split3(q):
    hi = q.astype(bf16)
    r1 = q - hi.astype(f32)
    mid = r1.astype(bf16)
    lo = (r1 - mid.astype(f32)).astype(bf16)
    return hi, mid, lo


def _split2_cat(q):
    hi = q.astype(bf16)
    lo = (q - hi.astype(f32)).astype(bf16)
    return jnp.concatenate([hi, lo], axis=1)


def _conv_silu(buf, w_ref, b_ref):
    w = w_ref[...]
    y = (w[0:1] * buf[7:135, :] + w[1:2] * buf[8:136, :] + w[2:3] * buf[9:137, :]
         + w[3:4] * buf[10:138, :] + b_ref[...])
    return _silu(y)


def _ssd_chunk(ph, first, last, r0, xm_ref, xp_ref, xn_ref, bm_ref, bp_ref, bn_ref, dt_ref,
               cwx_ref, cbx_ref, cwb_ref, cbb_ref, dtb_ref, alog_ref, e2_ref, tri_ref,
               s_scr, xbuf, bcbuf, xs_cache, bc_cache, with_output):
    @pl.when(ph == 0)
    def _():
        zero8x = jnp.zeros((SUBLANES, SSD_W), f32)
        zero8b = jnp.zeros((SUBLANES, BC_W), f32)
        xbuf[0:8, :] = jnp.where(first, zero8x, xp_ref[...])
        xbuf[8:136, :] = xm_ref[...]
        xbuf[136:144, :] = jnp.where(last, zero8x, xn_ref[...])
        bcbuf[0:8, :] = jnp.where(first, zero8b, bp_ref[...])
        bcbuf[8:136, :] = bm_ref[...]
        bcbuf[136:144, :] = jnp.where(last, zero8b, bn_ref[...])
        xs_cache[pl.ds(r0, CHUNK), :] = _conv_silu(xbuf, cwx_ref, cbx_ref).astype(bf16)
        bc_cache[pl.ds(r0, CHUNK), :] = _conv_silu(bcbuf, cwb_ref, cbb_ref).astype(bf16)

    xs = xs_cache[pl.ds(r0, CHUNK), :].astype(f32)
    bcv = bc_cache[pl.ds(r0, CHUNK), :]

    dtr = dt_ref[...]
    dtr = jnp.where(ph == 0, dtr, pltpu.roll(dtr, LANES - SSD_H, axis=1))
    pre = dtr + dtb_ref[0]
    dt = jnp.maximum(pre, 0.0) + jnp.log(1.0 + jnp.exp(-jnp.abs(pre)))
    d_a = dt * (-jnp.exp(alog_ref[0]))

    tri = tri_ref[0]
    mask = tri > 0
    col = lax.broadcasted_iota(i32, (CHUNK, CHUNK), 1)
    hi, mid, lo = _split3(d_a)
    acs = _dot(tri, hi) + _dot(tri, mid) + _dot(tri, lo)
    tot = jnp.sum(d_a, axis=0, keepdims=True)
    e_a = jnp.exp(acs)
    dte = jnp.exp(tot - acs)
    cd = jnp.broadcast_to(jnp.exp(tot), (SUBLANES, LANES))
    ex = _dot(jnp.concatenate([_split2_cat(dt), _split2_cat(e_a), _split2_cat(dte), _split2_cat(cd)], axis=0),
              e2_ref[...])
    dt_e = ex[0:128]
    ea_e = ex[128:256]
    dte_e = ex[256:384]
    cd_e = ex[384:385]
    xdt = xs * dt_e

    y_parts = []
    if with_output:
        acs_t = acs.T
        lane_lo = col < SSD_P
        g_mats = []
        for g in range(SSD_G):
            b_g = bcv[:, g * D_STATE:(g + 1) * D_STATE]
            c_g = bcv[:, (SSD_G + g) * D_STATE:(SSD_G + g + 1) * D_STATE]
            g_mats.append(lax.dot_general(c_g, b_g, (((1,), (1,)), ((), ())), preferred_element_type=f32))
        for p in range(SSD_H // 2):
            g = (2 * p) // (SSD_H // SSD_G)
            xp = xdt[:, p * LANES:(p + 1) * LANES]
            acc = None
            for k in range(2):
                h = 2 * p + k
                seg = jnp.broadcast_to(acs[:, h:h + 1], (CHUNK, CHUNK)) - jnp.broadcast_to(acs_t[h:h + 1, :], (CHUNK, CHUNK))
                dec = jnp.where(mask, jnp.exp(jnp.minimum(seg, 0.0)), 0.0)
                m_h = (g_mats[g] * dec).astype(bf16)
                x_h = jnp.where(lane_lo if k == 0 else jnp.logical_not(lane_lo), xp, 0.0).astype(bf16)
                t = _dot(m_h, x_h)
                acc = t if acc is None else acc + t
            y_parts.append(acc)

    gw = SSD_W // SSD_G
    y_off = []
    for g in range(SSD_G):
        sl = slice(g * gw, (g + 1) * gw)
        b_g = bcv[:, g * D_STATE:(g + 1) * D_STATE]
        s_g = s_scr[g]
        if with_output:
            c_g = bcv[:, (SSD_G + g) * D_STATE:(SSD_G + g + 1) * D_STATE]
            y_off.append(_dot(c_g, s_g.astype(bf16)) * ea_e[:, sl])
        upd = lax.dot_general(b_g, (xdt[:, sl] * dte_e[:, sl]).astype(bf16), (((0,), (0,)), ((), ())),
                              preferred_element_type=f32)
        s_scr[g] = s_g * cd_e[:, sl] + upd
    if not with_output:
        return xs, None
    y = jnp.concatenate(y_parts, axis=1) + jnp.concatenate(y_off, axis=1)
    return xs, y


def _ctx_state_kernel(xm_ref, xp_ref, xn_ref, bm_ref, bp_ref, bn_ref, dt_ref,
                      cwx_ref, cbx_ref, cwb_ref, cbb_ref, dtb_ref, alog_ref, e2_ref, tri_ref,
                      h_ref, s_scr, xbuf, bcbuf, xs_cache, bc_cache, *, nc):
    ph = pl.program_id(1)
    c = pl.program_id(2)
    ci = jnp.where(ph == 0, c, nc - 1 - c)

    @pl.when(c == 0)
    def _():
        s_scr[...] = jnp.zeros_like(s_scr)

    _ssd_chunk(ph, ci == 0, ci == nc - 1, pl.multiple_of(ci * CHUNK, CHUNK),
               xm_ref, xp_ref, xn_ref, bm_ref, bp_ref, bn_ref, dt_ref,
               cwx_ref, cbx_ref, cwb_ref, cbb_ref, dtb_ref, alog_ref, e2_ref, tri_ref,
               s_scr, xbuf, bcbuf, xs_cache, bc_cache, False)

    @pl.when(c == nc - 1)
    def _():
        h_ref[0, 0] = s_scr[...]


def _gelu_tanh(x):
    return 0.5 * x * (1.0 + jnp.tanh(0.7978845608028654 * (x + 0.044715 * x * x * x)))


def _ssd_main_kernel(xm_ref, xp_ref, xn_ref, bm_ref, bp_ref, bn_ref, dt_ref,
                     cwx_ref, cbx_ref, cwb_ref, cbb_ref, dtb_ref, alog_ref, e2_ref, tri_ref,
                     h0_ref, z_ref, u_ref, v_ref, dsk_ref, snw_ref, cnw_ref, ws_ref, bs_ref,
                     o_ref, s_scr, xbuf, bcbuf, xs_cache, bc_cache, yf_scr, *, nc):
    ph = pl.program_id(1)
    c = pl.program_id(2)
    ci = jnp.where(ph == 0, c, nc - 1 - c)
    r0 = pl.multiple_of(ci * CHUNK, CHUNK)

    @pl.when(c == 0)
    def _():
        s_scr[...] = h0_ref[0, 0]

    xs, y = _ssd_chunk(ph, ci == 0, ci == nc - 1, r0, xm_ref, xp_ref, xn_ref, bm_ref, bp_ref, bn_ref, dt_ref,
                       cwx_ref, cbx_ref, cwb_ref, cbb_ref, dtb_ref, alog_ref, e2_ref, tri_ref,
                       s_scr, xbuf, bcbuf, xs_cache, bc_cache, True)

    @pl.when(ph == 0)
    def _():
        yf_scr[pl.ds(r0, CHUNK), :] = y

    @pl.when(ph == 1)
    def _():
        yt = (yf_scr[pl.ds(r0, CHUNK), :] + y + dsk_ref[...] * xs) * _silu(z_ref[...])
        yt = yt * lax.rsqrt(jnp.mean(yt * yt, axis=-1, keepdims=True) + EPS) * snw_ref[...]
        o_ref[:, 0:SSD_W] = yt.astype(bf16)
        u = _gelu_tanh(u_ref[...])
        v = _gelu_tanh(v_ref[...])
        vn = (v * lax.rsqrt(jnp.mean(v * v, axis=-1, keepdims=True) + EPS) * cnw_ref[...]).astype(bf16)
        mixed = [_dot(ws_ref[h], vn[:, h * CM_HD:(h + 1) * CM_HD]) for h in range(CM_H)]
        o_ref[:, SSD_W:SSD_W + CM_W] = (u * (jnp.concatenate(mixed, axis=1) + bs_ref[...])).astype(bf16)


def _ssd_common_specs(nc, row_blocks_total):
    cpb = nc
    sub = CHUNK // SUBLANES

    def dt_i(b, ph, c):
        return b * cpb + jnp.where(ph == 0, c, nc - 1 - c)

    def main_i(b, ph, c):
        return b * cpb + jnp.where(ph == 0, c, nc - 1)

    def prev_i(b, ph, c):
        return jnp.maximum(main_i(b, ph, c) * sub - 1, 0)

    def next_i(b, ph, c):
        return jnp.minimum(main_i(b, ph, c) * sub + sub, row_blocks_total * sub - 1)

    bc_col = (4 * 1024) // BC_W
    specs = [
        pl.BlockSpec((CHUNK, SSD_W), lambda b, ph, c: (main_i(b, ph, c), 0)),
        pl.BlockSpec((SUBLANES, SSD_W), lambda b, ph, c: (prev_i(b, ph, c), 0)),
        pl.BlockSpec((SUBLANES, SSD_W), lambda b, ph, c: (next_i(b, ph, c), 0)),
        pl.BlockSpec((CHUNK, BC_W), lambda b, ph, c: (main_i(b, ph, c), bc_col)),
        pl.BlockSpec((SUBLANES, BC_W), lambda b, ph, c: (prev_i(b, ph, c), bc_col)),
        pl.BlockSpec((SUBLANES, BC_W), lambda b, ph, c: (next_i(b, ph, c), bc_col)),
        pl.BlockSpec((CHUNK, DT_W), lambda b, ph, c: (dt_i(b, ph, c), 0)),
        pl.BlockSpec((D_CONV, SSD_W), lambda b, ph, c: (0, 0)),
        pl.BlockSpec((1, SSD_W), lambda b, ph, c: (0, 0)),
        pl.BlockSpec((D_CONV, BC_W), lambda b, ph, c: (0, 0)),
        pl.BlockSpec((1, BC_W), lambda b, ph, c: (0, 0)),
        pl.BlockSpec((1, 1, DT_W), lambda b, ph, c: (ph, 0, 0)),
        pl.BlockSpec((1, 1, DT_W), lambda b, ph, c: (ph, 0, 0)),
        pl.BlockSpec((2 * LANES, SSD_W), lambda b, ph, c: (0, 0)),
        pl.BlockSpec((1, CHUNK, CHUNK), lambda b, ph, c: (ph, 0, 0)),
    ]
    return specs


def _ssd_scratch(seq_rows):
    return [
        pltpu.VMEM((SSD_G, D_STATE, SSD_W // SSD_G), f32),
        pltpu.VMEM((CHUNK + 2 * SUBLANES, SSD_W), f32),
        pltpu.VMEM((CHUNK + 2 * SUBLANES, BC_W), f32),
        pltpu.VMEM((seq_rows, SSD_W), bf16),
        pltpu.VMEM((seq_rows, BC_W), bf16),
    ]


def _ctx_state_call(pm, dtm, ssd_consts):
    nc = CTX // CHUNK
    specs = _ssd_common_specs(nc, NB_BATCH * nc)
    return pl.pallas_call(
        functools.partial(_ctx_state_kernel, nc=nc),
        grid=(NB_BATCH, 2, nc),
        in_specs=specs,
        out_specs=pl.BlockSpec((1, 1, SSD_G, D_STATE, SSD_W // SSD_G), lambda b, ph, c: (b, ph, 0, 0, 0)),
        out_shape=jax.ShapeDtypeStruct((NB_BATCH, 2, SSD_G, D_STATE, SSD_W // SSD_G), f32),
        scratch_shapes=_ssd_scratch(CTX),
        compiler_params=_cparams(("arbitrary", "arbitrary", "arbitrary")),
        name="ssd_ctx",
    )(pm, pm, pm, pm, pm, pm, dtm, *ssd_consts)


def _ssd_main_call(pm, dtm, ssd_consts, h0, dsk, snw, cnw, ws, bs_e):
    nc = SEQ // CHUNK
    specs = _ssd_common_specs(nc, NB_BATCH * nc)

    def zrow(b, ph, c):
        return b * nc + jnp.where(ph == 0, nc - 1, nc - 1 - c)

    specs += [
        pl.BlockSpec((1, 1, SSD_G, D_STATE, SSD_W // SSD_G), lambda b, ph, c: (b, ph, 0, 0, 0)),
        pl.BlockSpec((CHUNK, SSD_W), lambda b, ph, c: (zrow(b, ph, c), 1)),
        pl.BlockSpec((CHUNK, CM_W), lambda b, ph, c: (zrow(b, ph, c), 2)),
        pl.BlockSpec((CHUNK, CM_W), lambda b, ph, c: (zrow(b, ph, c), 3)),
        pl.BlockSpec((1, SSD_W), lambda b, ph, c: (0, 0)),
        pl.BlockSpec((1, SSD_W), lambda b, ph, c: (0, 0)),
        pl.BlockSpec((1, CM_W), lambda b, ph, c: (0, 0)),
        pl.BlockSpec((CM_H, CHUNK, CHUNK), lambda b, ph, c: (0, 0, 0)),
        pl.BlockSpec((CHUNK, CM_W), lambda b, ph, c: (0, 0)),
    ]
    return pl.pallas_call(
        functools.partial(_ssd_main_kernel, nc=nc),
        grid=(NB_BATCH, 2, nc),
        in_specs=specs,
        out_specs=pl.BlockSpec((CHUNK, SSD_W + CM_W), lambda b, ph, c: (zrow(b, ph, c), 0)),
        out_shape=jax.ShapeDtypeStruct((N_TOK, SSD_W + CM_W), bf16),
        scratch_shapes=_ssd_scratch(SEQ) + [pltpu.VMEM((SEQ, SSD_W), f32)],
        compiler_params=_cparams(("arbitrary", "arbitrary", "arbitrary")),
        name="ssd_main",
    )(pm, pm, pm, pm, pm, pm, dtm, *ssd_consts, h0, pm, pm, pm, dsk, snw, cnw, ws, bs_e)


ROW_CHUNKS = D // LANES
ROW_DT = f32


def _store_token_major(ref, val, n_rows, row0=0):
    for c in range(ROW_CHUNKS):
        ref[pl.ds(row0 + c, n_rows, stride=ROW_CHUNKS), :] = val[:, c * LANES:(c + 1) * LANES].astype(ref.dtype)


def _load_token_major_chunk(ref, lead, c, n_rows, row0=0):
    idx = pl.ds(row0 + c, n_rows, stride=ROW_CHUNKS)
    return ref[idx, :] if lead is None else ref[lead, idx, :]


def _outproj_kernel(y_ref, x_ref, g_ref, sh_ref, sc_ref, g5_ref, nw_ref, wo_ref, wr_ref, w1_ref, w2_ref,
                    base_ref, h_ref, lg_ref):
    x1 = x_ref[...] + g_ref[0, 0] * _dot(y_ref[...], wo_ref[...])
    y = x1 * lax.rsqrt(jnp.mean(x1 * x1, axis=-1, keepdims=True) + EPS) * nw_ref[...]
    h = y * (1.0 + sc_ref[0, 0]) + sh_ref[0, 0]
    _store_token_major(h_ref, h, h.shape[0])
    hb = h.astype(bf16)
    lg_ref[...] = lax.dot_general(wr_ref[...], hb, (((1,), (1,)), ((), ())), preferred_element_type=f32)
    gu = _dot(hb, w1_ref[...])
    act = (_silu(gu[:, :D_EXP]) * gu[:, D_EXP:]).astype(bf16)
    base_ref[...] = x1 + g5_ref[0, 0] * _dot(act, w2_ref[...])


def _outproj_call(ycat, x2d, mod6, norm_w, w_out, w_router_t, w1, w2):
    tm = TM_OUT
    rb = SEQ // tm
    const = lambda i: (0, 0)
    return pl.pallas_call(
        _outproj_kernel,
        grid=(N_TOK // tm,),
        in_specs=[
            pl.BlockSpec((tm, D), lambda i: (i, 0)),
            pl.BlockSpec((tm, D), lambda i: (i, 0)),
            pl.BlockSpec((1, 1, 1, D), lambda i: (i // rb, 2, 0, 0)),
            pl.BlockSpec((1, 1, 1, D), lambda i: (i // rb, 3, 0, 0)),
            pl.BlockSpec((1, 1, 1, D), lambda i: (i // rb, 4, 0, 0)),
            pl.BlockSpec((1, 1, 1, D), lambda i: (i // rb, 5, 0, 0)),
            pl.BlockSpec((1, D), const),
            pl.BlockSpec((D, D), const),
            pl.BlockSpec((N_EXP, D), const),
            pl.BlockSpec((D, 2 * D_EXP), const),
            pl.BlockSpec((D_EXP, D), const),
        ],
        out_specs=[
            pl.BlockSpec((tm, D), lambda i: (i, 0)),
            pl.BlockSpec((tm * ROW_CHUNKS, LANES), lambda i: (i, 0)),
            pl.BlockSpec((N_EXP, tm), lambda i: (0, i)),
        ],
        out_shape=[
            jax.ShapeDtypeStruct((N_TOK, D), f32),
            jax.ShapeDtypeStruct((N_TOK * ROW_CHUNKS, LANES), ROW_DT),
            jax.ShapeDtypeStruct((N_EXP, N_TOK), f32),
        ],
        compiler_params=_cparams(("arbitrary",)),
        name="outproj",
    )(ycat, x2d, mod6, mod6, mod6, mod6, norm_w, w_out, w_router_t, w1, w2)


def _route_kernel(lg_ref, bias_ref, dest_ref, gate_ref, bexp_ref, bend_ref, nval_ref,
                  idx_scr, pos_scr, cnt_scr, *, n_tiles):
    j = pl.program_id(0)
    tt = TT_ROUTE
    neg = -jnp.inf

    @pl.when(j == 0)
    def _():
        cnt_scr[...] = jnp.zeros_like(cnt_scr)

    s = _sigmoid(lg_ref[...])
    biased = s + bias_ref[...]
    v3 = biased.reshape(N_GRP, GRP_SZ, tt)
    io3 = lax.broadcasted_iota(i32, (N_GRP, GRP_SZ, tt), 1)
    m1 = jnp.max(v3, axis=1, keepdims=True)
    i1 = jnp.min(jnp.where(v3 == m1, io3, GRP_SZ), axis=1, keepdims=True)
    m2 = jnp.max(jnp.where(io3 == i1, neg, v3), axis=1, keepdims=True)
    gs = m1 + m2
    gio = lax.broadcasted_iota(i32, (N_GRP, 1, tt), 0)
    sel = jnp.zeros((N_GRP, 1, tt), dtype=jnp.bool_)
    cur_g = gs
    for _ in range(TOP_GRP):
        m = jnp.max(cur_g, axis=0, keepdims=True)
        gi = jnp.min(jnp.where(cur_g == m, gio, N_GRP), axis=0, keepdims=True)
        hit = gio == gi
        sel = jnp.logical_or(sel, hit)
        cur_g = jnp.where(hit, neg, cur_g)
    cur = jnp.where(sel, v3, neg).reshape(N_EXP, tt)

    eio = lax.broadcasted_iota(i32, (N_EXP, tt), 0)
    onehot = jnp.zeros((N_EXP, tt), f32)
    idxs, gts = [], []
    for _ in range(TOP_K):
        m = jnp.max(cur, axis=0, keepdims=True)
        ei = jnp.min(jnp.where(cur == m, eio, N_EXP), axis=0, keepdims=True)
        hit = eio == ei
        gts.append(jnp.sum(jnp.where(hit, s, 0.0), axis=0, keepdims=True))
        idxs.append(ei)
        onehot = jnp.where(hit, 1.0, onehot)
        cur = jnp.where(hit, neg, cur)
    gsum = gts[0]
    for k in range(1, TOP_K):
        gsum = gsum + gts[k]

    r_io = lax.broadcasted_iota(i32, (tt, tt), 0)
    c_io = lax.broadcasted_iota(i32, (tt, tt), 1)
    upper = (r_io <= c_io).astype(f32).astype(bf16)
    oh = onehot.astype(bf16)
    incl = _dot(oh, upper)
    base = cnt_scr[...]
    pos = incl - 1.0 + jnp.concatenate([base] * (tt // LANES), axis=1)
    cnt_scr[...] = base + _dot(oh, jnp.ones((tt, LANES), bf16))
    c0 = pl.multiple_of(j * tt, tt)
    for k in range(TOP_K):
        hit = eio == idxs[k]
        idx_scr[k:k + 1, pl.ds(c0, tt)] = idxs[k]
        pos_scr[k:k + 1, pl.ds(c0, tt)] = jnp.sum(jnp.where(hit, pos, 0.0), axis=0, keepdims=True)
        gate_ref[k:k + 1, pl.ds(c0, tt)] = gts[k] / gsum * ROUTED_SCALE

    @pl.when(j == n_tiles - 1)
    def _():
        cnt = cnt_scr[...]
        r = float(ROWS_PER_BLOCK)
        nblk = jnp.floor((cnt + (r - 1.0)) * (1.0 / r))
        er = lax.broadcasted_iota(i32, (N_EXP, N_EXP), 0)
        ec = lax.broadcasted_iota(i32, (N_EXP, N_EXP), 1)
        lower = (er >= ec).astype(f32).astype(bf16)
        bend = _dot(lower, nblk.astype(bf16))
        pstart = (bend - nblk) * r
        pstart_t = jnp.concatenate([pstart] * (tt // LANES), axis=1)
        def slots(t, carry):
            c1 = pl.multiple_of(t * tt, tt)
            for k in range(TOP_K):
                hit = eio == idx_scr[k:k + 1, pl.ds(c1, tt)]
                d = jnp.sum(jnp.where(hit, pstart_t, 0.0), axis=0, keepdims=True) + pos_scr[k:k + 1, pl.ds(c1, tt)]
                dest_ref[k:k + 1, pl.ds(c1, tt)] = d.astype(i32)
            return carry

        lax.fori_loop(0, n_tiles, slots, 0)
        bio = lax.broadcasted_iota(i32, (N_EXP, N_BLOCKS), 1).astype(f32)
        bend_t = jnp.concatenate([bend] * (N_BLOCKS // LANES), axis=1)
        be = jnp.sum(jnp.where(bend_t <= bio, 1.0, 0.0), axis=0, keepdims=True)
        bexp_ref[...] = jnp.minimum(be, float(N_EXP - 1)).astype(i32)
        bend_ref[...] = bend.astype(i32)
        cnt_t = jnp.concatenate([cnt] * (N_BLOCKS // LANES), axis=1)
        pst_b = jnp.concatenate([pstart] * (N_BLOCKS // LANES), axis=1)
        lo = jnp.maximum(pst_b, bio * r)
        hi = jnp.minimum(pst_b + cnt_t, (bio + 1.0) * r)
        nval_ref[...] = jnp.sum(jnp.maximum(hi - lo, 0.0), axis=0, keepdims=True).astype(i32)


def _route_call(logits_t, bias_col):
    tt = TT_ROUTE
    n_tiles = N_TOK // tt
    return pl.pallas_call(
        functools.partial(_route_kernel, n_tiles=n_tiles),
        grid=(n_tiles,),
        in_specs=[
            pl.BlockSpec((N_EXP, tt), lambda j: (0, j)),
            pl.BlockSpec((N_EXP, 1), lambda j: (0, 0)),
        ],
        out_specs=[
            pl.BlockSpec((TOP_K, N_TOK), lambda j: (0, 0)),
            pl.BlockSpec((TOP_K, N_TOK), lambda j: (0, 0)),
            pl.BlockSpec((1, N_BLOCKS), lambda j: (0, 0)),
            pl.BlockSpec((N_EXP, LANES), lambda j: (0, 0)),
            pl.BlockSpec((1, N_BLOCKS), lambda j: (0, 0)),
        ],
        out_shape=[
            jax.ShapeDtypeStruct((TOP_K, N_TOK), i32),
            jax.ShapeDtypeStruct((TOP_K, N_TOK), f32),
            jax.ShapeDtypeStruct((1, N_BLOCKS), i32),
            jax.ShapeDtypeStruct((N_EXP, LANES), i32),
            jax.ShapeDtypeStruct((1, N_BLOCKS), i32),
        ],
        scratch_shapes=[
            pltpu.VMEM((TOP_K, N_TOK), i32),
            pltpu.VMEM((TOP_K, N_TOK), f32),
            pltpu.VMEM((N_EXP, LANES), f32),
        ],
        compiler_params=_cparams(("arbitrary",)),
        name="route",
    )(logits_t, bias_col)


def _slots_kernel(dest_ref, tab_ref):
    i = pl.program_id(0)
    tt = TT_COMB

    @pl.when(i == 0)
    def _():
        def zero_row(b, c):
            b0 = b * ROWS_PER_BLOCK
            for l in range(ROWS_PER_BLOCK):
                tab_ref[b0 + l] = 0
            return c
        lax.fori_loop(0, N_BLOCKS, zero_row, 0)

    base = i * tt
    for rr in range(tt):
        for k in range(TOP_K):
            tab_ref[dest_ref[0, 0, k * tt + rr]] = base + rr


def _slots_call(dest3):
    tt = TT_COMB
    return pl.pallas_call(
        _slots_kernel,
        grid=(N_TOK // tt,),
        in_specs=[pl.BlockSpec((1, 1, TOP_K * tt), lambda i: (i, 0, 0), memory_space=pltpu.SMEM)],
        out_specs=pl.BlockSpec(memory_space=pltpu.SMEM),
        out_shape=jax.ShapeDtypeStruct((N_BLOCKS * ROWS_PER_BLOCK,), i32),
        compiler_params=_cparams(("arbitrary",)),
        name="slots",
    )(dest3)


GATHER_GROUP = 16


def _gather_rows(idx_ref, src_hbm, dst, sem, n_rows, n_valid=None):
    def issue(lo, hi):
        for r in range(lo, hi):
            t = idx_ref[0, 0, r]
            pltpu.make_async_copy(src_hbm.at[t], dst.at[pl.ds(r * ROW_CHUNKS, ROW_CHUNKS), :], sem).start(
                priority=r % 2)

    if n_valid is None:
        issue(0, n_rows)
        return
    for lo in range(0, n_rows, GATHER_GROUP):
        pl.when(lo < n_valid)(functools.partial(issue, lo, lo + GATHER_GROUP))


def _gather_rows_wait(src_hbm, dst, sem, n_rows, n_valid=None):
    def body(r, carry):
        pltpu.make_async_copy(src_hbm.at[0], dst.at[pl.ds(0, ROW_CHUNKS), :], sem).wait()
        return carry
    if n_valid is not None:
        n_rows = ((n_valid + (GATHER_GROUP - 1)) // GATHER_GROUP) * GATHER_GROUP
    lax.fori_loop(0, n_rows, body, 0, unroll=8 if n_valid is None else 1)


W_CHUNKS = 4
W_SLOTS = 3


def _experts_kernel(bexp_ref, bend_ref, nval_ref, nact_ref, tok0_ref, tokn_ref, h_hbm, wgu_hbm, wd_hbm, o_ref,
                    xbuf, xsem, wgu_buf, wd_buf, wsem, ord_scr):
    g = pl.program_id(0)
    na = nact_ref[0]
    e = bexp_ref[g]
    r = ROWS_PER_BLOCK
    xslot = g % 2

    def fetch(ex, slot):
        cps = []
        for q in range(W_CHUNKS):
            a = D // W_CHUNKS
            cps.append(pltpu.make_async_copy(wgu_hbm.at[ex, pl.ds(q * a, a), :],
                                             wgu_buf.at[slot, pl.ds(q * a, a), :], wsem.at[slot]))
            a = D_EXP // W_CHUNKS
            cps.append(pltpu.make_async_copy(wd_hbm.at[ex, pl.ds(q * a, a), :],
                                             wd_buf.at[slot, pl.ds(q * a, a), :], wsem.at[slot]))
        return cps

    def start_fetch(ex, slot):
        for n, cp in enumerate(fetch(ex, slot)):
            cp.start(priority=(n // 2) % 2)

    def next_expert_block(ex):
        return bend_ref[ex]

    def expert_of(blk):
        return bexp_ref[jnp.minimum(blk, N_BLOCKS - 1)]

    @pl.when(g == 0)
    def _():
        ord_scr[0] = 0
        start_fetch(e, 0)
        n1 = next_expert_block(e)

        @pl.when(n1 < na)
        def _():
            start_fetch(expert_of(n1), 1)

        xbuf[...] = jnp.zeros_like(xbuf)
        _gather_rows(tok0_ref, h_hbm, xbuf.at[0], xsem.at[0], r, n_valid=nval_ref[0])

    @pl.when(g + 1 < na)
    def _():
        _gather_rows(tokn_ref, h_hbm, xbuf.at[1 - xslot], xsem.at[1 - xslot], r,
                     n_valid=nval_ref[jnp.minimum(g + 1, N_BLOCKS - 1)])

    first = jnp.logical_or(g == 0, e != bexp_ref[jnp.maximum(g - 1, 0)])

    @pl.when(jnp.logical_and(g < na, first))
    def _():
        k = ord_scr[0]
        for cp in fetch(e, k % W_SLOTS):
            cp.wait()
        n1 = next_expert_block(e)

        @pl.when(n1 < na)
        def _():
            n2 = next_expert_block(expert_of(n1))

            @pl.when(n2 < na)
            def _():
                start_fetch(expert_of(n2), (k + 2) % W_SLOTS)

        ord_scr[0] = k + 1

    @pl.when(g < na)
    def _():
        _gather_rows_wait(h_hbm, xbuf.at[xslot], xsem.at[xslot], r, n_valid=nval_ref[g])
        ws = (ord_scr[0] + W_SLOTS - 1) % W_SLOTS
        dn = (((1,), (0,)), ((), ()))
        gu = None
        for p in range(ROW_CHUNKS // 2):
            xp = jnp.concatenate([_load_token_major_chunk(xbuf, xslot, 2 * p, r),
                                  _load_token_major_chunk(xbuf, xslot, 2 * p + 1, r)], axis=1).astype(bf16)
            t = lax.dot_general(xp, wgu_buf[ws, 2 * p * LANES:(2 * p + 2) * LANES, :], dn, preferred_element_type=f32)
            gu = t if gu is None else gu + t
        act = (_silu(gu[:, :D_EXP]) * gu[:, D_EXP:]).astype(bf16)
        _store_token_major(o_ref, lax.dot_general(act, wd_buf[ws], dn, preferred_element_type=f32), r)

    @pl.when(g >= na)
    def _():
        o_ref[...] = jnp.zeros_like(o_ref)


def _experts_call(bexp, bend, nval, nact, slot_tok3, h2, w_gate_up, w_down):
    r = ROWS_PER_BLOCK
    nb = N_BLOCKS
    grid_spec = pltpu.PrefetchScalarGridSpec(
        num_scalar_prefetch=4,
        grid=(nb,),
        in_specs=[
            pl.BlockSpec((1, 1, r), lambda g, *_: (0, 0, 0), memory_space=pltpu.SMEM),
            pl.BlockSpec((1, 1, r), lambda g, *_: (jnp.minimum(g + 1, nb - 1), 0, 0), memory_space=pltpu.SMEM),
            pl.BlockSpec(memory_space=pl.ANY),
            pl.BlockSpec(memory_space=pl.ANY),
            pl.BlockSpec(memory_space=pl.ANY),
        ],
        out_specs=pl.BlockSpec((r * ROW_CHUNKS, LANES), lambda g, *_: (g, 0)),
        scratch_shapes=[
            pltpu.VMEM((2, r * ROW_CHUNKS, LANES), ROW_DT),
            pltpu.SemaphoreType.DMA((2,)),
            pltpu.VMEM((W_SLOTS, D, 2 * D_EXP), f32),
            pltpu.VMEM((W_SLOTS, D_EXP, D), f32),
            pltpu.SemaphoreType.DMA((W_SLOTS,)),
            pltpu.SMEM((1,), i32),
        ],
    )
    return pl.pallas_call(
        _experts_kernel,
        grid_spec=grid_spec,
        out_shape=jax.ShapeDtypeStruct((nb * r * ROW_CHUNKS, LANES), ROW_DT),
        compiler_params=_cparams(("arbitrary",)),
        name="experts",
    )(bexp, bend, nval, nact, slot_tok3, slot_tok3, h2, w_gate_up, w_down)


def _combine_kernel(d0_ref, dn_ref, ys_hbm, gate_ref, base_ref, g5_ref, nw_ref, o_ref, buf, sem, *, n_tiles):
    i = pl.program_id(0)
    tt = TT_COMB
    n_rows = TOP_K * tt
    slot = i % 2

    @pl.when(i == 0)
    def _():
        _gather_rows(d0_ref, ys_hbm, buf.at[0], sem.at[0], n_rows)

    @pl.when(i + 1 < n_tiles)
    def _():
        _gather_rows(dn_ref, ys_hbm, buf.at[1 - slot], sem.at[1 - slot], n_rows)

    _gather_rows_wait(ys_hbm, buf.at[slot], sem.at[slot], n_rows)
    g = gate_ref[...]
    gcols = [jnp.broadcast_to(g[:, k:k + 1], (tt, LANES)) for k in range(TOP_K)]
    parts = []
    for c in range(ROW_CHUNKS):
        a = gcols[0] * _load_token_major_chunk(buf, slot, c, tt).astype(f32)
        for k in range(1, TOP_K):
            a = a + gcols[k] * _load_token_major_chunk(buf, slot, c, tt, row0=k * tt * ROW_CHUNKS).astype(f32)
        parts.append(a)
    x2 = base_ref[...] + g5_ref[0, 0] * jnp.concatenate(parts, axis=1)
    o_ref[...] = x2 * lax.rsqrt(jnp.mean(x2 * x2, axis=-1, keepdims=True) + EPS) * nw_ref[...]


def _combine_call(dest3, ys, gates_tk, base, mod6, final_w):
    tt = TT_COMB
    n_tiles = N_TOK // tt
    rb = SEQ // tt
    return pl.pallas_call(
        functools.partial(_combine_kernel, n_tiles=n_tiles),
        grid=(n_tiles,),
        in_specs=[
            pl.BlockSpec((1, 1, TOP_K * tt), lambda i: (0, 0, 0), memory_space=pltpu.SMEM),
            pl.BlockSpec((1, 1, TOP_K * tt), lambda i: (jnp.minimum(i + 1, n_tiles - 1), 0, 0),
                         memory_space=pltpu.SMEM),
            pl.BlockSpec(memory_space=pl.ANY),
            pl.BlockSpec((tt, TOP_K), lambda i: (i, 0)),
            pl.BlockSpec((tt, D), lambda i: (i, 0)),
            pl.BlockSpec((1, 1, 1, D), lambda i: (i // rb, 5, 0, 0)),
            pl.BlockSpec((1, D), lambda i: (0, 0)),
        ],
        out_specs=pl.BlockSpec((tt, D), lambda i: (i, 0)),
        out_shape=jax.ShapeDtypeStruct((N_TOK, D), f32),
        scratch_shapes=[
            pltpu.VMEM((2, TOP_K * tt * ROW_CHUNKS, LANES), ROW_DT),
            pltpu.SemaphoreType.DMA((2,)),
        ],
        compiler_params=_cparams(("arbitrary",)),
        name="combine",
    )(dest3, dest3, ys, gates_tk, base, mod6, final_w)


def _tile_major(a, tt):
    n = a.shape[1] // tt
    return a.reshape(TOP_K, n, tt).transpose(1, 0, 2).reshape(n, 1, TOP_K * tt)


def kernel(x, c, ctx, c_ctx, w_mod, b_mod, norm1_w, w_in, conv_w, conv_b, dt_bias_f, dt_bias_b, a_log_f, a_log_b, d_skip, ssd_norm_w, cmlp_norm_w, w_spatial, b_spatial, w_out, norm2_w, w_router, router_bias, w_gate_up, w_down, w_shared_gate_up, w_shared_down, final_norm_w):
    assert x.shape == (NB_BATCH, SEQ, D) and ctx.shape == (NB_BATCH, CTX, D) and w_mod.shape[0] == 1

    cs = jnp.concatenate([c, c_ctx[None, :], jnp.zeros((SUBLANES - NB_BATCH - 1, D), f32)], axis=0)
    mod6 = _mod_call(cs, w_mod[0], b_mod).reshape(SUBLANES, N_MOD, 1, D)

    wi = w_in[0]
    w_main = jnp.concatenate([wi[:, OFF_Z:OFF_Z + SSD_W], wi[:, 0:OFF_Z], wi[:, OFF_DTB:OFF_U], wi[:, OFF_U:],
                              wi[:, OFF_Z + SSD_W:OFF_XBC]], axis=1).astype(bf16)
    w_dt = jnp.pad(wi[:, OFF_XBC:OFF_DTB], ((0, 0), (0, DT_W - 2 * SSD_H))).astype(bf16)
    pad_h = DT_W - SSD_H
    dtb = jnp.stack([jnp.pad(dt_bias_f[0], (0, pad_h)), jnp.pad(dt_bias_b[0], (0, pad_h))])[:, None, :]
    alog = jnp.stack([jnp.pad(a_log_f[0], (0, pad_h)), jnp.pad(a_log_b[0], (0, pad_h))])[:, None, :]
    e_rows = lax.broadcasted_iota(i32, (2 * LANES, SSD_W), 0) % LANES
    e_cols = lax.broadcasted_iota(i32, (2 * LANES, SSD_W), 1) // SSD_P
    e2 = (e_rows == e_cols).astype(bf16)
    t_rows = lax.broadcasted_iota(i32, (CHUNK, CHUNK), 0)
    t_cols = lax.broadcasted_iota(i32, (CHUNK, CHUNK), 1)
    tri2 = jnp.stack([t_rows >= t_cols, t_rows <= t_cols]).astype(bf16)
    ssd_consts = (conv_w[0][:, :SSD_W], conv_b[0][None, :SSD_W], conv_w[0][:, SSD_W:], conv_b[0][None, SSD_W:],
                  dtb, alog, e2, tri2)

    n1 = norm1_w[0][None, :]
    pm_c, dt_c = _inproj_call(ctx.reshape(NB_BATCH * CTX, D), mod6, n1, w_main, w_dt, lambda i: 2)
    x2d = x.reshape(N_TOK, D)
    pm_l, dt_l = _inproj_call(x2d, mod6, n1, w_main, w_dt, lambda i: i // (SEQ // TM_INPROJ))

    h0 = _ctx_state_call(pm_c, dt_c, ssd_consts)
    dsk = jnp.repeat(d_skip[0], SSD_P)[None, :]
    bs_e = jnp.repeat(jnp.swapaxes(b_spatial[0], 0, 1), CM_HD, axis=1)
    ycat = _ssd_main_call(pm_l, dt_l, ssd_consts, h0, dsk, ssd_norm_w[0][None, :], cmlp_norm_w[0][None, :],
                          w_spatial[0].astype(bf16), bs_e)

    base, h2, logits_t = _outproj_call(ycat, x2d, mod6, norm2_w[0][None, :], w_out[0].astype(bf16),
                                       jnp.swapaxes(w_router[0], 0, 1).astype(bf16),
                                       w_shared_gate_up[0].astype(bf16), w_shared_down[0].astype(bf16))
    h2 = h2.reshape(N_TOK, ROW_CHUNKS, LANES)

    dest, gates, bexp, bend, nval = _route_call(logits_t, router_bias[0][:, None])
    bend = bend[:, 0]
    nact = bend[N_EXP - 1:]
    dest3 = _tile_major(dest, TT_COMB)
    slot_tok = _slots_call(dest3)
    ys = _experts_call(bexp.reshape(N_BLOCKS), bend, nval.reshape(N_BLOCKS), nact,
                       slot_tok.reshape(N_BLOCKS, 1, ROWS_PER_BLOCK),
                       h2, w_gate_up[0], w_down[0]).reshape(N_BLOCKS * ROWS_PER_BLOCK, ROW_CHUNKS, LANES)
    out = _combine_call(dest3, ys, gates.T, base, mod6, final_norm_w[None, :])
    return out.reshape(NB_BATCH, SEQ, D)
```

```python
import functools

import jax
import jax.numpy as jnp
from jax import lax
from jax.experimental import pallas as pl
from jax.experimental.pallas import tpu as pltpu

f32 = jnp.float32
bf16 = jnp.bfloat16
i32 = jnp.int32

D = 2048
NB_BATCH = 2
SEQ = 4096
CTX = 256
N_MOD = 6
SSD_W = 1024
SSD_H = 16
SSD_P = 64
SSD_G = 2
D_STATE = 128
CHUNK = 128
D_CONV = 4
BC_W = 2 * SSD_G * D_STATE
CM_W = 1024
CM_H = 8
CM_HD = 128
OFF_Z = SSD_W
OFF_XBC = OFF_Z + SSD_W + BC_W
OFF_DTF = OFF_XBC + SSD_H
OFF_DTB = OFF_DTF + SSD_H
OFF_U = OFF_DTB + CM_W
IN_PROJ_DIM = OFF_U + CM_W
N_EXP = 256
TOP_K = 8
N_GRP = 8
TOP_GRP = 4
GRP_SZ = N_EXP // N_GRP
D_EXP = 512
ROUTED_SCALE = 2.5
EPS = 1e-6

P_MAIN = 4 * 1024 + BC_W
DT_W = 128

LANES = 128
SUBLANES = 8
VMEM_LIMIT = 56 * 1024 * 1024

TM_INPROJ = 1024
TN_INPROJ = 1536
TM_OUT = 256
TM_SHARED = 512
TT_ROUTE = 512
ROWS_PER_BLOCK = 256
N_TOK = NB_BATCH * SEQ
N_BLOCKS = N_TOK * TOP_K // ROWS_PER_BLOCK + N_EXP
TT_COMB = 128


def _cparams(sem):
    return pltpu.CompilerParams(dimension_semantics=sem, vmem_limit_bytes=VMEM_LIMIT)


def _sigmoid(x):
    return 0.5 * jnp.tanh(0.5 * x) + 0.5


def _silu(x):
    return x * _sigmoid(x)


def _dot(a, b):
    return jnp.dot(a, b, preferred_element_type=f32)


def _mod_kernel(c_ref, w_ref, b_ref, o_ref):
    a = _silu(c_ref[...]).astype(bf16)
    o_ref[...] = _dot(a, w_ref[...].astype(bf16)) + b_ref[...]


def _mod_call(cs, w_mod, b_mod):
    tn = 1024
    n = w_mod.shape[1]
    return pl.pallas_call(
        _mod_kernel,
        grid=(n // tn,),
        in_specs=[
            pl.BlockSpec((SUBLANES, D), lambda j: (0, 0)),
            pl.BlockSpec((D, tn), lambda j: (0, j)),
            pl.BlockSpec((1, tn), lambda j: (0, j)),
        ],
        out_specs=pl.BlockSpec((SUBLANES, tn), lambda j: (0, j)),
        out_shape=jax.ShapeDtypeStruct((SUBLANES, n), f32),
        compiler_params=_cparams(("arbitrary",)),
        name="mod",
    )(cs, w_mod, b_mod)


def _inproj_kernel(x_ref, sh_ref, sc_ref, nw_ref, w_ref, wdt_ref, p_ref, dt_ref, h_scr):
    @pl.when(pl.program_id(1) == 0)
    def _():
        x = x_ref[...]
        y = x * lax.rsqrt(jnp.mean(x * x, axis=-1, keepdims=True) + EPS) * nw_ref[...]
        h = (y * (1.0 + sc_ref[0, 0]) + sh_ref[0, 0]).astype(bf16)
        h_scr[...] = h
        dt_ref[...] = _dot(h, wdt_ref[...])

    p_ref[...] = _dot(h_scr[...], w_ref[...])


def _inproj_call(x2d, mod6, norm_w, w_main, w_dt, mod_row_fn):
    m = x2d.shape[0]
    tm, tn = min(TM_INPROJ, m), TN_INPROJ
    return pl.pallas_call(
        _inproj_kernel,
        grid=(m // tm, P_MAIN // tn),
        in_specs=[
            pl.BlockSpec((tm, D), lambda i, j: (i, 0)),
            pl.BlockSpec((1, 1, 1, D), lambda i, j: (mod_row_fn(i), 0, 0, 0)),
            pl.BlockSpec((1, 1, 1, D), lambda i, j: (mod_row_fn(i), 1, 0, 0)),
            pl.BlockSpec((1, D), lambda i, j: (0, 0)),
            pl.BlockSpec((D, tn), lambda i, j: (0, j)),
            pl.BlockSpec((D, DT_W), lambda i, j: (0, 0)),
        ],
        out_specs=[
            pl.BlockSpec((tm, tn), lambda i, j: (i, j)),
            pl.BlockSpec((tm, DT_W), lambda i, j: (i, 0)),
        ],
        out_shape=[
            jax.ShapeDtypeStruct((m, P_MAIN), f32),
            jax.ShapeDtypeStruct((m, DT_W), f32),
        ],
        scratch_shapes=[pltpu.VMEM((tm, D), bf16)],
        compiler_params=_cparams(("arbitrary", "arbitrary")),
        name="inproj",
    )(x2d, mod6, mod6, norm_w, w_main, w_dt)


def _split3(q):
    hi = q.astype(bf16)
    r1 = q - hi.astype(f32)
    mid = r1.astype(bf16)
    lo = (r1 - mid.astype(f32)).astype(bf16)
    return hi, mid, lo


def _split2_cat(q):
    hi = q.astype(bf16)
    lo = (q - hi.astype(f32)).astype(bf16)
    return jnp.concatenate([hi, lo], axis=1)


def _conv_silu(buf, w_ref, b_ref):
    w = w_ref[...]
    y = (w[0:1] * buf[7:135, :] + w[1:2] * buf[8:136, :] + w[2:3] * buf[9:137, :]
         + w[3:4] * buf[10:138, :] + b_ref[...])
    return _silu(y)


def _ssd_chunk(ph, first, last, r0, xm_ref, xp_ref, xn_ref, bm_ref, bp_ref, bn_ref, dt_ref,
               cwx_ref, cbx_ref, cwb_ref, cbb_ref, dtb_ref, alog_ref, e2_ref, tri_ref,
               s_scr, xbuf, bcbuf, xs_cache, bc_cache, with_output):
    @pl.when(ph == 0)
    def _():
        zero8x = jnp.zeros((SUBLANES, SSD_W), f32)
        zero8b = jnp.zeros((SUBLANES, BC_W), f32)
        xbuf[0:8, :] = jnp.where(first, zero8x, xp_ref[...])
        xbuf[8:136, :] = xm_ref[...]
        xbuf[136:144, :] = jnp.where(last, zero8x, xn_ref[...])
        bcbuf[0:8, :] = jnp.where(first, zero8b, bp_ref[...])
        bcbuf[8:136, :] = bm_ref[...]
        bcbuf[136:144, :] = jnp.where(last, zero8b, bn_ref[...])
        xs_cache[pl.ds(r0, CHUNK), :] = _conv_silu(xbuf, cwx_ref, cbx_ref).astype(bf16)
        bc_cache[pl.ds(r0, CHUNK), :] = _conv_silu(bcbuf, cwb_ref, cbb_ref).astype(bf16)

    xs = xs_cache[pl.ds(r0, CHUNK), :].astype(f32)
    bcv = bc_cache[pl.ds(r0, CHUNK), :]

    dtr = dt_ref[...]
    dtr = jnp.where(ph == 0, dtr, pltpu.roll(dtr, LANES - SSD_H, axis=1))
    pre = dtr + dtb_ref[0]
    dt = jnp.maximum(pre, 0.0) + jnp.log(1.0 + jnp.exp(-jnp.abs(pre)))
    d_a = dt * (-jnp.exp(alog_ref[0]))

    tri = tri_ref[0]
    mask = tri > 0
    col = lax.broadcasted_iota(i32, (CHUNK, CHUNK), 1)
    hi, mid, lo = _split3(d_a)
    acs = _dot(tri, hi) + _dot(tri, mid) + _dot(tri, lo)
    tot = jnp.sum(d_a, axis=0, keepdims=True)
    e_a = jnp.exp(acs)
    dte = jnp.exp(tot - acs)
    cd = jnp.broadcast_to(jnp.exp(tot), (SUBLANES, LANES))
    ex = _dot(jnp.concatenate([_split2_cat(dt), _split2_cat(e_a), _split2_cat(dte), _split2_cat(cd)], axis=0),
              e2_ref[...])
    dt_e = ex[0:128]
    ea_e = ex[128:256]
    dte_e = ex[256:384]
    cd_e = ex[384:385]
    xdt = xs * dt_e

    y_parts = []
    if with_output:
        acs_t = acs.T
        lane_lo = col < SSD_P
        g_mats = []
        for g in range(SSD_G):
            b_g = bcv[:, g * D_STATE:(g + 1) * D_STATE]
            c_g = bcv[:, (SSD_G + g) * D_STATE:(SSD_G + g + 1) * D_STATE]
            g_mats.append(lax.dot_general(c_g, b_g, (((1,), (1,)), ((), ())), preferred_element_type=f32))
        for p in range(SSD_H // 2):
            g = (2 * p) // (SSD_H // SSD_G)
            xp = xdt[:, p * LANES:(p + 1) * LANES]
            acc = None
            for k in range(2):
                h = 2 * p + k
                seg = jnp.broadcast_to(acs[:, h:h + 1], (CHUNK, CHUNK)) - jnp.broadcast_to(acs_t[h:h + 1, :], (CHUNK, CHUNK))
                dec = jnp.where(mask, jnp.exp(jnp.minimum(seg, 0.0)), 0.0)
                m_h = (g_mats[g] * dec).astype(bf16)
                x_h = jnp.where(lane_lo if k == 0 else jnp.logical_not(lane_lo), xp, 0.0).astype(bf16)
                t = _dot(m_h, x_h)
                acc = t if acc is None else acc + t
            y_parts.append(acc)

    gw = SSD_W // SSD_G
    y_off = []
    for g in range(SSD_G):
        sl = slice(g * gw, (g + 1) * gw)
        b_g = bcv[:, g * D_STATE:(g + 1) * D_STATE]
        s_g = s_scr[g]
        if with_output:
            c_g = bcv[:, (SSD_G + g) * D_STATE:(SSD_G + g + 1) * D_STATE]
            y_off.append(_dot(c_g, s_g.astype(bf16)) * ea_e[:, sl])
        upd = lax.dot_general(b_g, (xdt[:, sl] * dte_e[:, sl]).astype(bf16), (((0,), (0,)), ((), ())),
                              preferred_element_type=f32)
        s_scr[g] = s_g * cd_e[:, sl] + upd
    if not with_output:
        return xs, None
    y = jnp.concatenate(y_parts, axis=1) + jnp.concatenate(y_off, axis=1)
    return xs, y


def _ctx_state_kernel(xm_ref, xp_ref, xn_ref, bm_ref, bp_ref, bn_ref, dt_ref,
                      cwx_ref, cbx_ref, cwb_ref, cbb_ref, dtb_ref, alog_ref, e2_ref, tri_ref,
                      h_ref, s_scr, xbuf, bcbuf, xs_cache, bc_cache, *, nc):
    ph = pl.program_id(1)
    c = pl.program_id(2)
    ci = jnp.where(ph == 0, c, nc - 1 - c)

    @pl.when(c == 0)
    def _():
        s_scr[...] = jnp.zeros_like(s_scr)

    _ssd_chunk(ph, ci == 0, ci == nc - 1, pl.multiple_of(ci * CHUNK, CHUNK),
               xm_ref, xp_ref, xn_ref, bm_ref, bp_ref, bn_ref, dt_ref,
               cwx_ref, cbx_ref, cwb_ref, cbb_ref, dtb_ref, alog_ref, e2_ref, tri_ref,
               s_scr, xbuf, bcbuf, xs_cache, bc_cache, False)

    @pl.when(c == nc - 1)
    def _():
        h_ref[0, 0] = s_scr[...]


def _gelu_tanh(x):
    return 0.5 * x * (1.0 + jnp.tanh(0.7978845608028654 * (x + 0.044715 * x * x * x)))


def _ssd_main_kernel(xm_ref, xp_ref, xn_ref, bm_ref, bp_ref, bn_ref, dt_ref,
                     cwx_ref, cbx_ref, cwb_ref, cbb_ref, dtb_ref, alog_ref, e2_ref, tri_ref,
                     h0_ref, z_ref, u_ref, v_ref, dsk_ref, snw_ref, cnw_ref, ws_ref, bs_ref,
                     o_ref, s_scr, xbuf, bcbuf, xs_cache, bc_cache, yf_scr, *, nc):
    ph = pl.program_id(1)
    c = pl.program_id(2)
    ci = jnp.where(ph == 0, c, nc - 1 - c)
    r0 = pl.multiple_of(ci * CHUNK, CHUNK)

    @pl.when(c == 0)
    def _():
        s_scr[...] = h0_ref[0, 0]

    xs, y = _ssd_chunk(ph, ci == 0, ci == nc - 1, r0, xm_ref, xp_ref, xn_ref, bm_ref, bp_ref, bn_ref, dt_ref,
                       cwx_ref, cbx_ref, cwb_ref, cbb_ref, dtb_ref, alog_ref, e2_ref, tri_ref,
                       s_scr, xbuf, bcbuf, xs_cache, bc_cache, True)

    @pl.when(ph == 0)
    def _():
        yf_scr[pl.ds(r0, CHUNK), :] = y

    @pl.when(ph == 1)
    def _():
        yt = (yf_scr[pl.ds(r0, CHUNK), :] + y + dsk_ref[...] * xs) * _silu(z_ref[...])
        yt = yt * lax.rsqrt(jnp.mean(yt * yt, axis=-1, keepdims=True) + EPS) * snw_ref[...]
        o_ref[:, 0:SSD_W] = yt.astype(bf16)
        u = _gelu_tanh(u_ref[...])
        v = _gelu_tanh(v_ref[...])
        vn = (v * lax.rsqrt(jnp.mean(v * v, axis=-1, keepdims=True) + EPS) * cnw_ref[...]).astype(bf16)
        mixed = [_dot(ws_ref[h], vn[:, h * CM_HD:(h + 1) * CM_HD]) for h in range(CM_H)]
        o_ref[:, SSD_W:SSD_W + CM_W] = (u * (jnp.concatenate(mixed, axis=1) + bs_ref[...])).astype(bf16)


def _ssd_common_specs(nc, row_blocks_total):
    cpb = nc
    sub = CHUNK // SUBLANES

    def dt_i(b, ph, c):
        return b * cpb + jnp.where(ph == 0, c, nc - 1 - c)

    def main_i(b, ph, c):
        return b * cpb + jnp.where(ph == 0, c, nc - 1)

    def prev_i(b, ph, c):
        return jnp.maximum(main_i(b, ph, c) * sub - 1, 0)

    def next_i(b, ph, c):
        return jnp.minimum(main_i(b, ph, c) * sub + sub, row_blocks_total * sub - 1)

    bc_col = (4 * 1024) // BC_W
    specs = [
        pl.BlockSpec((CHUNK, SSD_W), lambda b, ph, c: (main_i(b, ph, c), 0)),
        pl.BlockSpec((SUBLANES, SSD_W), lambda b, ph, c: (prev_i(b, ph, c), 0)),
        pl.BlockSpec((SUBLANES, SSD_W), lambda b, ph, c: (next_i(b, ph, c), 0)),
        pl.BlockSpec((CHUNK, BC_W), lambda b, ph, c: (main_i(b, ph, c), bc_col)),
        pl.BlockSpec((SUBLANES, BC_W), lambda b, ph, c: (prev_i(b, ph, c), bc_col)),
        pl.BlockSpec((SUBLANES, BC_W), lambda b, ph, c: (next_i(b, ph, c), bc_col)),
        pl.BlockSpec((CHUNK, DT_W), lambda b, ph, c: (dt_i(b, ph, c), 0)),
        pl.BlockSpec((D_CONV, SSD_W), lambda b, ph, c: (0, 0)),
        pl.BlockSpec((1, SSD_W), lambda b, ph, c: (0, 0)),
        pl.BlockSpec((D_CONV, BC_W), lambda b, ph, c: (0, 0)),
        pl.BlockSpec((1, BC_W), lambda b, ph, c: (0, 0)),
        pl.BlockSpec((1, 1, DT_W), lambda b, ph, c: (ph, 0, 0)),
        pl.BlockSpec((1, 1, DT_W), lambda b, ph, c: (ph, 0, 0)),
        pl.BlockSpec((2 * LANES, SSD_W), lambda b, ph, c: (0, 0)),
        pl.BlockSpec((1, CHUNK, CHUNK), lambda b, ph, c: (ph, 0, 0)),
    ]
    return specs


def _ssd_scratch(seq_rows):
    return [
        pltpu.VMEM((SSD_G, D_STATE, SSD_W // SSD_G), f32),
        pltpu.VMEM((CHUNK + 2 * SUBLANES, SSD_W), f32),
        pltpu.VMEM((CHUNK + 2 * SUBLANES, BC_W), f32),
        pltpu.VMEM((seq_rows, SSD_W), bf16),
        pltpu.VMEM((seq_rows, BC_W), bf16),
    ]


def _ctx_state_call(pm, dtm, ssd_consts):
    nc = CTX // CHUNK
    specs = _ssd_common_specs(nc, NB_BATCH * nc)
    return pl.pallas_call(
        functools.partial(_ctx_state_kernel, nc=nc),
        grid=(NB_BATCH, 2, nc),
        in_specs=specs,
        out_specs=pl.BlockSpec((1, 1, SSD_G, D_STATE, SSD_W // SSD_G), lambda b, ph, c: (b, ph, 0, 0, 0)),
        out_shape=jax.ShapeDtypeStruct((NB_BATCH, 2, SSD_G, D_STATE, SSD_W // SSD_G), f32),
        scratch_shapes=_ssd_scratch(CTX),
        compiler_params=_cparams(("arbitrary", "arbitrary", "arbitrary")),
        name="ssd_ctx",
    )(pm, pm, pm, pm, pm, pm, dtm, *ssd_consts)


def _ssd_main_call(pm, dtm, ssd_consts, h0, dsk, snw, cnw, ws, bs_e):
    nc = SEQ // CHUNK
    specs = _ssd_common_specs(nc, NB_BATCH * nc)

    def zrow(b, ph, c):
        return b * nc + jnp.where(ph == 0, nc - 1, nc - 1 - c)

    specs += [
        pl.BlockSpec((1, 1, SSD_G, D_STATE, SSD_W // SSD_G), lambda b, ph, c: (b, ph, 0, 0, 0)),
        pl.BlockSpec((CHUNK, SSD_W), lambda b, ph, c: (zrow(b, ph, c), 1)),
        pl.BlockSpec((CHUNK, CM_W), lambda b, ph, c: (zrow(b, ph, c), 2)),
        pl.BlockSpec((CHUNK, CM_W), lambda b, ph, c: (zrow(b, ph, c), 3)),
        pl.BlockSpec((1, SSD_W), lambda b, ph, c: (0, 0)),
        pl.BlockSpec((1, SSD_W), lambda b, ph, c: (0, 0)),
        pl.BlockSpec((1, CM_W), lambda b, ph, c: (0, 0)),
        pl.BlockSpec((CM_H, CHUNK, CHUNK), lambda b, ph, c: (0, 0, 0)),
        pl.BlockSpec((CHUNK, CM_W), lambda b, ph, c: (0, 0)),
    ]
    return pl.pallas_call(
        functools.partial(_ssd_main_kernel, nc=nc),
        grid=(NB_BATCH, 2, nc),
        in_specs=specs,
        out_specs=pl.BlockSpec((CHUNK, SSD_W + CM_W), lambda b, ph, c: (zrow(b, ph, c), 0)),
        out_shape=jax.ShapeDtypeStruct((N_TOK, SSD_W + CM_W), bf16),
        scratch_shapes=_ssd_scratch(SEQ) + [pltpu.VMEM((SEQ, SSD_W), f32)],
        compiler_params=_cparams(("arbitrary", "arbitrary", "arbitrary")),
        name="ssd_main",
    )(pm, pm, pm, pm, pm, pm, dtm, *ssd_consts, h0, pm, pm, pm, dsk, snw, cnw, ws, bs_e)


ROW_CHUNKS = D // LANES
ROW_DT = f32


def _store_token_major(ref, val, n_rows, row0=0):
    for c in range(ROW_CHUNKS):
        ref[pl.ds(row0 + c, n_rows, stride=ROW_CHUNKS), :] = val[:, c * LANES:(c + 1) * LANES].astype(ref.dtype)


def _load_token_major_chunk(ref, lead, c, n_rows, row0=0):
    idx = pl.ds(row0 + c, n_rows, stride=ROW_CHUNKS)
    return ref[idx, :] if lead is None else ref[lead, idx, :]


def _outproj_kernel(y_ref, x_ref, g_ref, sh_ref, sc_ref, g5_ref, nw_ref, wo_ref, wr_ref, w1_ref, w2_ref,
                    base_ref, h_ref, lg_ref):
    x1 = x_ref[...] + g_ref[0, 0] * _dot(y_ref[...], wo_ref[...])
    y = x1 * lax.rsqrt(jnp.mean(x1 * x1, axis=-1, keepdims=True) + EPS) * nw_ref[...]
    h = y * (1.0 + sc_ref[0, 0]) + sh_ref[0, 0]
    _store_token_major(h_ref, h, h.shape[0])
    hb = h.astype(bf16)
    lg_ref[...] = lax.dot_general(wr_ref[...], hb, (((1,), (1,)), ((), ())), preferred_element_type=f32)
    gu = _dot(hb, w1_ref[...])
    act = (_silu(gu[:, :D_EXP]) * gu[:, D_EXP:]).astype(bf16)
    base_ref[...] = x1 + g5_ref[0, 0] * _dot(act, w2_ref[...])


def _outproj_call(ycat, x2d, mod6, norm_w, w_out, w_router_t, w1, w2):
    tm = TM_OUT
    rb = SEQ // tm
    const = lambda i: (0, 0)
    return pl.pallas_call(
        _outproj_kernel,
        grid=(N_TOK // tm,),
        in_specs=[
            pl.BlockSpec((tm, D), lambda i: (i, 0)),
            pl.BlockSpec((tm, D), lambda i: (i, 0)),
            pl.BlockSpec((1, 1, 1, D), lambda i: (i // rb, 2, 0, 0)),
            pl.BlockSpec((1, 1, 1, D), lambda i: (i // rb, 3, 0, 0)),
            pl.BlockSpec((1, 1, 1, D), lambda i: (i // rb, 4, 0, 0)),
            pl.BlockSpec((1, 1, 1, D), lambda i: (i // rb, 5, 0, 0)),
            pl.BlockSpec((1, D), const),
            pl.BlockSpec((D, D), const),
            pl.BlockSpec((N_EXP, D), const),
            pl.BlockSpec((D, 2 * D_EXP), const),
            pl.BlockSpec((D_EXP, D), const),
        ],
        out_specs=[
            pl.BlockSpec((tm, D), lambda i: (i, 0)),
            pl.BlockSpec((tm * ROW_CHUNKS, LANES), lambda i: (i, 0)),
            pl.BlockSpec((N_EXP, tm), lambda i: (0, i)),
        ],
        out_shape=[
            jax.ShapeDtypeStruct((N_TOK, D), f32),
            jax.ShapeDtypeStruct((N_TOK * ROW_CHUNKS, LANES), ROW_DT),
            jax.ShapeDtypeStruct((N_EXP, N_TOK), f32),
        ],
        compiler_params=_cparams(("arbitrary",)),
        name="outproj",
    )(ycat, x2d, mod6, mod6, mod6, mod6, norm_w, w_out, w_router_t, w1, w2)


def _route_kernel(lg_ref, bias_ref, dest_ref, gate_ref, bexp_ref, bend_ref, nval_ref,
                  idx_scr, pos_scr, cnt_scr, *, n_tiles):
    j = pl.program_id(0)
    tt = TT_ROUTE
    neg = -jnp.inf

    @pl.when(j == 0)
    def _():
        cnt_scr[...] = jnp.zeros_like(cnt_scr)

    s = _sigmoid(lg_ref[...])
    biased = s + bias_ref[...]
    v3 = biased.reshape(N_GRP, GRP_SZ, tt)
    io3 = lax.broadcasted_iota(i32, (N_GRP, GRP_SZ, tt), 1)
    m1 = jnp.max(v3, axis=1, keepdims=True)
    i1 = jnp.min(jnp.where(v3 == m1, io3, GRP_SZ), axis=1, keepdims=True)
    m2 = jnp.max(jnp.where(io3 == i1, neg, v3), axis=1, keepdims=True)
    gs = m1 + m2
    gio = lax.broadcasted_iota(i32, (N_GRP, 1, tt), 0)
    sel = jnp.zeros((N_GRP, 1, tt), dtype=jnp.bool_)
    cur_g = gs
    for _ in range(TOP_GRP):
        m = jnp.max(cur_g, axis=0, keepdims=True)
        gi = jnp.min(jnp.where(cur_g == m, gio, N_GRP), axis=0, keepdims=True)
        hit = gio == gi
        sel = jnp.logical_or(sel, hit)
        cur_g = jnp.where(hit, neg, cur_g)
    cur = jnp.where(sel, v3, neg).reshape(N_EXP, tt)

    eio = lax.broadcasted_iota(i32, (N_EXP, tt), 0)
    onehot = jnp.zeros((N_EXP, tt), f32)
    idxs, gts = [], []
    for _ in range(TOP_K):
        m = jnp.max(cur, axis=0, keepdims=True)
        ei = jnp.min(jnp.where(cur == m, eio, N_EXP), axis=0, keepdims=True)
        hit = eio == ei
        gts.append(jnp.sum(jnp.where(hit, s, 0.0), axis=0, keepdims=True))
        idxs.append(ei)
        onehot = jnp.where(hit, 1.0, onehot)
        cur = jnp.where(hit, neg, cur)
    gsum = gts[0]
    for k in range(1, TOP_K):
        gsum = gsum + gts[k]

    r_io = lax.broadcasted_iota(i32, (tt, tt), 0)
    c_io = lax.broadcasted_iota(i32, (tt, tt), 1)
    upper = (r_io <= c_io).astype(f32).astype(bf16)
    oh = onehot.astype(bf16)
    incl = _dot(oh, upper)
    base = cnt_scr[...]
    pos = incl - 1.0 + jnp.concatenate([base] * (tt // LANES), axis=1)
    cnt_scr[...] = base + _dot(oh, jnp.ones((tt, LANES), bf16))
    c0 = pl.multiple_of(j * tt, tt)
    for k in range(TOP_K):
        hit = eio == idxs[k]
        idx_scr[k:k + 1, pl.ds(c0, tt)] = idxs[k]
        pos_scr[k:k + 1, pl.ds(c0, tt)] = jnp.sum(jnp.where(hit, pos, 0.0), axis=0, keepdims=True)
        gate_ref[k:k + 1, pl.ds(c0, tt)] = gts[k] / gsum * ROUTED_SCALE

    @pl.when(j == n_tiles - 1)
    def _():
        cnt = cnt_scr[...]
        r = float(ROWS_PER_BLOCK)
        nblk = jnp.floor((cnt + (r - 1.0)) * (1.0 / r))
        er = lax.broadcasted_iota(i32, (N_EXP, N_EXP), 0)
        ec = lax.broadcasted_iota(i32, (N_EXP, N_EXP), 1)
        lower = (er >= ec).astype(f32).astype(bf16)
        bend = _dot(lower, nblk.astype(bf16))
        pstart = (bend - nblk) * r
        pstart_t = jnp.concatenate([pstart] * (tt // LANES), axis=1)
        def slots(t, carry):
            c1 = pl.multiple_of(t * tt, tt)
            for k in range(TOP_K):
                hit = eio == idx_scr[k:k + 1, pl.ds(c1, tt)]
                d = jnp.sum(jnp.where(hit, pstart_t, 0.0), axis=0, keepdims=True) + pos_scr[k:k + 1, pl.ds(c1, tt)]
                dest_ref[k:k + 1, pl.ds(c1, tt)] = d.astype(i32)
            return carry

        lax.fori_loop(0, n_tiles, slots, 0)
        bio = lax.broadcasted_iota(i32, (N_EXP, N_BLOCKS), 1).astype(f32)
        bend_t = jnp.concatenate([bend] * (N_BLOCKS // LANES), axis=1)
        be = jnp.sum(jnp.where(bend_t <= bio, 1.0, 0.0), axis=0, keepdims=True)
        bexp_ref[...] = jnp.minimum(be, float(N_EXP - 1)).astype(i32)
        bend_ref[...] = bend.astype(i32)
        cnt_t = jnp.concatenate([cnt] * (N_BLOCKS // LANES), axis=1)
        pst_b = jnp.concatenate([pstart] * (N_BLOCKS // LANES), axis=1)
        lo = jnp.maximum(pst_b, bio * r)
        hi = jnp.minimum(pst_b + cnt_t, (bio + 1.0) * r)
        nval_ref[...] = jnp.sum(jnp.maximum(hi - lo, 0.0), axis=0, keepdims=True).astype(i32)


def _route_call(logits_t, bias_col):
    tt = TT_ROUTE
    n_tiles = N_TOK // tt
    return pl.pallas_call(
        functools.partial(_route_kernel, n_tiles=n_tiles),
        grid=(n_tiles,),
        in_specs=[
            pl.BlockSpec((N_EXP, tt), lambda j: (0, j)),
            pl.BlockSpec((N_EXP, 1), lambda j: (0, 0)),
        ],
        out_specs=[
            pl.BlockSpec((TOP_K, N_TOK), lambda j: (0, 0)),
            pl.BlockSpec((TOP_K, N_TOK), lambda j: (0, 0)),
            pl.BlockSpec((1, N_BLOCKS), lambda j: (0, 0)),
            pl.BlockSpec((N_EXP, LANES), lambda j: (0, 0)),
            pl.BlockSpec((1, N_BLOCKS), lambda j: (0, 0)),
        ],
        out_shape=[
            jax.ShapeDtypeStruct((TOP_K, N_TOK), i32),
            jax.ShapeDtypeStruct((TOP_K, N_TOK), f32),
            jax.ShapeDtypeStruct((1, N_BLOCKS), i32),
            jax.ShapeDtypeStruct((N_EXP, LANES), i32),
            jax.ShapeDtypeStruct((1, N_BLOCKS), i32),
        ],
        scratch_shapes=[
            pltpu.VMEM((TOP_K, N_TOK), i32),
            pltpu.VMEM((TOP_K, N_TOK), f32),
            pltpu.VMEM((N_EXP, LANES), f32),
        ],
        compiler_params=_cparams(("arbitrary",)),
        name="route",
    )(logits_t, bias_col)


def _slots_kernel(dest_ref, tab_ref):
    i = pl.program_id(0)
    tt = TT_COMB

    @pl.when(i == 0)
    def _():
        def zero_row(b, c):
            b0 = b * ROWS_PER_BLOCK
            for l in range(ROWS_PER_BLOCK):
                tab_ref[b0 + l] = 0
            return c
        lax.fori_loop(0, N_BLOCKS, zero_row, 0)

    base = i * tt
    for rr in range(tt):
        for k in range(TOP_K):
            tab_ref[dest_ref[0, 0, k * tt + rr]] = base + rr


def _slots_call(dest3):
    tt = TT_COMB
    return pl.pallas_call(
        _slots_kernel,
        grid=(N_TOK // tt,),
        in_specs=[pl.BlockSpec((1, 1, TOP_K * tt), lambda i: (i, 0, 0), memory_space=pltpu.SMEM)],
        out_specs=pl.BlockSpec(memory_space=pltpu.SMEM),
        out_shape=jax.ShapeDtypeStruct((N_BLOCKS * ROWS_PER_BLOCK,), i32),
        compiler_params=_cparams(("arbitrary",)),
        name="slots",
    )(dest3)


GATHER_GROUP = 16


def _gather_rows(idx_ref, src_hbm, dst, sem, n_rows, n_valid=None):
    def issue(lo, hi):
        for r in range(lo, hi):
            t = idx_ref[0, 0, r]
            pltpu.make_async_copy(src_hbm.at[t], dst.at[pl.ds(r * ROW_CHUNKS, ROW_CHUNKS), :], sem).start(
                priority=r % 2)

    if n_valid is None:
        issue(0, n_rows)
        return
    for lo in range(0, n_rows, GATHER_GROUP):
        pl.when(lo < n_valid)(functools.partial(issue, lo, lo + GATHER_GROUP))


def _gather_rows_wait(src_hbm, dst, sem, n_rows, n_valid=None):
    def body(r, carry):
        pltpu.make_async_copy(src_hbm.at[0], dst.at[pl.ds(0, ROW_CHUNKS), :], sem).wait()
        return carry
    if n_valid is not None:
        n_rows = ((n_valid + (GATHER_GROUP - 1)) // GATHER_GROUP) * GATHER_GROUP
    lax.fori_loop(0, n_rows, body, 0, unroll=8 if n_valid is None else 1)


W_CHUNKS = 4
W_SLOTS = 3


def _experts_kernel(bexp_ref, bend_ref, nval_ref, nact_ref, tok0_ref, tokn_ref, h_hbm, wgu_hbm, wd_hbm, o_ref,
                    xbuf, xsem, wgu_buf, wd_buf, wsem, ord_scr):
    g = pl.program_id(0)
    na = nact_ref[0]
    e = bexp_ref[g]
    r = ROWS_PER_BLOCK
    xslot = g % 2

    def fetch(ex, slot):
        cps = []
        for q in range(W_CHUNKS):
            a = D // W_CHUNKS
            cps.append(pltpu.make_async_copy(wgu_hbm.at[ex, pl.ds(q * a, a), :],
                                             wgu_buf.at[slot, pl.ds(q * a, a), :], wsem.at[slot]))
            a = D_EXP // W_CHUNKS
            cps.append(pltpu.make_async_copy(wd_hbm.at[ex, pl.ds(q * a, a), :],
                                             wd_buf.at[slot, pl.ds(q * a, a), :], wsem.at[slot]))
        return cps

    def start_fetch(ex, slot):
        for n, cp in enumerate(fetch(ex, slot)):
            cp.start(priority=(n // 2) % 2)

    def next_expert_block(ex):
        return bend_ref[ex]

    def expert_of(blk):
        return bexp_ref[jnp.minimum(blk, N_BLOCKS - 1)]

    @pl.when(g == 0)
    def _():
        ord_scr[0] = 0
        start_fetch(e, 0)
        n1 = next_expert_block(e)

        @pl.when(n1 < na)
        def _():
            start_fetch(expert_of(n1), 1)

        xbuf[...] = jnp.zeros_like(xbuf)
        _gather_rows(tok0_ref, h_hbm, xbuf.at[0], xsem.at[0], r, n_valid=nval_ref[0])

    @pl.when(g + 1 < na)
    def _():
        _gather_rows(tokn_ref, h_hbm, xbuf.at[1 - xslot], xsem.at[1 - xslot], r,
                     n_valid=nval_ref[jnp.minimum(g + 1, N_BLOCKS - 1)])

    first = jnp.logical_or(g == 0, e != bexp_ref[jnp.maximum(g - 1, 0)])

    @pl.when(jnp.logical_and(g < na, first))
    def _():
        k = ord_scr[0]
        for cp in fetch(e, k % W_SLOTS):
            cp.wait()
        n1 = next_expert_block(e)

        @pl.when(n1 < na)
        def _():
            n2 = next_expert_block(expert_of(n1))

            @pl.when(n2 < na)
            def _():
                start_fetch(expert_of(n2), (k + 2) % W_SLOTS)

        ord_scr[0] = k + 1

    @pl.when(g < na)
    def _():
        _gather_rows_wait(h_hbm, xbuf.at[xslot], xsem.at[xslot], r, n_valid=nval_ref[g])
        ws = (ord_scr[0] + W_SLOTS - 1) % W_SLOTS
        dn = (((1,), (0,)), ((), ()))
        gu = None
        for p in range(ROW_CHUNKS // 2):
            xp = jnp.concatenate([_load_token_major_chunk(xbuf, xslot, 2 * p, r),
                                  _load_token_major_chunk(xbuf, xslot, 2 * p + 1, r)], axis=1).astype(bf16)
            t = lax.dot_general(xp, wgu_buf[ws, 2 * p * LANES:(2 * p + 2) * LANES, :], dn, preferred_element_type=f32)
            gu = t if gu is None else gu + t
        act = (_silu(gu[:, :D_EXP]) * gu[:, D_EXP:]).astype(bf16)
        _store_token_major(o_ref, lax.dot_general(act, wd_buf[ws], dn, preferred_element_type=f32), r)

    @pl.when(g >= na)
    def _():
        o_ref[...] = jnp.zeros_like(o_ref)


def _experts_call(bexp, bend, nval, nact, slot_tok3, h2, w_gate_up, w_down):
    r = ROWS_PER_BLOCK
    nb = N_BLOCKS
    grid_spec = pltpu.PrefetchScalarGridSpec(
        num_scalar_prefetch=4,
        grid=(nb,),
        in_specs=[
            pl.BlockSpec((1, 1, r), lambda g, *_: (0, 0, 0), memory_space=pltpu.SMEM),
            pl.BlockSpec((1, 1, r), lambda g, *_: (jnp.minimum(g + 1, nb - 1), 0, 0), memory_space=pltpu.SMEM),
            pl.BlockSpec(memory_space=pl.ANY),
            pl.BlockSpec(memory_space=pl.ANY),
            pl.BlockSpec(memory_space=pl.ANY),
        ],
        out_specs=pl.BlockSpec((r * ROW_CHUNKS, LANES), lambda g, *_: (g, 0)),
        scratch_shapes=[
            pltpu.VMEM((2, r * ROW_CHUNKS, LANES), ROW_DT),
            pltpu.SemaphoreType.DMA((2,)),
            pltpu.VMEM((W_SLOTS, D, 2 * D_EXP), f32),
            pltpu.VMEM((W_SLOTS, D_EXP, D), f32),
            pltpu.SemaphoreType.DMA((W_SLOTS,)),
            pltpu.SMEM((1,), i32),
        ],
    )
    return pl.pallas_call(
        _experts_kernel,
        grid_spec=grid_spec,
        out_shape=jax.ShapeDtypeStruct((nb * r * ROW_CHUNKS, LANES), ROW_DT),
        compiler_params=_cparams(("arbitrary",)),
        name="experts",
    )(bexp, bend, nval, nact, slot_tok3, slot_tok3, h2, w_gate_up, w_down)


def _combine_kernel(d0_ref, dn_ref, ys_hbm, gate_ref, base_ref, g5_ref, nw_ref, o_ref, buf, sem, *, n_tiles):
    i = pl.program_id(0)
    tt = TT_COMB
    n_rows = TOP_K * tt
    slot = i % 2

    @pl.when(i == 0)
    def _():
        _gather_rows(d0_ref, ys_hbm, buf.at[0], sem.at[0], n_rows)

    @pl.when(i + 1 < n_tiles)
    def _():
        _gather_rows(dn_ref, ys_hbm, buf.at[1 - slot], sem.at[1 - slot], n_rows)

    _gather_rows_wait(ys_hbm, buf.at[slot], sem.at[slot], n_rows)
    g = gate_ref[...]
    gcols = [jnp.broadcast_to(g[:, k:k + 1], (tt, LANES)) for k in range(TOP_K)]
    parts = []
    for c in range(ROW_CHUNKS):
        a = gcols[0] * _load_token_major_chunk(buf, slot, c, tt).astype(f32)
        for k in range(1, TOP_K):
            a = a + gcols[k] * _load_token_major_chunk(buf, slot, c, tt, row0=k * tt * ROW_CHUNKS).astype(f32)
        parts.append(a)
    x2 = base_ref[...] + g5_ref[0, 0] * jnp.concatenate(parts, axis=1)
    o_ref[...] = x2 * lax.rsqrt(jnp.mean(x2 * x2, axis=-1, keepdims=True) + EPS) * nw_ref[...]


def _combine_call(dest3, ys, gates_tk, base, mod6, final_w):
    tt = TT_COMB
    n_tiles = N_TOK // tt
    rb = SEQ // tt
    return pl.pallas_call(
        functools.partial(_combine_kernel, n_tiles=n_tiles),
        grid=(n_tiles,),
        in_specs=[
            pl.BlockSpec((1, 1, TOP_K * tt), lambda i: (0, 0, 0), memory_space=pltpu.SMEM),
            pl.BlockSpec((1, 1, TOP_K * tt), lambda i: (jnp.minimum(i + 1, n_tiles - 1), 0, 0),
                         memory_space=pltpu.SMEM),
            pl.BlockSpec(memory_space=pl.ANY),
            pl.BlockSpec((tt, TOP_K), lambda i: (i, 0)),
            pl.BlockSpec((tt, D), lambda i: (i, 0)),
            pl.BlockSpec((1, 1, 1, D), lambda i: (i // rb, 5, 0, 0)),
            pl.BlockSpec((1, D), lambda i: (0, 0)),
        ],
        out_specs=pl.BlockSpec((tt, D), lambda i: (i, 0)),
        out_shape=jax.ShapeDtypeStruct((N_TOK, D), f32),
        scratch_shapes=[
            pltpu.VMEM((2, TOP_K * tt * ROW_CHUNKS, LANES), ROW_DT),
            pltpu.SemaphoreType.DMA((2,)),
        ],
        compiler_params=_cparams(("arbitrary",)),
        name="combine",
    )(dest3, dest3, ys, gates_tk, base, mod6, final_w)


def _tile_major(a, tt):
    n = a.shape[1] // tt
    return a.reshape(TOP_K, n, tt).transpose(1, 0, 2).reshape(n, 1, TOP_K * tt)


def kernel(x, c, ctx, c_ctx, w_mod, b_mod, norm1_w, w_in, conv_w, conv_b, dt_bias_f, dt_bias_b, a_log_f, a_log_b, d_skip, ssd_norm_w, cmlp_norm_w, w_spatial, b_spatial, w_out, norm2_w, w_router, router_bias, w_gate_up, w_down, w_shared_gate_up, w_shared_down, final_norm_w):
    assert x.shape == (NB_BATCH, SEQ, D) and ctx.shape == (NB_BATCH, CTX, D) and w_mod.shape[0] == 1

    cs = jnp.concatenate([c, c_ctx[None, :], jnp.zeros((SUBLANES - NB_BATCH - 1, D), f32)], axis=0)
    mod6 = _mod_call(cs, w_mod[0], b_mod).reshape(SUBLANES, N_MOD, 1, D)

    wi = w_in[0]
    w_main = jnp.concatenate([wi[:, OFF_Z:OFF_Z + SSD_W], wi[:, 0:OFF_Z], wi[:, OFF_DTB:OFF_U], wi[:, OFF_U:],
                              wi[:, OFF_Z + SSD_W:OFF_XBC]], axis=1).astype(bf16)
    w_dt = jnp.pad(wi[:, OFF_XBC:OFF_DTB], ((0, 0), (0, DT_W - 2 * SSD_H))).astype(bf16)
    pad_h = DT_W - SSD_H
    dtb = jnp.stack([jnp.pad(dt_bias_f[0], (0, pad_h)), jnp.pad(dt_bias_b[0], (0, pad_h))])[:, None, :]
    alog = jnp.stack([jnp.pad(a_log_f[0], (0, pad_h)), jnp.pad(a_log_b[0], (0, pad_h))])[:, None, :]
    e_rows = lax.broadcasted_iota(i32, (2 * LANES, SSD_W), 0) % LANES
    e_cols = lax.broadcasted_iota(i32, (2 * LANES, SSD_W), 1) // SSD_P
    e2 = (e_rows == e_cols).astype(bf16)
    t_rows = lax.broadcasted_iota(i32, (CHUNK, CHUNK), 0)
    t_cols = lax.broadcasted_iota(i32, (CHUNK, CHUNK), 1)
    tri2 = jnp.stack([t_rows >= t_cols, t_rows <= t_cols]).astype(bf16)
    ssd_consts = (conv_w[0][:, :SSD_W], conv_b[0][None, :SSD_W], conv_w[0][:, SSD_W:], conv_b[0][None, SSD_W:],
                  dtb, alog, e2, tri2)

    n1 = norm1_w[0][None, :]
    pm_c, dt_c = _inproj_call(ctx.reshape(NB_BATCH * CTX, D), mod6, n1, w_main, w_dt, lambda i: 2)
    x2d = x.reshape(N_TOK, D)
    pm_l, dt_l = _inproj_call(x2d, mod6, n1, w_main, w_dt, lambda i: i // (SEQ // TM_INPROJ))

    h0 = _ctx_state_call(pm_c, dt_c, ssd_consts)
    dsk = jnp.repeat(d_skip[0], SSD_P)[None, :]
    bs_e = jnp.repeat(jnp.swapaxes(b_spatial[0], 0, 1), CM_HD, axis=1)
    ycat = _ssd_main_call(pm_l, dt_l, ssd_consts, h0, dsk, ssd_norm_w[0][None, :], cmlp_norm_w[0][None, :],
                          w_spatial[0].astype(bf16), bs_e)

    base, h2, logits_t = _outproj_call(ycat, x2d, mod6, norm2_w[0][None, :], w_out[0].astype(bf16),
                                       jnp.swapaxes(w_router[0], 0, 1).astype(bf16),
                                       w_shared_gate_up[0].astype(bf16), w_shared_down[0].astype(bf16))
    h2 = h2.reshape(N_TOK, ROW_CHUNKS, LANES)

    dest, gates, bexp, bend, nval = _route_call(logits_t, router_bias[0][:, None])
    bend = bend[:, 0]
    nact = bend[N_EXP - 1:]
    dest3 = _tile_major(dest, TT_COMB)
    slot_tok = _slots_call(dest3)
    ys = _experts_call(bexp.reshape(N_BLOCKS), bend, nval.reshape(N_BLOCKS), nact,
                       slot_tok.reshape(N_BLOCKS, 1, ROWS_PER_BLOCK),
                       h2, w_gate_up[0], w_down[0]).reshape(N_BLOCKS * ROWS_PER_BLOCK, ROW_CHUNKS, LANES)
    out = _combine_call(dest3, ys, gates.T, base, mod6, final_norm_w[None, :])
    return out.reshape(NB_BATCH, SEQ, D)
```
